```python
import jax, jax.numpy as jnp
from jax import lax
import numpy as np

D_MODEL = 2048
BATCH = 1
SEQ = 16384
DEPTH = 1

CHUNK = 128
SGU_GROUPS = 8
SGU_WIDTH = 1024
SGU_CH = SGU_WIDTH // SGU_GROUPS
ATTN_HEADS = 8
HEAD_DIM = 128
ATTN_WIDTH = ATTN_HEADS * HEAD_DIM
IDX_HEADS = 16
IDX_DIM = 64
TOPK_MAX = 256
Q_BLOCK = 128
D_FF = 5504
N_ADA = 9
EPS = 1e-6
NEG = -1e30
IN_SIZES = (SGU_WIDTH, SGU_WIDTH,
            ATTN_WIDTH, ATTN_WIDTH, ATTN_WIDTH,
            IDX_HEADS * IDX_DIM, IDX_DIM, IDX_HEADS,
            D_MODEL, D_MODEL)
IN_WIDTH = sum(IN_SIZES)

kernel_name = "hybrid_gmlp_dsa_macaron_adaln"


def _rmsnorm(x, g):
    xf = x.astype(jnp.float32)
    y = xf * lax.rsqrt(jnp.mean(xf * xf, axis=-1, keepdims=True) + EPS)
    return (y * g.astype(jnp.float32)).astype(x.dtype)


def _layernorm(x, g, b):
    xf = x.astype(jnp.float32)
    mu = jnp.mean(xf, axis=-1, keepdims=True)
    var = jnp.mean(jnp.square(xf - mu), axis=-1, keepdims=True)
    y = (xf - mu) * lax.rsqrt(var + EPS)
    return (y * g.astype(jnp.float32) + b.astype(jnp.float32)).astype(x.dtype)


def _modulate(n, shift, scale):
    return n * (1 + scale) + shift


def _swiglu(n, w_gate, w_up, w_down):
    return (jax.nn.silu(n @ w_gate) * (n @ w_up)) @ w_down


def _split_cols(z, sizes):
    out, start = [], 0
    for s in sizes:
        out.append(z[..., start:start + s])
        start += s
    return out


def _alibi_slopes(n_heads):
    h = jnp.arange(1, n_heads + 1, dtype=jnp.float32)
    return jnp.exp2(-8.0 * h / n_heads)


def _chunked_sgu(u, v, g_sgu, w_spatial, b_spatial):
    B, S, _ = u.shape
    nc = S // CHUNK
    v = _rmsnorm(v.reshape(B, S, SGU_GROUPS, SGU_CH), g_sgu.reshape(SGU_GROUPS, SGU_CH))
    vc = v.reshape(B, nc, CHUNK, SGU_GROUPS, SGU_CH)
    causal = jnp.tril(jnp.ones((CHUNK, CHUNK), dtype=bool))
    ws = jnp.where(causal[None], w_spatial, 0)
    sv = jnp.einsum('gij,bnjgc->bnigc', ws, vc) + jnp.transpose(b_spatial)[None, None, :, :, None]
    return (u.reshape(B, nc, CHUNK, SGU_GROUPS, SGU_CH) * sv).reshape(B, S, SGU_WIDTH)


def _dsa_attention(q, k, v, q_idx, k_idx, w_idx):
    B, S = q.shape[:2]
    topk = min(TOPK_MAX, S // 4)
    nb = S // Q_BLOCK
    pos = jnp.arange(S, dtype=jnp.int32)
    slopes = _alibi_slopes(ATTN_HEADS)
    kf = k_idx.astype(jnp.float32)
    scale = HEAD_DIM ** -0.5

    def to_blocks(a):
        return jnp.moveaxis(a.reshape(B, nb, Q_BLOCK, *a.shape[2:]), 1, 0)

    def block(args):
        qb, qib, wb, tb = args
        rel = jax.nn.relu(jnp.einsum('bqhd,bsd->bqhs', qib.astype(jnp.float32), kf))
        iscore = jnp.einsum('bqhs,bqh->bqs', rel, wb.astype(jnp.float32))
        admissible = pos[None, :] <= tb[:, None]
        iscore = jnp.where(admissible[None], iscore, NEG)
        _, sel = lax.top_k(iscore, topk)
        valid = sel <= tb[None, :, None]
        k_sel = jax.vmap(lambda kk, ss: kk[ss])(k, sel)
        v_sel = jax.vmap(lambda vv, ss: vv[ss])(v, sel)
        logits = jnp.einsum('bqhd,bqkhd->bhqk', qb.astype(jnp.float32), k_sel.astype(jnp.float32)) * scale
        dist = (tb[None, :, None] - sel).astype(jnp.float32)
        logits = logits - slopes[None, :, None, None] * dist[:, None]
        logits = jnp.where(valid[:, None], logits, NEG)
        p = jax.nn.softmax(logits, axis=-1)
        o = jnp.einsum('bhqk,bqkhd->bqhd', p, v_sel.astype(jnp.float32))
        return o.astype(q.dtype)

    out = lax.map(block, (to_blocks(q), to_blocks(q_idx), to_blocks(w_idx), pos.reshape(nb, Q_BLOCK)))
    return jnp.moveaxis(out, 0, 1).reshape(B, S, ATTN_WIDTH)


def _decoder_layer(x, c, w_ada, b_ada, g_norm1, w1_gate, w1_up, w1_down, g_norm2, w_in,
                   g_sgu, w_spatial, b_spatial, g_q, g_k, g_kidx, b_kidx,
                   w_branch_a, w_branch_b, w_out, g_norm3, w2_gate, w2_up, w2_down):
    B, S, _ = x.shape
    ada = jax.nn.silu(c) @ w_ada + b_ada
    sh1, sc1, gt1, sh2, sc2, gt2, sh3, sc3, gt3 = jnp.split(ada[:, None, :], N_ADA, axis=-1)

    n = _modulate(_rmsnorm(x, g_norm1), sh1, sc1)
    x = x + 0.5 * gt1 * _swiglu(n, w1_gate, w1_up, w1_down)

    n = _modulate(_rmsnorm(x, g_norm2), sh2, sc2)
    z = n @ w_in
    u, v, q, k, vv, qi, ki, wi, ga, gb = _split_cols(z, IN_SIZES)

    y_a = _chunked_sgu(jax.nn.gelu(u), jax.nn.gelu(v), g_sgu, w_spatial, b_spatial) @ w_branch_a

    q = _rmsnorm(q.reshape(B, S, ATTN_HEADS, HEAD_DIM), g_q)
    k = _rmsnorm(k.reshape(B, S, ATTN_HEADS, HEAD_DIM), g_k)
    vv = vv.reshape(B, S, ATTN_HEADS, HEAD_DIM)
    qi = qi.reshape(B, S, IDX_HEADS, IDX_DIM)
    ki = _layernorm(ki, g_kidx, b_kidx)
    wi = wi * (IDX_HEADS ** -0.5 * IDX_DIM ** -0.5)
    y_b = _dsa_attention(q, k, vv, qi, ki, wi) @ w_branch_b

    merged = jax.nn.sigmoid(ga) * y_a + jax.nn.sigmoid(gb) * y_b
    x = x + gt2 * (merged @ w_out)

    n = _modulate(_rmsnorm(x, g_norm3), sh3, sc3)
    x = x + 0.5 * gt3 * _swiglu(n, w2_gate, w2_up, w2_down)
    return x


def setup_inputs(seed: int = 0) -> dict:
    key = jax.random.key(seed)
    ks = jax.random.split(key, 24)
    f32 = jnp.float32
    L, D = DEPTH, D_MODEL

    def nrm(k, shape, s):
        return jax.random.normal(k, shape, f32) * s

    def gain(k, shape):
        return 1.0 + 0.02 * jax.random.normal(k, shape, f32)

    return {
        "x": nrm(ks[0], (BATCH, SEQ, D), 1.0),
        "c": nrm(ks[1], (BATCH, D), 1.0),
        "w_ada": nrm(ks[2], (L, D, N_ADA * D), 0.5 * D ** -0.5),
        "b_ada": nrm(ks[3], (L, N_ADA * D), 0.01),
        "g_norm1": gain(ks[4], (L, D)),
        "w1_gate": nrm(ks[5], (L, D, D_FF), D ** -0.5),
        "w1_up": nrm(ks[6], (L, D, D_FF), D ** -0.5),
        "w1_down": nrm(ks[7], (L, D_FF, D), D_FF ** -0.5),
        "g_norm2": gain(ks[8], (L, D)),
        "w_in": nrm(ks[9], (L, D, IN_WIDTH), D ** -0.5),
        "g_sgu": gain(ks[10], (L, SGU_WIDTH)),
        "w_spatial": nrm(ks[11], (L, SGU_GROUPS, CHUNK, CHUNK), CHUNK ** -0.5),
        "b_spatial": 1.0 + nrm(ks[12], (L, SGU_GROUPS, CHUNK), 0.1),
        "g_q": gain(ks[13], (L, HEAD_DIM)),
        "g_k": gain(ks[14], (L, HEAD_DIM)),
        "g_kidx": gain(ks[15], (L, IDX_DIM)),
        "b_kidx": nrm(ks[16], (L, IDX_DIM), 0.02),
        "w_branch_a": nrm(ks[17], (L, SGU_WIDTH, D), SGU_WIDTH ** -0.5),
        "w_branch_b": nrm(ks[18], (L, ATTN_WIDTH, D), ATTN_WIDTH ** -0.5),
        "w_out": nrm(ks[19], (L, D, D), D ** -0.5),
        "g_norm3": gain(ks[20], (L, D)),
        "w2_gate": nrm(ks[21], (L, D, D_FF), D ** -0.5),
        "w2_up": nrm(ks[22], (L, D, D_FF), D ** -0.5),
        "w2_down": nrm(ks[23], (L, D_FF, D), D_FF ** -0.5),
    }


def reference(x, c, w_ada, b_ada, g_norm1, w1_gate, w1_up, w1_down, g_norm2, w_in,
              g_sgu, w_spatial, b_spatial, g_q, g_k, g_kidx, b_kidx,
              w_branch_a, w_branch_b, w_out, g_norm3, w2_gate, w2_up, w2_down):
    for l in range(DEPTH):
        x = _decoder_layer(x, c, w_ada[l], b_ada[l], g_norm1[l], w1_gate[l], w1_up[l], w1_down[l],
                           g_norm2[l], w_in[l], g_sgu[l], w_spatial[l], b_spatial[l], g_q[l], g_k[l],
                           g_kidx[l], b_kidx[l], w_branch_a[l], w_branch_b[l], w_out[l], g_norm3[l],
                           w2_gate[l], w2_up[l], w2_down[l])
    return x
```

```python
import functools

import jax
import jax.numpy as jnp
from jax import lax
from jax.experimental import pallas as pl
from jax.experimental.pallas import tpu as pltpu

F32 = jnp.float32
BF16 = jnp.bfloat16
I32 = jnp.int32

CHUNK = 128
SGU_GROUPS = 8
SGU_WIDTH = 1024
ATTN_HEADS = 8
HEAD_DIM = 128
ATTN_WIDTH = ATTN_HEADS * HEAD_DIM
IDX_HEADS = 16
IDX_DIM = 64
TOPK_MAX = 256
N_ADA = 9
EPS = 1e-6
NEG = -1e30
INT_MIN = -(2 ** 31)

LANES = 128
VMEM_LIMIT = 56 * 1024 * 1024

FFN_TM = 512
FFN_TF = 512
PROJ_TM = 1024
PROJ_TN = 512
SGU_TM = 512
MERGE_TM = 256
ATT_TQ = 256
ATT_TK = 512
IDX_TK = 256
CNT_RG = 64


def _cparams(sem):
    return pltpu.CompilerParams(dimension_semantics=sem, vmem_limit_bytes=VMEM_LIMIT)


def _gelu_tanh(x):
    return 0.5 * x * (1.0 + jnp.tanh(0.7978845608028654 * (x + 0.044715 * (x * x * x))))


def _rms_mod(x, g, sh, sc):
    ms = jnp.mean(x * x, axis=-1, keepdims=True)
    y = (x * lax.rsqrt(ms + EPS)) * g
    return y * (1.0 + sc) + sh


def _ada_kernel(c_ref, w_ref, b_ref, o_ref, sb_ref):
    d = w_ref.shape[0]
    tn = w_ref.shape[1]

    @pl.when(pl.program_id(0) == 0)
    def _():
        cc = c_ref[...]
        sb_ref[...] = jnp.broadcast_to(cc * jax.nn.sigmoid(cc), (d, LANES))

    sb = sb_ref[...]
    for cb in range(tn // LANES):
        cols = slice(cb * LANES, (cb + 1) * LANES)
        prod = (w_ref[:, cols] * sb).reshape(d // 8, 8, LANES).sum(axis=0)
        o_ref[:, cols] = prod.sum(axis=0, keepdims=True) + b_ref[:, cols]


def _ada(c_col, w, b):
    d, n = w.shape
    tn = 1024
    return pl.pallas_call(
        _ada_kernel,
        grid=(n // tn,),
        in_specs=[pl.BlockSpec((d, 1), lambda j: (0, 0)),
                  pl.BlockSpec((d, tn), lambda j: (0, j)),
                  pl.BlockSpec((1, tn), lambda j: (0, j))],
        out_specs=pl.BlockSpec((1, tn), lambda j: (0, j)),
        out_shape=jax.ShapeDtypeStruct((1, n), F32),
        scratch_shapes=[pltpu.VMEM((d, LANES), F32)],
        compiler_params=_cparams(("arbitrary",)),
        name="ada",
    )(c_col, w, b)


def _ffn_kernel(x_ref, g_ref, sh_ref, sc_ref, gt_ref, wg_ref, wu_ref, wd_ref, o_ref, n_ref):
    j = pl.program_id(1)

    @pl.when(j == 0)
    def _():
        n_ref[...] = _rms_mod(x_ref[...], g_ref[...], sh_ref[...], sc_ref[...]).astype(BF16)

    n = n_ref[...]
    g = jnp.dot(n, wg_ref[...], preferred_element_type=F32)
    u = jnp.dot(n, wu_ref[...], preferred_element_type=F32)
    h = ((g * jax.nn.sigmoid(g)) * u).astype(BF16)
    d = jnp.dot(h, wd_ref[...], preferred_element_type=F32)

    @pl.when(j == 0)
    def _():
        o_ref[...] = d

    @pl.when(j > 0)
    def _():
        o_ref[...] += d

    @pl.when(j == pl.num_programs(1) - 1)
    def _():
        o_ref[...] = x_ref[...] + (0.5 * gt_ref[...]) * o_ref[...]


def _ffn(x, g, sh, sc, gt, wg, wu, wd):
    s, d = x.shape
    f = wg.shape[1]
    row = pl.BlockSpec((1, d), lambda i, j: (0, 0))
    return pl.pallas_call(
        _ffn_kernel,
        grid=(s // FFN_TM, f // FFN_TF),
        in_specs=[pl.BlockSpec((FFN_TM, d), lambda i, j: (i, 0)), row, row, row, row,
                  pl.BlockSpec((d, FFN_TF), lambda i, j: (0, j)),
                  pl.BlockSpec((d, FFN_TF), lambda i, j: (0, j)),
                  pl.BlockSpec((FFN_TF, d), lambda i, j: (j, 0))],
        out_specs=pl.BlockSpec((FFN_TM, d), lambda i, j: (i, 0)),
        out_shape=jax.ShapeDtypeStruct((s, d), F32),
        scratch_shapes=[pltpu.VMEM((FFN_TM, d), BF16)],
        compiler_params=_cparams(("parallel", "arbitrary")),
        name="ffn",
    )(x, g, sh, sc, gt, wg, wu, wd)


def _normmod_kernel(x_ref, g_ref, sh_ref, sc_ref, o_ref):
    o_ref[...] = _rms_mod(x_ref[...], g_ref[...], sh_ref[...], sc_ref[...]).astype(o_ref.dtype)


def _normmod(x, g, sh, sc):
    s, d = x.shape
    tm = 512
    row = pl.BlockSpec((1, d), lambda i: (0, 0))
    return pl.pallas_call(
        _normmod_kernel,
        grid=(s // tm,),
        in_specs=[pl.BlockSpec((tm, d), lambda i: (i, 0)), row, row, row],
        out_specs=pl.BlockSpec((tm, d), lambda i: (i, 0)),
        out_shape=jax.ShapeDtypeStruct((s, d), BF16),
        compiler_params=_cparams(("parallel",)),
        name="normmod",
    )(x, g, sh, sc)


def _group_rms(z, gain, post_scale):
    outs = []
    for gidx in range(z.shape[1] // LANES):
        cols = slice(gidx * LANES, (gidx + 1) * LANES)
        zg = z[:, cols]
        ms = jnp.mean(zg * zg, axis=-1, keepdims=True)
        y = (zg * lax.rsqrt(ms + EPS)) * gain[:, cols]
        if post_scale != 1.0:
            y = y * post_scale
        outs.append(y)
    return jnp.concatenate(outs, axis=-1)


def _proj_kernel(n_ref, w_ref, *rest, mode, post_scale):
    o_ref = rest[-1]
    z = jnp.dot(n_ref[...], w_ref[...], preferred_element_type=F32)
    if mode == "gelu":
        out = _gelu_tanh(z)
    elif mode == "gelu_gnorm":
        out = _group_rms(_gelu_tanh(z), rest[0][...], post_scale)
    elif mode == "gnorm":
        out = _group_rms(z, rest[0][...], post_scale)
    elif mode == "sigmoid":
        out = jax.nn.sigmoid(z)
    else:
        out = z
    o_ref[...] = out.astype(o_ref.dtype)


def _proj(n, w, mode, gain=None, post_scale=1.0, out_dtype=BF16):
    s, d = n.shape
    nout = w.shape[1]
    in_specs = [pl.BlockSpec((PROJ_TM, d), lambda i, j: (i, 0)),
                pl.BlockSpec((d, PROJ_TN), lambda i, j: (0, j))]
    args = [n, w]
    if gain is not None:
        in_specs.append(pl.BlockSpec((1, PROJ_TN), lambda i, j: (0, j)))
        args.append(gain)
    return pl.pallas_call(
        functools.partial(_proj_kernel, mode=mode, post_scale=post_scale),
        grid=(s // PROJ_TM, nout // PROJ_TN),
        in_specs=in_specs,
        out_specs=pl.BlockSpec((PROJ_TM, PROJ_TN), lambda i, j: (i, j)),
        out_shape=jax.ShapeDtypeStruct((s, nout), out_dtype),
        compiler_params=_cparams(("parallel", "arbitrary")),
        name="proj_" + mode,
    )(*args)


def _qidx_kernel(n_ref, w_ref, o_ref):
    z = jnp.dot(n_ref[...], w_ref[...], preferred_element_type=F32)
    for h in range(IDX_HEADS):
        o_ref[h] = z[:, h * IDX_DIM:(h + 1) * IDX_DIM].astype(o_ref.dtype)


def _qidx(n, w):
    s, d = n.shape
    tm = 512
    return pl.pallas_call(
        _qidx_kernel,
        grid=(s // tm,),
        in_specs=[pl.BlockSpec((tm, d), lambda i: (i, 0)),
                  pl.BlockSpec((d, IDX_HEADS * IDX_DIM), lambda i: (0, 0))],
        out_specs=pl.BlockSpec((IDX_HEADS, tm, IDX_DIM), lambda i: (0, i, 0)),
        out_shape=jax.ShapeDtypeStruct((IDX_HEADS, s, IDX_DIM), BF16),
        compiler_params=_cparams(("parallel",)),
        name="proj_qidx",
    )(n, w)


def _kidx_kernel(n_ref, w_ref, g_ref, b_ref, ki_ref, wi_ref):
    z = jnp.dot(n_ref[...], w_ref[...], preferred_element_type=F32)
    ki = z[:, :IDX_DIM]
    mu = jnp.mean(ki, axis=-1, keepdims=True)
    var = jnp.mean(jnp.square(ki - mu), axis=-1, keepdims=True)
    y = (ki - mu) * lax.rsqrt(var + EPS)
    ki_ref[...] = (y * g_ref[...] + b_ref[...]).astype(ki_ref.dtype)
    wi_ref[...] = z[:, IDX_DIM:IDX_DIM + IDX_HEADS] * (IDX_HEADS ** -0.5 * IDX_DIM ** -0.5)


def _kidx(n, w, g, b):
    s, d = n.shape
    tm = 512
    return pl.pallas_call(
        _kidx_kernel,
        grid=(s // tm,),
        in_specs=[pl.BlockSpec((tm, d), lambda i: (i, 0)),
                  pl.BlockSpec((d, LANES), lambda i: (0, 0)),
                  pl.BlockSpec((1, IDX_DIM), lambda i: (0, 0)),
                  pl.BlockSpec((1, IDX_DIM), lambda i: (0, 0))],
        out_specs=[pl.BlockSpec((tm, IDX_DIM), lambda i: (i, 0)),
                   pl.BlockSpec((tm, IDX_HEADS), lambda i: (i, 0))],
        out_shape=[jax.ShapeDtypeStruct((s, IDX_DIM), BF16),
                   jax.ShapeDtypeStruct((s, IDX_HEADS), F32)],
        compiler_params=_cparams(("parallel",)),
        name="proj_kidx",
    )(n, w, g, b)


def _sgu_kernel(u_ref, v_ref, ws_ref, bt_ref, o_ref):
    tm = u_ref.shape[0]
    r = lax.broadcasted_iota(I32, (CHUNK, CHUNK), 0)
    c = lax.broadcasted_iota(I32, (CHUNK, CHUNK), 1)
    causal = c <= r
    for g in range(SGU_GROUPS):
        cols = slice(g * LANES, (g + 1) * LANES)
        w = jnp.where(causal, ws_ref[g], 0.0).astype(BF16)
        bcol = bt_ref[:, g:g + 1]
        for ch in range(tm // CHUNK):
            rows = slice(ch * CHUNK, (ch + 1) * CHUNK)
            sv = jnp.dot(w, v_ref[rows, cols], preferred_element_type=F32) + bcol
            o_ref[rows, cols] = (u_ref[rows, cols].astype(F32) * sv).astype(o_ref.dtype)


def _sgu(u, v, ws, bt):
    s, wdt = u.shape
    return pl.pallas_call(
        _sgu_kernel,
        grid=(s // SGU_TM,),
        in_specs=[pl.BlockSpec((SGU_TM, wdt), lambda i: (i, 0)),
                  pl.BlockSpec((SGU_TM, wdt), lambda i: (i, 0)),
                  pl.BlockSpec((SGU_GROUPS, CHUNK, CHUNK), lambda i: (0, 0, 0)),
                  pl.BlockSpec((CHUNK, SGU_GROUPS), lambda i: (0, 0))],
        out_specs=pl.BlockSpec((SGU_TM, wdt), lambda i: (i, 0)),
        out_shape=jax.ShapeDtypeStruct((s, wdt), BF16),
        compiler_params=_cparams(("parallel",)),
        name="sgu",
    )(u, v, ws, bt)


def _tile_lanes(x, n):
    return x if n == 1 else jnp.concatenate([x] * n, axis=1)


def _attn_kernel(qhm_ref, wi_ref, kit_ref, q_ref, kt_ref, v_ref, o_ref,
                 keys_ref, r_ref, wib_ref, thr_ref, m_ref, l_ref, acc_ref, *, slopes):
    i = pl.program_id(0)
    j = pl.program_id(1)
    tq, tk = ATT_TQ, ATT_TK
    q0 = i * tq
    n_kv = (q0 + tq + tk - 1) // tk

    @pl.when(j == 0)
    def _index_and_threshold():
        m_ref[...] = jnp.full(m_ref.shape, NEG, F32)
        l_ref[...] = jnp.zeros(l_ref.shape, F32)
        acc_ref[...] = jnp.zeros(acc_ref.shape, F32)

        wi = wi_ref[...]
        for h in range(IDX_HEADS):
            wib_ref[h] = jnp.broadcast_to(wi[:, h:h + 1], (tq, LANES))
        qh = qhm_ref[...].reshape(IDX_HEADS * tq, IDX_DIM)

        def chunk_body(c, carry):
            off = pl.multiple_of(c * IDX_TK, IDX_TK)
            r_ref[...] = jnp.dot(qh, kit_ref[:, pl.ds(off, IDX_TK)], preferred_element_type=F32)
            for rh in range(tq // LANES):
                rows = slice(rh * LANES, (rh + 1) * LANES)
                t_idx = q0 + rh * LANES + lax.broadcasted_iota(I32, (LANES, LANES), 0)
                for lb in range(IDX_TK // LANES):
                    cols = slice(lb * LANES, (lb + 1) * LANES)
                    acc = jnp.zeros((LANES, LANES), F32)
                    for h in range(IDX_HEADS):
                        rr = r_ref[h * tq + rh * LANES:h * tq + (rh + 1) * LANES, cols]
                        acc = acc + jnp.maximum(rr, 0.0) * wib_ref[h, rows, :]
                    bits = lax.bitcast_convert_type(acc, I32)
                    key = bits ^ ((bits >> 31) & 0x7FFFFFFF)
                    s_idx = off + lb * LANES + lax.broadcasted_iota(I32, (LANES, LANES), 1)
                    key = jnp.where(s_idx <= t_idx, key, INT_MIN)
                    keys_ref[rows, pl.ds(pl.multiple_of(off + lb * LANES, LANES), LANES)] = key
            return carry

        lax.fori_loop(0, (q0 + tq) // IDX_TK, chunk_body, 0)

        def fill_body(f, carry):
            off = pl.multiple_of(q0 + tq + f * tq, tq)
            keys_ref[:, pl.ds(off, tq)] = jnp.full((tq, tq), INT_MIN, I32)
            return carry

        lax.fori_loop(0, (n_kv * tk - (q0 + tq)) // tq, fill_body, 0)

        n_cb = (q0 + tq) // tq

        def bit_body(b, thr):
            cand_all = jnp.where(b == 0, 0, thr + lax.shift_left(jnp.int32(1), 31 - b))
            new = []
            for rg in range(tq // CNT_RG):
                rows = slice(rg * CNT_RG, (rg + 1) * CNT_RG)
                cand = cand_all[rows]

                def col_body(cb, cnt):
                    off = pl.multiple_of(cb * tq, tq)
                    for lb in range(tq // LANES):
                        blk = keys_ref[rows, pl.ds(pl.multiple_of(off + lb * LANES, LANES), LANES)]
                        cnt = cnt + jnp.where(blk >= cand, 1, 0)
                    return cnt

                cnt = lax.fori_loop(0, n_cb, col_body, jnp.zeros((CNT_RG, LANES), I32))
                tot = jnp.sum(cnt.astype(F32), axis=-1, keepdims=True)
                t_row = q0 + rg * CNT_RG + lax.broadcasted_iota(I32, (CNT_RG, 1), 0)
                kq = jnp.minimum(TOPK_MAX, t_row + 1).astype(F32)
                new.append(jnp.where(tot >= kq, cand, thr[rows]))
            return jnp.concatenate(new, axis=0)

        thr_ref[...] = lax.fori_loop(0, 32, bit_body, jnp.full((tq, LANES), INT_MIN, I32))

    @pl.when(j < n_kv)
    def _attend():
        k0 = pl.multiple_of(j * tk, tk)
        nl = tk // LANES
        sel = keys_ref[:, pl.ds(k0, tk)] >= _tile_lanes(thr_ref[...], nl)
        bias = jnp.where(sel, 0.0, NEG)
        dist = ((k0 + lax.broadcasted_iota(I32, (tq, tk), 1))
                - (q0 + lax.broadcasted_iota(I32, (tq, tk), 0))).astype(F32)
        for h in range(ATTN_HEADS):
            cols = slice(h * HEAD_DIM, (h + 1) * HEAD_DIM)
            s = jnp.dot(q_ref[:, cols], kt_ref[cols, :], preferred_element_type=F32)
            s = s + (dist * slopes[h] + bias)
            m_prev = m_ref[h]
            m_new = jnp.maximum(m_prev, jnp.max(s, axis=-1, keepdims=True))
            alpha = jnp.exp(m_prev - m_new)
            p = jnp.exp(s - _tile_lanes(m_new, nl))
            l_ref[h] = alpha * l_ref[h] + jnp.sum(p, axis=-1, keepdims=True)
            acc_ref[:, cols] = alpha * acc_ref[:, cols] + jnp.dot(
                p.astype(BF16), v_ref[:, cols], preferred_element_type=F32)
            m_ref[h] = m_new

    @pl.when(j == pl.num_programs(1) - 1)
    def _finish():
        for h in range(ATTN_HEADS):
            cols = slice(h * HEAD_DIM, (h + 1) * HEAD_DIM)
            o_ref[:, cols] = (acc_ref[:, cols] / l_ref[h]).astype(o_ref.dtype)


def _attention(qhm, wi, kit, q, kt, v):
    s = q.shape[0]
    tq, tk = ATT_TQ, ATT_TK
    slopes = tuple(float(2.0 ** (-8.0 * (h + 1) / ATTN_HEADS)) for h in range(ATTN_HEADS))

    def last_kv(i):
        return (i * tq + tq + tk - 1) // tk - 1

    return pl.pallas_call(
        functools.partial(_attn_kernel, slopes=slopes),
        grid=(s // tq, s // tk),
        in_specs=[pl.BlockSpec((IDX_HEADS, tq, IDX_DIM), lambda i, j: (0, i, 0)),
                  pl.BlockSpec((tq, IDX_HEADS), lambda i, j: (i, 0)),
                  pl.BlockSpec((IDX_DIM, s), lambda i, j: (0, 0)),
                  pl.BlockSpec((tq, ATTN_WIDTH), lambda i, j: (i, 0)),
                  pl.BlockSpec((ATTN_WIDTH, tk), lambda i, j: (0, jnp.minimum(j, last_kv(i)))),
                  pl.BlockSpec((tk, ATTN_WIDTH), lambda i, j: (jnp.minimum(j, last_kv(i)), 0))],
        out_specs=pl.BlockSpec((tq, ATTN_WIDTH), lambda i, j: (i, 0)),
        out_shape=jax.ShapeDtypeStruct((s, ATTN_WIDTH), BF16),
        scratch_shapes=[pltpu.VMEM((tq, s), I32),
                        pltpu.VMEM((IDX_HEADS * tq, IDX_TK), F32),
                        pltpu.VMEM((IDX_HEADS, tq, LANES), F32),
                        pltpu.VMEM((tq, LANES), I32),
                        pltpu.VMEM((ATTN_HEADS, tq, LANES), F32),
                        pltpu.VMEM((ATTN_HEADS, tq, LANES), F32),
                        pltpu.VMEM((tq, ATTN_WIDTH), F32)],
        compiler_params=_cparams(("parallel", "arbitrary")),
        name="dsa_attention",
    )(qhm, wi, kit, q, kt, v)


def _merge_kernel(ya_ref, yb_ref, sg_a_ref, sg_b_ref, x_ref, gt_ref, wa_ref, wb_ref, wo_ref, o_ref):
    a = jnp.dot(ya_ref[...], wa_ref[...], preferred_element_type=F32)
    b = jnp.dot(yb_ref[...], wb_ref[...], preferred_element_type=F32)
    merged = sg_a_ref[...].astype(F32) * a + sg_b_ref[...].astype(F32) * b
    o_ref[...] = x_ref[...] + gt_ref[...] * jnp.dot(
        merged.astype(BF16), wo_ref[...], preferred_element_type=F32)


def _merge(ya, yb, sg, x, gt, wa, wb, wo):
    s, d = x.shape
    tm = MERGE_TM
    wdt = ya.shape[1]
    const = lambda i: (0, 0)
    return pl.pallas_call(
        _merge_kernel,
        grid=(s // tm,),
        in_specs=[pl.BlockSpec((tm, wdt), lambda i: (i, 0)),
                  pl.BlockSpec((tm, wdt), lambda i: (i, 0)),
                  pl.BlockSpec((tm, d), lambda i: (i, 0)),
                  pl.BlockSpec((tm, d), lambda i: (i, 1)),
                  pl.BlockSpec((tm, d), lambda i: (i, 0)),
                  pl.BlockSpec((1, d), const),
                  pl.BlockSpec((wdt, d), const),
                  pl.BlockSpec((wdt, d), const),
                  pl.BlockSpec((d, d), const)],
        out_specs=pl.BlockSpec((tm, d), lambda i: (i, 0)),
        out_shape=jax.ShapeDtypeStruct((s, d), F32),
        compiler_params=_cparams(("parallel",)),
        name="merge",
    )(ya, yb, sg, sg, x, gt, wa, wb, wo)


def _pad_cols(w, n):
    return jnp.pad(w, ((0, 0), (0, n - w.shape[1])))


def _layer(x, c, w_ada, b_ada, g_norm1, w1_gate, w1_up, w1_down, g_norm2, w_in, g_sgu, w_spatial,
           b_spatial, g_q, g_k, g_kidx, b_kidx, w_branch_a, w_branch_b, w_out, g_norm3,
           w2_gate, w2_up, w2_down):
    s, d = x.shape
    dff = w1_gate.shape[1]
    dff_p = -(-dff // FFN_TF) * FFN_TF

    ada = _ada(c.reshape(d, 1), w_ada, b_ada.reshape(1, -1))
    sh1, sc1, gt1, sh2, sc2, gt2, sh3, sc3, gt3 = [ada[:, k * d:(k + 1) * d] for k in range(N_ADA)]

    def ffn_weights(wg, wu, wd):
        return (_pad_cols(wg.astype(BF16), dff_p), _pad_cols(wu.astype(BF16), dff_p),
                jnp.pad(wd.astype(BF16), ((0, dff_p - dff), (0, 0))))

    x1 = _ffn(x, g_norm1.reshape(1, d), sh1, sc1, gt1, *ffn_weights(w1_gate, w1_up, w1_down))

    n2 = _normmod(x1, g_norm2.reshape(1, d), sh2, sc2)
    wi_b = w_in.astype(BF16)
    o_u, o_v, o_q, o_k, o_vv = 0, SGU_WIDTH, 2 * SGU_WIDTH, 2 * SGU_WIDTH + ATTN_WIDTH, 2 * SGU_WIDTH + 2 * ATTN_WIDTH
    o_qi = o_vv + ATTN_WIDTH
    o_ki = o_qi + IDX_HEADS * IDX_DIM
    o_g = o_ki + IDX_DIM + IDX_HEADS

    ug = _proj(n2, wi_b[:, o_u:o_v], "gelu")
    vn = _proj(n2, wi_b[:, o_v:o_q], "gelu_gnorm", gain=g_sgu.reshape(1, SGU_WIDTH))
    qk_gain = jnp.concatenate([jnp.tile(g_q, ATTN_HEADS), jnp.tile(g_k, ATTN_HEADS)]).reshape(1, -1)
    qn = _proj(n2, wi_b[:, o_q:o_k], "gnorm", gain=qk_gain[:, :ATTN_WIDTH], post_scale=HEAD_DIM ** -0.5)
    kn = _proj(n2, wi_b[:, o_k:o_vv], "gnorm", gain=qk_gain[:, ATTN_WIDTH:])
    vv = _proj(n2, wi_b[:, o_vv:o_qi], "none")
    sg = _proj(n2, wi_b[:, o_g:], "sigmoid")
    qhm = _qidx(n2, wi_b[:, o_qi:o_ki])
    ki, wi = _kidx(n2, _pad_cols(wi_b[:, o_ki:o_g], LANES),
                   g_kidx.reshape(1, IDX_DIM), b_kidx.reshape(1, IDX_DIM))

    y_a = _sgu(ug, vn, w_spatial, jnp.transpose(b_spatial))
    y_b = _attention(qhm, wi, jnp.transpose(ki), qn, jnp.transpose(kn), vv)

    x2 = _merge(y_a, y_b, sg, x1, gt2, w_branch_a.astype(BF16), w_branch_b.astype(BF16),
                w_out.astype(BF16))

    return _ffn(x2, g_norm3.reshape(1, d), sh3, sc3, gt3, *ffn_weights(w2_gate, w2_up, w2_down))


def kernel(x, c, w_ada, b_ada, g_norm1, w1_gate, w1_up, w1_down, g_norm2, w_in, g_sgu, w_spatial,
           b_spatial, g_q, g_k, g_kidx, b_kidx, w_branch_a, w_branch_b, w_out, g_norm3,
           w2_gate, w2_up, w2_down):
    batch, depth = x.shape[0], w_ada.shape[0]
    outs = []
    for b in range(batch):
        xb = x[b]
        for l in range(depth):
            xb = _layer(xb, c[b], w_ada[l], b_ada[l], g_norm1[l], w1_gate[l], w1_up[l], w1_down[l],
                        g_norm2[l], w_in[l], g_sgu[l], w_spatial[l], b_spatial[l], g_q[l], g_k[l],
                        g_kidx[l], b_kidx[l], w_branch_a[l], w_branch_b[l], w_out[l], g_norm3[l],
                        w2_gate[l], w2_up[l], w2_down[l])
        outs.append(xb)
    return jnp.stack(outs)
```

```python
import functools

import numpy as np
import jax
import jax.numpy as jnp
from jax import lax
from jax.experimental import pallas as pl
from jax.experimental.pallas import tpu as pltpu

F32 = jnp.float32
BF16 = jnp.bfloat16
I32 = jnp.int32

CHUNK = 128
SGU_GROUPS = 8
SGU_WIDTH = 1024
ATTN_HEADS = 8
HEAD_DIM = 128
ATTN_WIDTH = ATTN_HEADS * HEAD_DIM
IDX_HEADS = 16
IDX_DIM = 64
TOPK_MAX = 256
N_ADA = 9
EPS = 1e-6
NEG = -1e30
INT_MIN = -(2 ** 31)

LANES = 128
SUBLANES = 8
VMEM_LIMIT = 56 * 1024 * 1024

FFN_TM = 512
FFN_TF = 512
PROJ_TM = 1024
PROJ_TN = 512
SGU_TM = 512
MERGE_TM = 256
ATT_TQ = 256
ATT_TK = 512
IDX_TK = 256
CNT_BLK = 1024
CNT_ACC = 8
AUG = 128
LOG2E = 1.4426950408889634


def _cparams(sem):
    return pltpu.CompilerParams(dimension_semantics=sem, vmem_limit_bytes=VMEM_LIMIT)


def _gelu_tanh(x):
    return 0.5 * x * (1.0 + jnp.tanh(0.7978845608028654 * (x + 0.044715 * (x * x * x))))


def _rms_mod(x, g, sh, sc):
    ms = jnp.mean(x * x, axis=-1, keepdims=True)
    y = (x * lax.rsqrt(ms + EPS)) * g
    return y * (1.0 + sc) + sh


def _ada_kernel(c_ref, w_ref, b_ref, o_ref, sb_ref):
    d = w_ref.shape[0]
    tn = w_ref.shape[1]

    @pl.when(pl.program_id(0) == 0)
    def _():
        cc = c_ref[...]
        sb_ref[...] = jnp.broadcast_to(cc * jax.nn.sigmoid(cc), (d, LANES))

    sb = sb_ref[...]
    for cb in range(tn // LANES):
        cols = slice(cb * LANES, (cb + 1) * LANES)
        prod = (w_ref[:, cols] * sb).reshape(d // SUBLANES, SUBLANES, LANES).sum(axis=0)
        o_ref[:, cols] = prod.sum(axis=0, keepdims=True) + b_ref[:, cols]


def _ada(c_col, w, b):
    d, n = w.shape
    tn = 1024
    return pl.pallas_call(
        _ada_kernel,
        grid=(n // tn,),
        in_specs=[pl.BlockSpec((d, 1), lambda j: (0, 0)),
                  pl.BlockSpec((d, tn), lambda j: (0, j)),
                  pl.BlockSpec((1, tn), lambda j: (0, j))],
        out_specs=pl.BlockSpec((1, tn), lambda j: (0, j)),
        out_shape=jax.ShapeDtypeStruct((1, n), F32),
        scratch_shapes=[pltpu.VMEM((d, LANES), F32)],
        compiler_params=_cparams(("arbitrary",)),
        name="ada",
    )(c_col, w, b)


def _ffn_kernel(x_ref, g_ref, sh_ref, sc_ref, gt_ref, wg_ref, wu_ref, wd_ref, o_ref, n_ref):
    j = pl.program_id(1)

    @pl.when(j == 0)
    def _():
        n_ref[...] = _rms_mod(x_ref[...], g_ref[...], sh_ref[...], sc_ref[...]).astype(BF16)

    n = n_ref[...]
    g = jnp.dot(n, wg_ref[...], preferred_element_type=F32)
    u = jnp.dot(n, wu_ref[...], preferred_element_type=F32)
    h = ((g * jax.nn.sigmoid(g)) * u).astype(BF16)
    d = jnp.dot(h, wd_ref[...], preferred_element_type=F32)

    @pl.when(j == 0)
    def _():
        o_ref[...] = d

    @pl.when(j > 0)
    def _():
        o_ref[...] += d

    @pl.when(j == pl.num_programs(1) - 1)
    def _():
        o_ref[...] = x_ref[...] + (0.5 * gt_ref[...]) * o_ref[...]


def _ffn(x, g, sh, sc, gt, wg, wu, wd):
    s, d = x.shape
    f = wg.shape[1]
    tm = min(FFN_TM, s)
    row = pl.BlockSpec((1, d), lambda i, j: (0, 0))
    return pl.pallas_call(
        _ffn_kernel,
        grid=(s // tm, f // FFN_TF),
        in_specs=[pl.BlockSpec((tm, d), lambda i, j: (i, 0)), row, row, row, row,
                  pl.BlockSpec((d, FFN_TF), lambda i, j: (0, j)),
                  pl.BlockSpec((d, FFN_TF), lambda i, j: (0, j)),
                  pl.BlockSpec((FFN_TF, d), lambda i, j: (j, 0))],
        out_specs=pl.BlockSpec((tm, d), lambda i, j: (i, 0)),
        out_shape=jax.ShapeDtypeStruct((s, d), F32),
        scratch_shapes=[pltpu.VMEM((tm, d), BF16)],
        compiler_params=_cparams(("parallel", "arbitrary")),
        name="ffn",
    )(x, g, sh, sc, gt, wg, wu, wd)


def _normmod_kernel(x_ref, g_ref, sh_ref, sc_ref, o_ref):
    o_ref[...] = _rms_mod(x_ref[...], g_ref[...], sh_ref[...], sc_ref[...]).astype(o_ref.dtype)


def _normmod(x, g, sh, sc):
    s, d = x.shape
    tm = 512
    row = pl.BlockSpec((1, d), lambda i: (0, 0))
    return pl.pallas_call(
        _normmod_kernel,
        grid=(s // tm,),
        in_specs=[pl.BlockSpec((tm, d), lambda i: (i, 0)), row, row, row],
        out_specs=pl.BlockSpec((tm, d), lambda i: (i, 0)),
        out_shape=jax.ShapeDtypeStruct((s, d), BF16),
        compiler_params=_cparams(("parallel",)),
        name="normmod",
    )(x, g, sh, sc)


def _group_rms(z, gain, post_scale):
    outs = []
    for gidx in range(z.shape[1] // LANES):
        cols = slice(gidx * LANES, (gidx + 1) * LANES)
        zg = z[:, cols]
        ms = jnp.mean(zg * zg, axis=-1, keepdims=True)
        y = (zg * lax.rsqrt(ms + EPS)) * gain[:, cols]
        if post_scale != 1.0:
            y = y * post_scale
        outs.append(y)
    return jnp.concatenate(outs, axis=-1)


def _proj_kernel(n_ref, w_ref, *rest, mode, post_scale):
    o_ref = rest[-1]
    z = jnp.dot(n_ref[...], w_ref[...], preferred_element_type=F32)
    if mode == "gelu":
        out = _gelu_tanh(z)
    elif mode == "gelu_gnorm":
        out = _group_rms(_gelu_tanh(z), rest[0][...], post_scale)
    elif mode == "gnorm":
        out = _group_rms(z, rest[0][...], post_scale)
    elif mode == "sigmoid":
        out = jax.nn.sigmoid(z)
    else:
        out = z
    o_ref[...] = out.astype(o_ref.dtype)


def _proj(n, w, mode, gain=None, post_scale=1.0, out_dtype=BF16):
    s, d = n.shape
    nout = w.shape[1]
    in_specs = [pl.BlockSpec((PROJ_TM, d), lambda i, j: (i, 0)),
                pl.BlockSpec((d, PROJ_TN), lambda i, j: (0, j))]
    args = [n, w]
    if gain is not None:
        in_specs.append(pl.BlockSpec((1, PROJ_TN), lambda i, j: (0, j)))
        args.append(gain)
    return pl.pallas_call(
        functools.partial(_proj_kernel, mode=mode, post_scale=post_scale),
        grid=(s // PROJ_TM, nout // PROJ_TN),
        in_specs=in_specs,
        out_specs=pl.BlockSpec((PROJ_TM, PROJ_TN), lambda i, j: (i, j)),
        out_shape=jax.ShapeDtypeStruct((s, nout), out_dtype),
        compiler_params=_cparams(("parallel", "arbitrary")),
        name="proj_" + mode,
    )(*args)


def _kidx_kernel(n_ref, w_ref, g_ref, b_ref, ki_ref, wi_ref):
    z = jnp.dot(n_ref[...], w_ref[...], preferred_element_type=F32)
    ki = z[:, :IDX_DIM]
    mu = jnp.mean(ki, axis=-1, keepdims=True)
    var = jnp.mean(jnp.square(ki - mu), axis=-1, keepdims=True)
    y = (ki - mu) * lax.rsqrt(var + EPS)
    ki_ref[...] = (y * g_ref[...] + b_ref[...]).astype(ki_ref.dtype)
    wi_ref[...] = z[:, IDX_DIM:IDX_DIM + IDX_HEADS] * (IDX_HEADS ** -0.5 * IDX_DIM ** -0.5)


def _kidx(n, w, g, b):
    s, d = n.shape
    tm = 512
    return pl.pallas_call(
        _kidx_kernel,
        grid=(s // tm,),
        in_specs=[pl.BlockSpec((tm, d), lambda i: (i, 0)),
                  pl.BlockSpec((d, LANES), lambda i: (0, 0)),
                  pl.BlockSpec((1, IDX_DIM), lambda i: (0, 0)),
                  pl.BlockSpec((1, IDX_DIM), lambda i: (0, 0))],
        out_specs=[pl.BlockSpec((tm, IDX_DIM), lambda i: (i, 0)),
                   pl.BlockSpec((tm, IDX_HEADS), lambda i: (i, 0))],
        out_shape=[jax.ShapeDtypeStruct((s, IDX_DIM), BF16),
                   jax.ShapeDtypeStruct((s, IDX_HEADS), F32)],
        compiler_params=_cparams(("parallel",)),
        name="proj_kidx",
    )(n, w, g, b)


def _sgu_kernel(u_ref, v_ref, ws_ref, bt_ref, o_ref):
    tm = u_ref.shape[0]
    r = lax.broadcasted_iota(I32, (CHUNK, CHUNK), 0)
    c = lax.broadcasted_iota(I32, (CHUNK, CHUNK), 1)
    causal = c <= r
    for g in range(SGU_GROUPS):
        cols = slice(g * LANES, (g + 1) * LANES)
        w = jnp.where(causal, ws_ref[g], 0.0).astype(BF16)
        bcol = bt_ref[:, g:g + 1]
        for ch in range(tm // CHUNK):
            rows = slice(ch * CHUNK, (ch + 1) * CHUNK)
            sv = jnp.dot(w, v_ref[rows, cols], preferred_element_type=F32) + bcol
            o_ref[rows, cols] = (u_ref[rows, cols].astype(F32) * sv).astype(o_ref.dtype)


def _sgu(u, v, ws, bt):
    s, wdt = u.shape
    return pl.pallas_call(
        _sgu_kernel,
        grid=(s // SGU_TM,),
        in_specs=[pl.BlockSpec((SGU_TM, wdt), lambda i: (i, 0)),
                  pl.BlockSpec((SGU_TM, wdt), lambda i: (i, 0)),
                  pl.BlockSpec((SGU_GROUPS, CHUNK, CHUNK), lambda i: (0, 0, 0)),
                  pl.BlockSpec((CHUNK, SGU_GROUPS), lambda i: (0, 0))],
        out_specs=pl.BlockSpec((SGU_TM, wdt), lambda i: (i, 0)),
        out_shape=jax.ShapeDtypeStruct((s, wdt), BF16),
        compiler_params=_cparams(("parallel",)),
        name="sgu",
    )(u, v, ws, bt)


def _tile_lanes(x, n):
    return x if n == 1 else jnp.concatenate([x] * n, axis=1)


def _count_keys(keys_ref, n_blocks, pred):
    tq = keys_ref.shape[1]

    def body(it, acc):
        off = pl.multiple_of(it * CNT_BLK, CNT_BLK)
        blk = keys_ref[pl.ds(off, CNT_BLK), :].reshape(CNT_BLK // SUBLANES, SUBLANES, tq)
        ind = jnp.where(pred(blk, off), 1, 0)
        return acc + ind.reshape(-1, CNT_ACC, SUBLANES, tq).sum(axis=0)

    acc = lax.fori_loop(0, n_blocks, body, jnp.zeros((CNT_ACC, SUBLANES, tq), I32))
    tot = jnp.sum(acc.sum(axis=0).astype(F32), axis=0, keepdims=True)
    return jnp.broadcast_to(tot, (SUBLANES, tq)).astype(I32)


def _attn_kernel(itab_ref, jtab_ref, qit_ref, wit_ref, ki_ref, q_ref, kt_ref, v_ref, o_ref,
                 keys_ref, r_ref, thr_ref, m_ref, l_ref, acc_ref, *, alibi_c, topk, idx_bits):
    i = itab_ref[pl.program_id(0)]
    j = jtab_ref[pl.program_id(0)]
    tq, tk = ATT_TQ, ATT_TK
    hw = HEAD_DIM + AUG
    q0 = i * tq
    n_kv = (q0 + tq + tk - 1) // tk
    n_blk = (q0 + tq + CNT_BLK - 1) // CNT_BLK

    @pl.when(j == 0)
    def _index_and_threshold():
        m_ref[...] = jnp.full(m_ref.shape, NEG, F32)
        l_ref[...] = jnp.zeros(l_ref.shape, F32)
        acc_ref[...] = jnp.zeros(acc_ref.shape, F32)

        def chunk_body(c, carry):
            off = pl.multiple_of(c * IDX_TK, IDX_TK)
            r_ref[...] = jnp.dot(ki_ref[pl.ds(off, IDX_TK), :], qit_ref[0],
                                 preferred_element_type=F32)
            s_idx = off + lax.broadcasted_iota(I32, (IDX_TK, LANES), 0)
            for g in range(tq // LANES):
                lanes = slice(g * LANES, (g + 1) * LANES)
                t_idx = q0 + g * LANES + lax.broadcasted_iota(I32, (IDX_TK, LANES), 1)
                acc = jnp.zeros((IDX_TK, LANES), F32)
                for h in range(IDX_HEADS):
                    rr = r_ref[:, h * tq + g * LANES:h * tq + (g + 1) * LANES]
                    acc = acc + jnp.maximum(rr, 0.0) * wit_ref[h:h + 1, lanes]
                bits = lax.bitcast_convert_type(acc, I32)
                key = bits ^ ((bits >> 31) & 0x7FFFFFFF)
                keys_ref[pl.ds(off, IDX_TK), lanes] = jnp.where(s_idx <= t_idx, key, INT_MIN)
            return carry

        lax.fori_loop(0, (q0 + tq) // IDX_TK, chunk_body, 0)

        def fill_body(f, carry):
            off = pl.multiple_of(q0 + tq + f * tq, tq)
            keys_ref[pl.ds(off, tq), :] = jnp.full((tq, tq), INT_MIN, I32)
            return carry

        lax.fori_loop(0, (n_blk * CNT_BLK - (q0 + tq)) // tq, fill_body, 0)

        t_row = q0 + lax.broadcasted_iota(I32, (SUBLANES, tq), 1)
        kq = jnp.minimum(topk, t_row + 1)

        def bis_cond(st):
            b, _, _, done = st
            return jnp.logical_and(b < 32, done == 0)

        def bis_body(st):
            b, thr, cnt, _ = st
            cand = jnp.where(b == 0, 0, thr + lax.shift_left(jnp.int32(1), 31 - b))
            tot = _count_keys(keys_ref, n_blk, lambda blk, off: blk >= cand[None])
            take = tot >= kq
            thr = jnp.where(take, cand, thr)
            cnt = jnp.where(take, tot, cnt)
            done = (jnp.min(jnp.where(cnt == kq, 1.0, 0.0)) > 0.5).astype(I32)
            return b + 1, thr, cnt, done

        _, thr, cnt, _ = lax.while_loop(
            bis_cond, bis_body,
            (jnp.int32(0), jnp.full((SUBLANES, tq), INT_MIN, I32),
             jnp.full((SUBLANES, tq), -1, I32), jnp.int32(0)))
        thr_ref[...] = thr

        @pl.when(jnp.max(jnp.where(cnt > kq, 1.0, 0.0)) > 0.5)
        def _break_ties():
            def row_idx(off):
                return off + lax.broadcasted_iota(I32, (CNT_BLK, tq), 0).reshape(
                    CNT_BLK // SUBLANES, SUBLANES, tq)

            above = _count_keys(keys_ref, n_blk, lambda blk, off: blk > thr[None])
            need = kq - above

            def cut_body(b, p):
                c = p + lax.shift_left(jnp.int32(1), idx_bits - 1 - b)
                below = _count_keys(
                    keys_ref, n_blk, lambda blk, off: (blk == thr[None]) & (row_idx(off) < c[None]))
                return jnp.where(below < need, c, p)

            p = lax.fori_loop(0, idx_bits, cut_body, jnp.zeros((SUBLANES, tq), I32))

            def demote_body(it, carry):
                off = pl.multiple_of(it * CNT_BLK, CNT_BLK)
                blk = keys_ref[pl.ds(off, CNT_BLK), :].reshape(CNT_BLK // SUBLANES, SUBLANES, tq)
                new = jnp.where((blk == thr[None]) & (row_idx(off) > p[None]), thr[None] - 1, blk)
                keys_ref[pl.ds(off, CNT_BLK), :] = new.reshape(CNT_BLK, tq)
                return carry

            lax.fori_loop(0, n_blk, demote_body, 0)

    k0 = pl.multiple_of(j * tk, tk)
    nl = tk // LANES
    sel_t = keys_ref[pl.ds(k0, tk), :] >= thr_ref[0:1, :]
    bias = jnp.transpose(jnp.where(sel_t, 0.0, NEG))
    tile_dist = jnp.full((tq, LANES), k0 - q0, I32).astype(F32)
    def logits(h):
        s = jnp.dot(q_ref[:, h * hw:(h + 1) * hw], kt_ref[h * hw:(h + 1) * hw, :],
                    preferred_element_type=F32) + bias
        return s, jnp.max(s, axis=-1, keepdims=True)

    def probs(h, s, s_max):
        shift = tile_dist * alibi_c[h]
        m_prev = m_ref[h]
        m_new = jnp.maximum(m_prev, s_max + shift)
        alpha = jnp.exp2(m_prev - m_new)
        p = jnp.exp2(s - _tile_lanes(m_new - shift, nl))
        l_ref[h] = alpha * l_ref[h] + jnp.sum(p, axis=-1, keepdims=True)
        m_ref[h] = m_new
        return p.astype(BF16), alpha

    def accumulate(h, p, alpha):
        cols = slice(h * HEAD_DIM, (h + 1) * HEAD_DIM)
        acc_ref[:, cols] = alpha * acc_ref[:, cols] + jnp.dot(
            p, v_ref[:, cols], preferred_element_type=F32)

    st_a, st_b = {}, {}
    for step in range(ATTN_HEADS + 2):
        if step < ATTN_HEADS:
            st_a[step] = logits(step)
        if 0 <= step - 1 < ATTN_HEADS:
            st_b[step - 1] = probs(step - 1, *st_a.pop(step - 1))
        if 0 <= step - 2 < ATTN_HEADS:
            accumulate(step - 2, *st_b.pop(step - 2))

    @pl.when(j == n_kv - 1)
    def _finish():
        for h in range(ATTN_HEADS):
            cols = slice(h * HEAD_DIM, (h + 1) * HEAD_DIM)
            o_ref[:, cols] = (acc_ref[:, cols] / l_ref[h]).astype(o_ref.dtype)


def _alibi_pieces():
    out = []
    for h in range(ATTN_HEADS):
        c = np.float32(LOG2E * 2.0 ** (-8.0 * (h + 1) / ATTN_HEADS))
        c1 = np.float32(c.astype(BF16))
        c2 = np.float32(np.float32(c - c1).astype(BF16))
        c3 = np.float32(np.float32(c - c1 - c2).astype(BF16))
        out.append((float(c1), float(c2), float(c3)))
    return out


def _alibi_augment(qn, kn, pieces):
    s = qn.shape[0]
    cp = jnp.asarray(pieces, F32)
    idx = jnp.arange(AUG)
    t_loc = jnp.arange(s, dtype=I32) % ATT_TQ
    u_loc = jnp.arange(s, dtype=I32) % ATT_TK

    def table(hi, lo, first):
        hi3 = (idx >= first) & (idx < first + 3)
        lo3 = (idx >= first + 3) & (idx < first + 6)
        return (jnp.where(hi3[None], (LANES * (hi // LANES))[:, None], 0)
                + jnp.where(lo3[None], (lo % LANES)[:, None], 0)).astype(F32)

    consts = jnp.zeros((ATTN_HEADS, AUG), F32)
    qa = (consts.at[:, 0:3].set(cp).at[:, 3:6].set(cp)[None]
          + table(t_loc, t_loc, 6)[:, None, :])
    ka = (consts.at[:, 6:9].set(-cp).at[:, 9:12].set(-cp)[:, :, None]
          + jnp.transpose(table(u_loc, u_loc, 0))[None])
    q_aug = jnp.concatenate([qn.reshape(s, ATTN_HEADS, HEAD_DIM), qa.astype(BF16)], axis=-1)
    kt_aug = jnp.concatenate([jnp.transpose(kn).reshape(ATTN_HEADS, HEAD_DIM, s), ka.astype(BF16)], axis=1)
    return q_aug.reshape(s, -1), kt_aug.reshape(-1, s)


def _attention(qit, wit, ki, qn, kn, v):
    s = qn.shape[0]
    tq, tk = ATT_TQ, ATT_TK
    hw = HEAD_DIM + AUG
    assert s % CNT_BLK == 0 and CNT_BLK % tk == 0 and CNT_BLK % tq == 0 and tq % LANES == 0
    topk = min(TOPK_MAX, s // 4)
    idx_bits = max(1, (s - 1).bit_length())
    pieces = _alibi_pieces()
    alibi_c = tuple(float(np.float32(p[0]) + np.float32(p[1]) + np.float32(p[2])) for p in pieces)
    q_aug, kt_aug = _alibi_augment(qn, kn, pieces)

    steps = [(i, j) for i in range(s // tq) for j in range(((i + 1) * tq + tk - 1) // tk)]
    itab = jnp.asarray(np.array([p[0] for p in steps], np.int32))
    jtab = jnp.asarray(np.array([p[1] for p in steps], np.int32))

    grid_spec = pltpu.PrefetchScalarGridSpec(
        num_scalar_prefetch=2,
        grid=(len(steps),),
        in_specs=[pl.BlockSpec((1, IDX_DIM, IDX_HEADS * tq), lambda t, it, jt: (it[t], 0, 0)),
                  pl.BlockSpec((IDX_HEADS, tq), lambda t, it, jt: (0, it[t])),
                  pl.BlockSpec((s, IDX_DIM), lambda t, it, jt: (0, 0)),
                  pl.BlockSpec((tq, ATTN_HEADS * hw), lambda t, it, jt: (it[t], 0)),
                  pl.BlockSpec((ATTN_HEADS * hw, tk), lambda t, it, jt: (0, jt[t])),
                  pl.BlockSpec((tk, ATTN_WIDTH), lambda t, it, jt: (jt[t], 0))],
        out_specs=pl.BlockSpec((tq, ATTN_WIDTH), lambda t, it, jt: (it[t], 0)),
        scratch_shapes=[pltpu.VMEM((s, tq), I32),
                        pltpu.VMEM((IDX_TK, IDX_HEADS * tq), F32),
                        pltpu.VMEM((SUBLANES, tq), I32),
                        pltpu.VMEM((ATTN_HEADS, tq, LANES), F32),
                        pltpu.VMEM((ATTN_HEADS, tq, LANES), F32),
                        pltpu.VMEM((tq, ATTN_WIDTH), F32)])
    return pl.pallas_call(
        functools.partial(_attn_kernel, alibi_c=alibi_c, topk=topk, idx_bits=idx_bits),
        grid_spec=grid_spec,
        out_shape=jax.ShapeDtypeStruct((s, ATTN_WIDTH), BF16),
        compiler_params=_cparams(("arbitrary",)),
        name="dsa_attention",
    )(itab, jtab, qit, wit, ki, q_aug, kt_aug, v)


def _merge_kernel(ya_ref, yb_ref, sg_a_ref, sg_b_ref, x_ref, gt_ref, wa_ref, wb_ref, wo_ref, o_ref):
    a = jnp.dot(ya_ref[...], wa_ref[...], preferred_element_type=F32)
    b = jnp.dot(yb_ref[...], wb_ref[...], preferred_element_type=F32)
    merged = sg_a_ref[...].astype(F32) * a + sg_b_ref[...].astype(F32) * b
    o_ref[...] = x_ref[...] + gt_ref[...] * jnp.dot(
        merged.astype(BF16), wo_ref[...], preferred_element_type=F32)


def _merge(ya, yb, sg, x, gt, wa, wb, wo):
    s, d = x.shape
    tm = MERGE_TM
    wdt = ya.shape[1]
    const = lambda i: (0, 0)
    return pl.pallas_call(
        _merge_kernel,
        grid=(s // tm,),
        in_specs=[pl.BlockSpec((tm, wdt), lambda i: (i, 0)),
                  pl.BlockSpec((tm, wdt), lambda i: (i, 0)),
                  pl.BlockSpec((tm, d), lambda i: (i, 0)),
                  pl.BlockSpec((tm, d), lambda i: (i, 1)),
                  pl.BlockSpec((tm, d), lambda i: (i, 0)),
                  pl.BlockSpec((1, d), const),
                  pl.BlockSpec((wdt, d), const),
                  pl.BlockSpec((wdt, d), const),
                  pl.BlockSpec((d, d), const)],
        out_specs=pl.BlockSpec((tm, d), lambda i: (i, 0)),
        out_shape=jax.ShapeDtypeStruct((s, d), F32),
        compiler_params=_cparams(("parallel",)),
        name="merge",
    )(ya, yb, sg, sg, x, gt, wa, wb, wo)


def _pad_cols(w, n):
    return jnp.pad(w, ((0, 0), (0, n - w.shape[1])))


def _layer(x, c, w_ada, b_ada, g_norm1, w1_gate, w1_up, w1_down, g_norm2, w_in, g_sgu, w_spatial,
           b_spatial, g_q, g_k, g_kidx, b_kidx, w_branch_a, w_branch_b, w_out, g_norm3,
           w2_gate, w2_up, w2_down):
    s, d = x.shape
    dff = w1_gate.shape[1]
    dff_p = -(-dff // FFN_TF) * FFN_TF

    ada = _ada(c.reshape(d, 1), w_ada, b_ada.reshape(1, -1))
    sh1, sc1, gt1, sh2, sc2, gt2, sh3, sc3, gt3 = [ada[:, k * d:(k + 1) * d] for k in range(N_ADA)]

    def ffn_weights(wg, wu, wd):
        return (_pad_cols(wg.astype(BF16), dff_p), _pad_cols(wu.astype(BF16), dff_p),
                jnp.pad(wd.astype(BF16), ((0, dff_p - dff), (0, 0))))

    x1 = _ffn(x, g_norm1.reshape(1, d), sh1, sc1, gt1, *ffn_weights(w1_gate, w1_up, w1_down))

    n2 = _normmod(x1, g_norm2.reshape(1, d), sh2, sc2)
    wi_b = w_in.astype(BF16)
    o_u, o_v, o_q = 0, SGU_WIDTH, 2 * SGU_WIDTH
    o_k, o_vv = o_q + ATTN_WIDTH, o_q + 2 * ATTN_WIDTH
    o_qi = o_vv + ATTN_WIDTH
    o_ki = o_qi + IDX_HEADS * IDX_DIM
    o_g = o_ki + IDX_DIM + IDX_HEADS

    ug = _proj(n2, wi_b[:, o_u:o_v], "gelu")
    vn = _proj(n2, wi_b[:, o_v:o_q], "gelu_gnorm", gain=g_sgu.reshape(1, SGU_WIDTH))
    qk_gain = jnp.concatenate([jnp.tile(g_q, ATTN_HEADS), jnp.tile(g_k, ATTN_HEADS)]).reshape(1, -1)
    qn = _proj(n2, wi_b[:, o_q:o_k], "gnorm", gain=qk_gain[:, :ATTN_WIDTH],
               post_scale=HEAD_DIM ** -0.5 * LOG2E)
    kn = _proj(n2, wi_b[:, o_k:o_vv], "gnorm", gain=qk_gain[:, ATTN_WIDTH:])
    vv = _proj(n2, wi_b[:, o_vv:o_qi], "none")
    sg = _proj(n2, wi_b[:, o_g:], "sigmoid")
    qi = _proj(n2, wi_b[:, o_qi:o_ki], "none")
    ki, wi = _kidx(n2, _pad_cols(wi_b[:, o_ki:o_g], LANES),
                   g_kidx.reshape(1, IDX_DIM), b_kidx.reshape(1, IDX_DIM))

    nq = s // ATT_TQ
    qit = qi.reshape(nq, ATT_TQ, IDX_HEADS, IDX_DIM).transpose(0, 3, 2, 1).reshape(
        nq, IDX_DIM, IDX_HEADS * ATT_TQ)

    y_a = _sgu(ug, vn, w_spatial, jnp.transpose(b_spatial))
    y_b = _attention(qit, jnp.transpose(wi), ki, qn, kn, vv)

    x2 = _merge(y_a, y_b, sg, x1, gt2, w_branch_a.astype(BF16), w_branch_b.astype(BF16),
                w_out.astype(BF16))

    return _ffn(x2, g_norm3.reshape(1, d), sh3, sc3, gt3, *ffn_weights(w2_gate, w2_up, w2_down))


def kernel(x, c, w_ada, b_ada, g_norm1, w1_gate, w1_up, w1_down, g_norm2, w_in, g_sgu, w_spatial,
           b_spatial, g_q, g_k, g_kidx, b_kidx, w_branch_a, w_branch_b, w_out, g_norm3,
           w2_gate, w2_up, w2_down):
    batch, depth = x.shape[0], w_ada.shape[0]
    outs = []
    for b in range(batch):
        xb = x[b]
        for l in range(depth):
            xb = _layer(xb, c[b], w_ada[l], b_ada[l], g_norm1[l], w1_gate[l], w1_up[l], w1_down[l],
                        g_norm2[l], w_in[l], g_sgu[l], w_spatial[l], b_spatial[l], g_q[l], g_k[l],
                        g_kidx[l], b_kidx[l], w_branch_a[l], w_branch_b[l], w_out[l], g_norm3[l],
                        w2_gate[l], w2_up[l], w2_down[l])
        outs.append(xb)
    return jnp.stack(outs)
```

```python
import functools

import numpy as np
import jax
import jax.numpy as jnp
from jax import lax
from jax.experimental import pallas as pl
from jax.experimental.pallas import tpu as pltpu

F32 = jnp.float32
BF16 = jnp.bfloat16
I32 = jnp.int32

CHUNK = 128
SGU_GROUPS = 8
SGU_WIDTH = 1024
ATTN_HEADS = 8
HEAD_DIM = 128
ATTN_WIDTH = ATTN_HEADS * HEAD_DIM
IDX_HEADS = 16
IDX_DIM = 64
TOPK_MAX = 256
N_ADA = 9
EPS = 1e-6
NEG = -1e30
INT_MIN = -(2 ** 31)
INT_MAX = 2 ** 31 - 1

LANES = 128
SUBLANES = 8
VMEM_LIMIT = 56 * 1024 * 1024

FFN_TM = 512
FFN_TF = 512
PROJ_TM = 1024
PROJ_TN = 512
SGU_TM = 512
MERGE_TM = 256
ATT_TQ = 256
ATT_TK = 512
IDX_TK = 256
CNT_BLK = 1024
CNT_ACC = 8
AUG = 128
LOG2E = 1.4426950408889634


def _cparams(sem):
    return pltpu.CompilerParams(dimension_semantics=sem, vmem_limit_bytes=VMEM_LIMIT)


def _gelu_tanh(x):
    return 0.5 * x * (1.0 + jnp.tanh(0.7978845608028654 * (x + 0.044715 * (x * x * x))))


def _rms_mod(x, g, sh, sc):
    ms = jnp.mean(x * x, axis=-1, keepdims=True)
    y = (x * lax.rsqrt(ms + EPS)) * g
    return y * (1.0 + sc) + sh


def _ada_kernel(c_ref, w_ref, b_ref, o_ref, sb_ref):
    d = w_ref.shape[0]
    tn = w_ref.shape[1]

    @pl.when(pl.program_id(0) == 0)
    def _():
        cc = c_ref[...]
        sb_ref[...] = jnp.broadcast_to(cc * jax.nn.sigmoid(cc), (d, LANES))

    sb = sb_ref[...]
    for cb in range(tn // LANES):
        cols = slice(cb * LANES, (cb + 1) * LANES)
        prod = (w_ref[:, cols] * sb).reshape(d // SUBLANES, SUBLANES, LANES).sum(axis=0)
        o_ref[:, cols] = prod.sum(axis=0, keepdims=True) + b_ref[:, cols]


def _ada(c_col, w, b):
    d, n = w.shape
    tn = 1024
    return pl.pallas_call(
        _ada_kernel,
        grid=(n // tn,),
        in_specs=[pl.BlockSpec((d, 1), lambda j: (0, 0)),
                  pl.BlockSpec((d, tn), lambda j: (0, j)),
                  pl.BlockSpec((1, tn), lambda j: (0, j))],
        out_specs=pl.BlockSpec((1, tn), lambda j: (0, j)),
        out_shape=jax.ShapeDtypeStruct((1, n), F32),
        scratch_shapes=[pltpu.VMEM((d, LANES), F32)],
        compiler_params=_cparams(("arbitrary",)),
        name="ada",
    )(c_col, w, b)


def _ffn_kernel(x_ref, g_ref, sh_ref, sc_ref, gt_ref, wg_ref, wu_ref, wd_ref, o_ref, n_ref):
    j = pl.program_id(1)

    @pl.when(j == 0)
    def _():
        n_ref[...] = _rms_mod(x_ref[...], g_ref[...], sh_ref[...], sc_ref[...]).astype(BF16)

    n = n_ref[...]
    g = jnp.dot(n, wg_ref[...], preferred_element_type=F32)
    u = jnp.dot(n, wu_ref[...], preferred_element_type=F32)
    h = ((g * jax.nn.sigmoid(g)) * u).astype(BF16)
    d = jnp.dot(h, wd_ref[...], preferred_element_type=F32)

    @pl.when(j == 0)
    def _():
        o_ref[...] = d

    @pl.when(j > 0)
    def _():
        o_ref[...] += d

    @pl.when(j == pl.num_programs(1) - 1)
    def _():
        o_ref[...] = x_ref[...] + (0.5 * gt_ref[...]) * o_ref[...]


def _ffn(x, g, sh, sc, gt, wg, wu, wd):
    s, d = x.shape
    f = wg.shape[1]
    tm = min(FFN_TM, s)
    row = pl.BlockSpec((1, d), lambda i, j: (0, 0))
    return pl.pallas_call(
        _ffn_kernel,
        grid=(s // tm, f // FFN_TF),
        in_specs=[pl.BlockSpec((tm, d), lambda i, j: (i, 0)), row, row, row, row,
                  pl.BlockSpec((d, FFN_TF), lambda i, j: (0, j)),
                  pl.BlockSpec((d, FFN_TF), lambda i, j: (0, j)),
                  pl.BlockSpec((FFN_TF, d), lambda i, j: (j, 0))],
        out_specs=pl.BlockSpec((tm, d), lambda i, j: (i, 0)),
        out_shape=jax.ShapeDtypeStruct((s, d), F32),
        scratch_shapes=[pltpu.VMEM((tm, d), BF16)],
        compiler_params=_cparams(("parallel", "arbitrary")),
        name="ffn",
    )(x, g, sh, sc, gt, wg, wu, wd)


def _normmod_kernel(x_ref, g_ref, sh_ref, sc_ref, o_ref):
    o_ref[...] = _rms_mod(x_ref[...], g_ref[...], sh_ref[...], sc_ref[...]).astype(o_ref.dtype)


def _normmod(x, g, sh, sc):
    s, d = x.shape
    tm = 512
    row = pl.BlockSpec((1, d), lambda i: (0, 0))
    return pl.pallas_call(
        _normmod_kernel,
        grid=(s // tm,),
        in_specs=[pl.BlockSpec((tm, d), lambda i: (i, 0)), row, row, row],
        out_specs=pl.BlockSpec((tm, d), lambda i: (i, 0)),
        out_shape=jax.ShapeDtypeStruct((s, d), BF16),
        compiler_params=_cparams(("parallel",)),
        name="normmod",
    )(x, g, sh, sc)


def _group_rms(z, gain, post_scale):
    outs = []
    for gidx in range(z.shape[1] // LANES):
        cols = slice(gidx * LANES, (gidx + 1) * LANES)
        zg = z[:, cols]
        ms = jnp.mean(zg * zg, axis=-1, keepdims=True)
        y = (zg * lax.rsqrt(ms + EPS)) * gain[:, cols]
        if post_scale != 1.0:
            y = y * post_scale
        outs.append(y)
    return jnp.concatenate(outs, axis=-1)


def _proj_kernel(n_ref, w_ref, *rest, mode, post_scale):
    o_ref = rest[-1]
    z = jnp.dot(n_ref[...], w_ref[...], preferred_element_type=F32)
    if mode == "gelu":
        out = _gelu_tanh(z)
    elif mode == "gelu_gnorm":
        out = _group_rms(_gelu_tanh(z), rest[0][...], post_scale)
    elif mode == "gnorm":
        out = _group_rms(z, rest[0][...], post_scale)
    elif mode == "sigmoid":
        out = jax.nn.sigmoid(z)
    else:
        out = z
    o_ref[...] = out.astype(o_ref.dtype)


def _proj(n, w, mode, gain=None, post_scale=1.0, out_dtype=BF16):
    s, d = n.shape
    nout = w.shape[1]
    in_specs = [pl.BlockSpec((PROJ_TM, d), lambda i, j: (i, 0)),
                pl.BlockSpec((d, PROJ_TN), lambda i, j: (0, j))]
    args = [n, w]
    if gain is not None:
        in_specs.append(pl.BlockSpec((1, PROJ_TN), lambda i, j: (0, j)))
        args.append(gain)
    return pl.pallas_call(
        functools.partial(_proj_kernel, mode=mode, post_scale=post_scale),
        grid=(s // PROJ_TM, nout // PROJ_TN),
        in_specs=in_specs,
        out_specs=pl.BlockSpec((PROJ_TM, PROJ_TN), lambda i, j: (i, j)),
        out_shape=jax.ShapeDtypeStruct((s, nout), out_dtype),
        compiler_params=_cparams(("parallel", "arbitrary")),
        name="proj_" + mode,
    )(*args)


def _kidx_kernel(n_ref, w_ref, g_ref, b_ref, ki_ref, wi_ref):
    z = jnp.dot(n_ref[...], w_ref[...], preferred_element_type=F32)
    ki = z[:, :IDX_DIM]
    mu = jnp.mean(ki, axis=-1, keepdims=True)
    var = jnp.mean(jnp.square(ki - mu), axis=-1, keepdims=True)
    y = (ki - mu) * lax.rsqrt(var + EPS)
    ki_ref[...] = (y * g_ref[...] + b_ref[...]).astype(ki_ref.dtype)
    wi_ref[...] = z[:, IDX_DIM:IDX_DIM + IDX_HEADS] * (IDX_HEADS ** -0.5 * IDX_DIM ** -0.5)


def _kidx(n, w, g, b):
    s, d = n.shape
    tm = 512
    return pl.pallas_call(
        _kidx_kernel,
        grid=(s // tm,),
        in_specs=[pl.BlockSpec((tm, d), lambda i: (i, 0)),
                  pl.BlockSpec((d, LANES), lambda i: (0, 0)),
                  pl.BlockSpec((1, IDX_DIM), lambda i: (0, 0)),
                  pl.BlockSpec((1, IDX_DIM), lambda i: (0, 0))],
        out_specs=[pl.BlockSpec((tm, IDX_DIM), lambda i: (i, 0)),
                   pl.BlockSpec((tm, IDX_HEADS), lambda i: (i, 0))],
        out_shape=[jax.ShapeDtypeStruct((s, IDX_DIM), BF16),
                   jax.ShapeDtypeStruct((s, IDX_HEADS), F32)],
        compiler_params=_cparams(("parallel",)),
        name="proj_kidx",
    )(n, w, g, b)


def _sgu_kernel(u_ref, v_ref, ws_ref, bt_ref, o_ref):
    tm = u_ref.shape[0]
    r = lax.broadcasted_iota(I32, (CHUNK, CHUNK), 0)
    c = lax.broadcasted_iota(I32, (CHUNK, CHUNK), 1)
    causal = c <= r
    for g in range(SGU_GROUPS):
        cols = slice(g * LANES, (g + 1) * LANES)
        w = jnp.where(causal, ws_ref[g], 0.0).astype(BF16)
        bcol = bt_ref[:, g:g + 1]
        for ch in range(tm // CHUNK):
            rows = slice(ch * CHUNK, (ch + 1) * CHUNK)
            sv = jnp.dot(w, v_ref[rows, cols], preferred_element_type=F32) + bcol
            o_ref[rows, cols] = (u_ref[rows, cols].astype(F32) * sv).astype(o_ref.dtype)


def _sgu(u, v, ws, bt):
    s, wdt = u.shape
    return pl.pallas_call(
        _sgu_kernel,
        grid=(s // SGU_TM,),
        in_specs=[pl.BlockSpec((SGU_TM, wdt), lambda i: (i, 0)),
                  pl.BlockSpec((SGU_TM, wdt), lambda i: (i, 0)),
                  pl.BlockSpec((SGU_GROUPS, CHUNK, CHUNK), lambda i: (0, 0, 0)),
                  pl.BlockSpec((CHUNK, SGU_GROUPS), lambda i: (0, 0))],
        out_specs=pl.BlockSpec((SGU_TM, wdt), lambda i: (i, 0)),
        out_shape=jax.ShapeDtypeStruct((s, wdt), BF16),
        compiler_params=_cparams(("parallel",)),
        name="sgu",
    )(u, v, ws, bt)


def _tile_lanes(x, n):
    return x if n == 1 else jnp.concatenate([x] * n, axis=1)


def _sublane_all(x, op):
    for shift in (4, 2, 1):
        x = op(x, pltpu.roll(x, shift, 0))
    return x


def _count_keys(keys_ref, n_blocks, pred):
    tq = keys_ref.shape[1]

    def body(it, acc):
        off = pl.multiple_of(it * CNT_BLK, CNT_BLK)
        blk = keys_ref[pl.ds(off, CNT_BLK), :].reshape(CNT_BLK // SUBLANES, SUBLANES, tq)
        ind = jnp.where(pred(blk, off), 1, 0)
        return acc + ind.reshape(-1, CNT_ACC, SUBLANES, tq).sum(axis=0)

    acc = lax.fori_loop(0, n_blocks, body, jnp.zeros((CNT_ACC, SUBLANES, tq), I32))
    tot = jnp.sum(acc.sum(axis=0).astype(F32), axis=0, keepdims=True)
    return jnp.broadcast_to(tot, (SUBLANES, tq)).astype(I32)


def _attn_kernel(itab_ref, jtab_ref, qit_ref, wit_ref, ki_ref, q_ref, kt_ref, v_ref, o_ref,
                 keys_ref, r_ref, gmax_ref, thr_ref, m_ref, acc_ref, *, alibi_c, topk, idx_bits):
    i = itab_ref[pl.program_id(0)]
    j = jtab_ref[pl.program_id(0)]
    tq, tk = ATT_TQ, ATT_TK
    hw = HEAD_DIM + AUG
    q0 = i * tq
    n_kv = (q0 + tq + tk - 1) // tk
    n_blk = (q0 + tq + CNT_BLK - 1) // CNT_BLK

    @pl.when(j == 0)
    def _index_and_threshold():
        m_ref[...] = jnp.full(m_ref.shape, NEG, F32)
        acc_ref[...] = jnp.zeros(acc_ref.shape, F32)

        def chunk_body(c, carry):
            off = pl.multiple_of(c * IDX_TK, IDX_TK)
            r_ref[...] = jnp.dot(ki_ref[pl.ds(off, IDX_TK), :], qit_ref[0],
                                 preferred_element_type=F32)
            s_idx = off + lax.broadcasted_iota(I32, (IDX_TK, LANES), 0)
            for g in range(tq // LANES):
                lanes = slice(g * LANES, (g + 1) * LANES)
                t_idx = q0 + g * LANES + lax.broadcasted_iota(I32, (IDX_TK, LANES), 1)
                acc = jnp.zeros((IDX_TK, LANES), F32)
                for h in range(IDX_HEADS):
                    rr = r_ref[:, h * tq + g * LANES:h * tq + (g + 1) * LANES]
                    acc = acc + jnp.maximum(rr, 0.0) * wit_ref[h:h + 1, lanes]
                bits = lax.bitcast_convert_type(acc, I32)
                key = bits ^ ((bits >> 31) & 0x7FFFFFFF)
                key = jnp.where(s_idx <= t_idx, key, INT_MIN)
                keys_ref[pl.ds(off, IDX_TK), lanes] = key
                gmax_ref[:, lanes] = jnp.maximum(gmax_ref[:, lanes], key)
            return carry

        gmax_ref[...] = jnp.full(gmax_ref.shape, INT_MIN, I32)
        lax.fori_loop(0, (q0 + tq) // IDX_TK, chunk_body, 0)

        def fill_body(f, carry):
            off = pl.multiple_of(q0 + tq + f * tq, tq)
            keys_ref[pl.ds(off, tq), :] = jnp.full((tq, tq), INT_MIN, I32)
            return carry

        lax.fori_loop(0, (n_blk * CNT_BLK - (q0 + tq)) // tq, fill_body, 0)

        t_row = q0 + lax.broadcasted_iota(I32, (SUBLANES, tq), 1)
        kq = jnp.minimum(topk, t_row + 1)

        g = gmax_ref[...].reshape(IDX_TK // SUBLANES, SUBLANES, tq)
        hi = _sublane_all(g.max(axis=0), jnp.maximum)
        lo = _sublane_all(jnp.where(g == INT_MIN, INT_MAX, g).min(axis=0), jnp.minimum)
        n_known = jnp.min(lax.clz(lo ^ hi).astype(F32)).astype(I32)
        known = lax.shift_right_arithmetic(jnp.int32(INT_MIN), jnp.maximum(n_known - 1, 0))
        thr0 = jnp.where(n_known > 0, lo & known, INT_MIN)

        def bis_cond(st):
            b, _, _, done = st
            return jnp.logical_and(b < 32, done == 0)

        def bis_body(st):
            b, thr, cnt, _ = st
            cand = jnp.where(b == 0, 0, thr + lax.shift_left(jnp.int32(1), 31 - b))
            tot = _count_keys(keys_ref, n_blk, lambda blk, off: blk >= cand[None])
            take = tot >= kq
            thr = jnp.where(take, cand, thr)
            cnt = jnp.where(take, tot, cnt)
            done = (jnp.min(jnp.where(cnt == kq, 1.0, 0.0)) > 0.5).astype(I32)
            return b + 1, thr, cnt, done

        _, thr, cnt, _ = lax.while_loop(
            bis_cond, bis_body,
            (n_known, thr0, jnp.full((SUBLANES, tq), -1, I32), jnp.int32(0)))
        thr_ref[...] = thr
        cnt = lax.cond(jnp.min(cnt.astype(F32)) < 0.0,
                       lambda: jnp.where(cnt < 0, _count_keys(
                           keys_ref, n_blk, lambda blk, off: blk >= thr[None]), cnt),
                       lambda: cnt)

        @pl.when(jnp.max(jnp.where(cnt > kq, 1.0, 0.0)) > 0.5)
        def _break_ties():
            def row_idx(off):
                return off + lax.broadcasted_iota(I32, (CNT_BLK, tq), 0).reshape(
                    CNT_BLK // SUBLANES, SUBLANES, tq)

            above = _count_keys(keys_ref, n_blk, lambda blk, off: blk > thr[None])
            need = kq - above

            def cut_body(b, p):
                c = p + lax.shift_left(jnp.int32(1), idx_bits - 1 - b)
                below = _count_keys(
                    keys_ref, n_blk, lambda blk, off: (blk == thr[None]) & (row_idx(off) < c[None]))
                return jnp.where(below < need, c, p)

            p = lax.fori_loop(0, idx_bits, cut_body, jnp.zeros((SUBLANES, tq), I32))

            def demote_body(it, carry):
                off = pl.multiple_of(it * CNT_BLK, CNT_BLK)
                blk = keys_ref[pl.ds(off, CNT_BLK), :].reshape(CNT_BLK // SUBLANES, SUBLANES, tq)
                new = jnp.where((blk == thr[None]) & (row_idx(off) > p[None]), thr[None] - 1, blk)
                keys_ref[pl.ds(off, CNT_BLK), :] = new.reshape(CNT_BLK, tq)
                return carry

            lax.fori_loop(0, n_blk, demote_body, 0)

    k0 = pl.multiple_of(j * tk, tk)
    nl = tk // LANES
    sel_t = keys_ref[pl.ds(k0, tk), :] >= thr_ref[0:1, :]
    bias = jnp.transpose(jnp.where(sel_t, 0.0, NEG))
    tile_dist = jnp.full((tq, LANES), k0 - q0, I32).astype(F32)
    ones_col = (lax.broadcasted_iota(I32, (tk, LANES), 1) == 0).astype(BF16)

    def logits(h):
        s = jnp.dot(q_ref[:, h * hw:(h + 1) * hw], kt_ref[h * hw:(h + 1) * hw, :],
                    preferred_element_type=F32) + bias
        return s, jnp.max(s, axis=-1, keepdims=True)

    def probs(h, s, s_max):
        shift = tile_dist * alibi_c[h]
        m_prev = m_ref[h]
        m_new = jnp.maximum(m_prev, s_max + shift)
        m_ref[h] = m_new
        p = jnp.exp2(s - _tile_lanes(m_new - shift, nl)).astype(BF16)
        return p, jnp.exp2(m_prev - m_new)

    def accumulate(h, p, alpha):
        v_aug = jnp.concatenate([v_ref[:, h * HEAD_DIM:(h + 1) * HEAD_DIM], ones_col], axis=1)
        acc_ref[h] = _tile_lanes(alpha, 2) * acc_ref[h] + jnp.dot(
            p, v_aug, preferred_element_type=F32)

    st_a, st_b = {}, {}
    for step in range(ATTN_HEADS + 2):
        if 0 <= step - 2 < ATTN_HEADS:
            accumulate(step - 2, *st_b.pop(step - 2))
        if step < ATTN_HEADS:
            st_a[step] = logits(step)
        if 0 <= step - 1 < ATTN_HEADS:
            st_b[step - 1] = probs(step - 1, *st_a.pop(step - 1))

    @pl.when(j == n_kv - 1)
    def _finish():
        for h in range(ATTN_HEADS):
            acc = acc_ref[h]
            o_ref[:, h * HEAD_DIM:(h + 1) * HEAD_DIM] = (
                acc[:, :HEAD_DIM] / acc[:, HEAD_DIM:HEAD_DIM + 1]).astype(o_ref.dtype)


def _alibi_pieces():
    out = []
    for h in range(ATTN_HEADS):
        c = np.float32(LOG2E * 2.0 ** (-8.0 * (h + 1) / ATTN_HEADS))
        c1 = np.float32(c.astype(BF16))
        c2 = np.float32(np.float32(c - c1).astype(BF16))
        c3 = np.float32(np.float32(c - c1 - c2).astype(BF16))
        out.append((float(c1), float(c2), float(c3)))
    return out


def _alibi_augment(qn, kn, pieces):
    s = qn.shape[0]
    cp = jnp.asarray(pieces, F32)
    idx = jnp.arange(AUG)
    t_loc = jnp.arange(s, dtype=I32) % ATT_TQ
    u_loc = jnp.arange(s, dtype=I32) % ATT_TK

    def table(hi, lo, first):
        hi3 = (idx >= first) & (idx < first + 3)
        lo3 = (idx >= first + 3) & (idx < first + 6)
        return (jnp.where(hi3[None], (LANES * (hi // LANES))[:, None], 0)
                + jnp.where(lo3[None], (lo % LANES)[:, None], 0)).astype(F32)

    consts = jnp.zeros((ATTN_HEADS, AUG), F32)
    qa = (consts.at[:, 0:3].set(cp).at[:, 3:6].set(cp)[None]
          + table(t_loc, t_loc, 6)[:, None, :])
    ka = (consts.at[:, 6:9].set(-cp).at[:, 9:12].set(-cp)[:, :, None]
          + jnp.transpose(table(u_loc, u_loc, 0))[None])
    q_aug = jnp.concatenate([qn.reshape(s, ATTN_HEADS, HEAD_DIM), qa.astype(BF16)], axis=-1)
    kt_aug = jnp.concatenate([jnp.transpose(kn).reshape(ATTN_HEADS, HEAD_DIM, s), ka.astype(BF16)], axis=1)
    return q_aug.reshape(s, -1), kt_aug.reshape(-1, s)


def _attention(qit, wit, ki, qn, kn, v):
    s = qn.shape[0]
    tq, tk = ATT_TQ, ATT_TK
    hw = HEAD_DIM + AUG
    assert s % CNT_BLK == 0 and CNT_BLK % tk == 0 and CNT_BLK % tq == 0 and tq % LANES == 0
    topk = min(TOPK_MAX, s // 4)
    idx_bits = max(1, (s - 1).bit_length())
    assert IDX_TK >= topk and tq % IDX_TK == 0
    pieces = _alibi_pieces()
    alibi_c = tuple(float(np.float32(p[0]) + np.float32(p[1]) + np.float32(p[2])) for p in pieces)
    q_aug, kt_aug = _alibi_augment(qn, kn, pieces)

    steps = [(i, j) for i in range(s // tq) for j in range(((i + 1) * tq + tk - 1) // tk)]
    itab = jnp.asarray(np.array([p[0] for p in steps], np.int32))
    jtab = jnp.asarray(np.array([p[1] for p in steps], np.int32))

    grid_spec = pltpu.PrefetchScalarGridSpec(
        num_scalar_prefetch=2,
        grid=(len(steps),),
        in_specs=[pl.BlockSpec((1, IDX_DIM, IDX_HEADS * tq), lambda t, it, jt: (it[t], 0, 0)),
                  pl.BlockSpec((IDX_HEADS, tq), lambda t, it, jt: (0, it[t])),
                  pl.BlockSpec((s, IDX_DIM), lambda t, it, jt: (0, 0)),
                  pl.BlockSpec((tq, ATTN_HEADS * hw), lambda t, it, jt: (it[t], 0)),
                  pl.BlockSpec((ATTN_HEADS * hw, tk), lambda t, it, jt: (0, jt[t])),
                  pl.BlockSpec((tk, ATTN_WIDTH), lambda t, it, jt: (jt[t], 0))],
        out_specs=pl.BlockSpec((tq, ATTN_WIDTH), lambda t, it, jt: (it[t], 0)),
        scratch_shapes=[pltpu.VMEM((s, tq), I32),
                        pltpu.VMEM((IDX_TK, IDX_HEADS * tq), F32),
                        pltpu.VMEM((IDX_TK, tq), I32),
                        pltpu.VMEM((SUBLANES, tq), I32),
                        pltpu.VMEM((ATTN_HEADS, tq, LANES), F32),
                        pltpu.VMEM((ATTN_HEADS, tq, 2 * HEAD_DIM), F32)])
    return pl.pallas_call(
        functools.partial(_attn_kernel, alibi_c=alibi_c, topk=topk, idx_bits=idx_bits),
        grid_spec=grid_spec,
        out_shape=jax.ShapeDtypeStruct((s, ATTN_WIDTH), BF16),
        compiler_params=_cparams(("arbitrary",)),
        name="dsa_attention",
    )(itab, jtab, qit, wit, ki, q_aug, kt_aug, v)


def _merge_kernel(ya_ref, yb_ref, sg_a_ref, sg_b_ref, x_ref, gt_ref, wa_ref, wb_ref, wo_ref, o_ref):
    a = jnp.dot(ya_ref[...], wa_ref[...], preferred_element_type=F32)
    b = jnp.dot(yb_ref[...], wb_ref[...], preferred_element_type=F32)
    merged = sg_a_ref[...].astype(F32) * a + sg_b_ref[...].astype(F32) * b
    o_ref[...] = x_ref[...] + gt_ref[...] * jnp.dot(
        merged.astype(BF16), wo_ref[...], preferred_element_type=F32)


def _merge(ya, yb, sg, x, gt, wa, wb, wo):
    s, d = x.shape
    tm = MERGE_TM
    wdt = ya.shape[1]
    const = lambda i: (0, 0)
    return pl.pallas_call(
        _merge_kernel,
        grid=(s // tm,),
        in_specs=[pl.BlockSpec((tm, wdt), lambda i: (i, 0)),
                  pl.BlockSpec((tm, wdt), lambda i: (i, 0)),
                  pl.BlockSpec((tm, d), lambda i: (i, 0)),
                  pl.BlockSpec((tm, d), lambda i: (i, 1)),
                  pl.BlockSpec((tm, d), lambda i: (i, 0)),
                  pl.BlockSpec((1, d), const),
                  pl.BlockSpec((wdt, d), const),
                  pl.BlockSpec((wdt, d), const),
                  pl.BlockSpec((d, d), const)],
        out_specs=pl.BlockSpec((tm, d), lambda i: (i, 0)),
        out_shape=jax.ShapeDtypeStruct((s, d), F32),
        compiler_params=_cparams(("parallel",)),
        name="merge",
    )(ya, yb, sg, sg, x, gt, wa, wb, wo)


def _pad_cols(w, n):
    return jnp.pad(w, ((0, 0), (0, n - w.shape[1])))


def _layer(x, c, w_ada, b_ada, g_norm1, w1_gate, w1_up, w1_down, g_norm2, w_in, g_sgu, w_spatial,
           b_spatial, g_q, g_k, g_kidx, b_kidx, w_branch_a, w_branch_b, w_out, g_norm3,
           w2_gate, w2_up, w2_down):
    s, d = x.shape
    dff = w1_gate.shape[1]
    dff_p = -(-dff // FFN_TF) * FFN_TF

    ada = _ada(c.reshape(d, 1), w_ada, b_ada.reshape(1, -1))
    sh1, sc1, gt1, sh2, sc2, gt2, sh3, sc3, gt3 = [ada[:, k * d:(k + 1) * d] for k in range(N_ADA)]

    def ffn_weights(wg, wu, wd):
        return (_pad_cols(wg.astype(BF16), dff_p), _pad_cols(wu.astype(BF16), dff_p),
                jnp.pad(wd.astype(BF16), ((0, dff_p - dff), (0, 0))))

    x1 = _ffn(x, g_norm1.reshape(1, d), sh1, sc1, gt1, *ffn_weights(w1_gate, w1_up, w1_down))

    n2 = _normmod(x1, g_norm2.reshape(1, d), sh2, sc2)
    wi_b = w_in.astype(BF16)
    o_u, o_v, o_q = 0, SGU_WIDTH, 2 * SGU_WIDTH
    o_k, o_vv = o_q + ATTN_WIDTH, o_q + 2 * ATTN_WIDTH
    o_qi = o_vv + ATTN_WIDTH
    o_ki = o_qi + IDX_HEADS * IDX_DIM
    o_g = o_ki + IDX_DIM + IDX_HEADS

    ug = _proj(n2, wi_b[:, o_u:o_v], "gelu")
    vn = _proj(n2, wi_b[:, o_v:o_q], "gelu_gnorm", gain=g_sgu.reshape(1, SGU_WIDTH))
    qk_gain = jnp.concatenate([jnp.tile(g_q, ATTN_HEADS), jnp.tile(g_k, ATTN_HEADS)]).reshape(1, -1)
    qn = _proj(n2, wi_b[:, o_q:o_k], "gnorm", gain=qk_gain[:, :ATTN_WIDTH],
               post_scale=HEAD_DIM ** -0.5 * LOG2E)
    kn = _proj(n2, wi_b[:, o_k:o_vv], "gnorm", gain=qk_gain[:, ATTN_WIDTH:])
    vv = _proj(n2, wi_b[:, o_vv:o_qi], "none")
    sg = _proj(n2, wi_b[:, o_g:], "sigmoid")
    qi = _proj(n2, wi_b[:, o_qi:o_ki], "none")
    ki, wi = _kidx(n2, _pad_cols(wi_b[:, o_ki:o_g], LANES),
                   g_kidx.reshape(1, IDX_DIM), b_kidx.reshape(1, IDX_DIM))

    nq = s // ATT_TQ
    qit = qi.reshape(nq, ATT_TQ, IDX_HEADS, IDX_DIM).transpose(0, 3, 2, 1).reshape(
        nq, IDX_DIM, IDX_HEADS * ATT_TQ)

    y_a = _sgu(ug, vn, w_spatial, jnp.transpose(b_spatial))
    y_b = _attention(qit, jnp.transpose(wi), ki, qn, kn, vv)

    x2 = _merge(y_a, y_b, sg, x1, gt2, w_branch_a.astype(BF16), w_branch_b.astype(BF16),
                w_out.astype(BF16))

    return _ffn(x2, g_norm3.reshape(1, d), sh3, sc3, gt3, *ffn_weights(w2_gate, w2_up, w2_down))


def kernel(x, c, w_ada, b_ada, g_norm1, w1_gate, w1_up, w1_down, g_norm2, w_in, g_sgu, w_spatial,
           b_spatial, g_q, g_k, g_kidx, b_kidx, w_branch_a, w_branch_b, w_out, g_norm3,
           w2_gate, w2_up, w2_down):
    batch, depth = x.shape[0], w_ada.shape[0]
    outs = []
    for b in range(batch):
        xb = x[b]
        for l in range(depth):
            xb = _layer(xb, c[b], w_ada[l], b_ada[l], g_norm1[l], w1_gate[l], w1_up[l], w1_down[l],
                        g_norm2[l], w_in[l], g_sgu[l], w_spatial[l], b_spatial[l], g_q[l], g_k[l],
                        g_kidx[l], b_kidx[l], w_branch_a[l], w_branch_b[l], w_out[l], g_norm3[l],
                        w2_gate[l], w2_up[l], w2_down[l])
        outs.append(xb)
    return jnp.stack(outs)
```

```python
import functools

import numpy as np
import jax
import jax.numpy as jnp
from jax import lax
from jax.experimental import pallas as pl
from jax.experimental.pallas import tpu as pltpu

F32 = jnp.float32
BF16 = jnp.bfloat16
I32 = jnp.int32

CHUNK = 128
SGU_GROUPS = 8
SGU_WIDTH = 1024
ATTN_HEADS = 8
HEAD_DIM = 128
ATTN_WIDTH = ATTN_HEADS * HEAD_DIM
IDX_HEADS = 16
IDX_DIM = 64
TOPK_MAX = 256
N_ADA = 9
EPS = 1e-6
NEG = -1e30

LANES = 128
SUBLANES = 8
VMEM_LIMIT = 56 * 1024 * 1024

FFN_TM = 512
FFN_TF = 512
PROJ_TM = 1024
PROJ_TN = 512
SGU_TM = 512
MERGE_TM = 256
ATT_TQ = 256
ATT_TK = 1024
IDX_TK = 256
CNT_BLK = 1024
CNT_ACC = 8
BISECT_CAP = 512
AUG = 128
LOG2E = 1.4426950408889634


def _cparams(sem):
    return pltpu.CompilerParams(dimension_semantics=sem, vmem_limit_bytes=VMEM_LIMIT)


def _gelu_tanh(x):
    return 0.5 * x * (1.0 + jnp.tanh(0.7978845608028654 * (x + 0.044715 * (x * x * x))))


def _rms_mod(x, g, sh, sc):
    ms = jnp.mean(x * x, axis=-1, keepdims=True)
    y = (x * lax.rsqrt(ms + EPS)) * g
    return y * (1.0 + sc) + sh


def _ada_kernel(c_ref, w_ref, b_ref, o_ref, sb_ref):
    d = w_ref.shape[0]
    tn = w_ref.shape[1]

    @pl.when(pl.program_id(0) == 0)
    def _():
        cc = c_ref[...]
        sb_ref[...] = jnp.broadcast_to(cc * jax.nn.sigmoid(cc), (d, LANES))

    sb = sb_ref[...]
    for cb in range(tn // LANES):
        cols = slice(cb * LANES, (cb + 1) * LANES)
        prod = (w_ref[:, cols] * sb).reshape(d // SUBLANES, SUBLANES, LANES).sum(axis=0)
        o_ref[:, cols] = prod.sum(axis=0, keepdims=True) + b_ref[:, cols]


def _ada(c_col, w, b):
    d, n = w.shape
    tn = 1024
    return pl.pallas_call(
        _ada_kernel,
        grid=(n // tn,),
        in_specs=[pl.BlockSpec((d, 1), lambda j: (0, 0)),
                  pl.BlockSpec((d, tn), lambda j: (0, j)),
                  pl.BlockSpec((1, tn), lambda j: (0, j))],
        out_specs=pl.BlockSpec((1, tn), lambda j: (0, j)),
        out_shape=jax.ShapeDtypeStruct((1, n), F32),
        scratch_shapes=[pltpu.VMEM((d, LANES), F32)],
        compiler_params=_cparams(("arbitrary",)),
        name="ada",
    )(c_col, w, b)


def _ffn_kernel(x_ref, g_ref, sh_ref, sc_ref, gt_ref, wg_ref, wu_ref, wd_ref, o_ref, n_ref):
    j = pl.program_id(1)

    @pl.when(j == 0)
    def _():
        n_ref[...] = _rms_mod(x_ref[...], g_ref[...], sh_ref[...], sc_ref[...]).astype(BF16)

    n = n_ref[...]
    g = jnp.dot(n, wg_ref[...], preferred_element_type=F32)
    u = jnp.dot(n, wu_ref[...], preferred_element_type=F32)
    h = ((g * jax.nn.sigmoid(g)) * u).astype(BF16)
    d = jnp.dot(h, wd_ref[...], preferred_element_type=F32)

    @pl.when(j == 0)
    def _():
        o_ref[...] = d

    @pl.when(j > 0)
    def _():
        o_ref[...] += d

    @pl.when(j == pl.num_programs(1) - 1)
    def _():
        o_ref[...] = x_ref[...] + (0.5 * gt_ref[...]) * o_ref[...]


def _ffn(x, g, sh, sc, gt, wg, wu, wd):
    s, d = x.shape
    f = wg.shape[1]
    tm = min(FFN_TM, s)
    row = pl.BlockSpec((1, d), lambda i, j: (0, 0))
    return pl.pallas_call(
        _ffn_kernel,
        grid=(s // tm, f // FFN_TF),
        in_specs=[pl.BlockSpec((tm, d), lambda i, j: (i, 0)), row, row, row, row,
                  pl.BlockSpec((d, FFN_TF), lambda i, j: (0, j)),
                  pl.BlockSpec((d, FFN_TF), lambda i, j: (0, j)),
                  pl.BlockSpec((FFN_TF, d), lambda i, j: (j, 0))],
        out_specs=pl.BlockSpec((tm, d), lambda i, j: (i, 0)),
        out_shape=jax.ShapeDtypeStruct((s, d), F32),
        scratch_shapes=[pltpu.VMEM((tm, d), BF16)],
        compiler_params=_cparams(("parallel", "arbitrary")),
        name="ffn",
    )(x, g, sh, sc, gt, wg, wu, wd)


def _normmod_kernel(x_ref, g_ref, sh_ref, sc_ref, o_ref):
    o_ref[...] = _rms_mod(x_ref[...], g_ref[...], sh_ref[...], sc_ref[...]).astype(o_ref.dtype)


def _normmod(x, g, sh, sc):
    s, d = x.shape
    tm = 512
    row = pl.BlockSpec((1, d), lambda i: (0, 0))
    return pl.pallas_call(
        _normmod_kernel,
        grid=(s // tm,),
        in_specs=[pl.BlockSpec((tm, d), lambda i: (i, 0)), row, row, row],
        out_specs=pl.BlockSpec((tm, d), lambda i: (i, 0)),
        out_shape=jax.ShapeDtypeStruct((s, d), BF16),
        compiler_params=_cparams(("parallel",)),
        name="normmod",
    )(x, g, sh, sc)


def _group_rms(z, gain, post_scale):
    outs = []
    for gidx in range(z.shape[1] // LANES):
        cols = slice(gidx * LANES, (gidx + 1) * LANES)
        zg = z[:, cols]
        ms = jnp.mean(zg * zg, axis=-1, keepdims=True)
        y = (zg * lax.rsqrt(ms + EPS)) * gain[:, cols]
        if post_scale != 1.0:
            y = y * post_scale
        outs.append(y)
    return jnp.concatenate(outs, axis=-1)


def _proj_kernel(n_ref, w_ref, *rest, mode, post_scale):
    o_ref = rest[-1]
    z = jnp.dot(n_ref[...], w_ref[...], preferred_element_type=F32)
    if mode == "gelu":
        out = _gelu_tanh(z)
    elif mode == "gelu_gnorm":
        out = _group_rms(_gelu_tanh(z), rest[0][...], post_scale)
    elif mode == "gnorm":
        out = _group_rms(z, rest[0][...], post_scale)
    elif mode == "sigmoid":
        out = jax.nn.sigmoid(z)
    else:
        out = z
    o_ref[...] = out.astype(o_ref.dtype)


def _proj(n, w, mode, gain=None, post_scale=1.0, out_dtype=BF16):
    s, d = n.shape
    nout = w.shape[1]
    in_specs = [pl.BlockSpec((PROJ_TM, d), lambda i, j: (i, 0)),
                pl.BlockSpec((d, PROJ_TN), lambda i, j: (0, j))]
    args = [n, w]
    if gain is not None:
        in_specs.append(pl.BlockSpec((1, PROJ_TN), lambda i, j: (0, j)))
        args.append(gain)
    return pl.pallas_call(
        functools.partial(_proj_kernel, mode=mode, post_scale=post_scale),
        grid=(s // PROJ_TM, nout // PROJ_TN),
        in_specs=in_specs,
        out_specs=pl.BlockSpec((PROJ_TM, PROJ_TN), lambda i, j: (i, j)),
        out_shape=jax.ShapeDtypeStruct((s, nout), out_dtype),
        compiler_params=_cparams(("parallel", "arbitrary")),
        name="proj_" + mode,
    )(*args)


def _kidx_kernel(n_ref, w_ref, g_ref, b_ref, ki_ref, wi_ref):
    z = jnp.dot(n_ref[...], w_ref[...], preferred_element_type=F32)
    ki = z[:, :IDX_DIM]
    mu = jnp.mean(ki, axis=-1, keepdims=True)
    var = jnp.mean(jnp.square(ki - mu), axis=-1, keepdims=True)
    y = (ki - mu) * lax.rsqrt(var + EPS)
    ki_ref[...] = (y * g_ref[...] + b_ref[...]).astype(ki_ref.dtype)
    wi_ref[...] = z[:, IDX_DIM:IDX_DIM + IDX_HEADS] * (IDX_HEADS ** -0.5 * IDX_DIM ** -0.5)


def _kidx(n, w, g, b):
    s, d = n.shape
    tm = 512
    return pl.pallas_call(
        _kidx_kernel,
        grid=(s // tm,),
        in_specs=[pl.BlockSpec((tm, d), lambda i: (i, 0)),
                  pl.BlockSpec((d, LANES), lambda i: (0, 0)),
                  pl.BlockSpec((1, IDX_DIM), lambda i: (0, 0)),
                  pl.BlockSpec((1, IDX_DIM), lambda i: (0, 0))],
        out_specs=[pl.BlockSpec((tm, IDX_DIM), lambda i: (i, 0)),
                   pl.BlockSpec((tm, IDX_HEADS), lambda i: (i, 0))],
        out_shape=[jax.ShapeDtypeStruct((s, IDX_DIM), BF16),
                   jax.ShapeDtypeStruct((s, IDX_HEADS), F32)],
        compiler_params=_cparams(("parallel",)),
        name="proj_kidx",
    )(n, w, g, b)


def _sgu_kernel(u_ref, v_ref, ws_ref, bt_ref, o_ref):
    tm = u_ref.shape[0]
    r = lax.broadcasted_iota(I32, (CHUNK, CHUNK), 0)
    c = lax.broadcasted_iota(I32, (CHUNK, CHUNK), 1)
    causal = c <= r
    for g in range(SGU_GROUPS):
        cols = slice(g * LANES, (g + 1) * LANES)
        w = jnp.where(causal, ws_ref[g], 0.0).astype(BF16)
        bcol = bt_ref[:, g:g + 1]
        for ch in range(tm // CHUNK):
            rows = slice(ch * CHUNK, (ch + 1) * CHUNK)
            sv = jnp.dot(w, v_ref[rows, cols], preferred_element_type=F32) + bcol
            o_ref[rows, cols] = (u_ref[rows, cols].astype(F32) * sv).astype(o_ref.dtype)


def _sgu(u, v, ws, bt):
    s, wdt = u.shape
    return pl.pallas_call(
        _sgu_kernel,
        grid=(s // SGU_TM,),
        in_specs=[pl.BlockSpec((SGU_TM, wdt), lambda i: (i, 0)),
                  pl.BlockSpec((SGU_TM, wdt), lambda i: (i, 0)),
                  pl.BlockSpec((SGU_GROUPS, CHUNK, CHUNK), lambda i: (0, 0, 0)),
                  pl.BlockSpec((CHUNK, SGU_GROUPS), lambda i: (0, 0))],
        out_specs=pl.BlockSpec((SGU_TM, wdt), lambda i: (i, 0)),
        out_shape=jax.ShapeDtypeStruct((s, wdt), BF16),
        compiler_params=_cparams(("parallel",)),
        name="sgu",
    )(u, v, ws, bt)


def _tile_lanes(x, n):
    return x if n == 1 else jnp.concatenate([x] * n, axis=1)


def _sublane_all(x, op):
    for shift in (4, 2, 1):
        x = op(x, pltpu.roll(x, shift, 0))
    return x


def _count_keys(keys_ref, n_blocks, pred):
    tq = keys_ref.shape[1]

    def body(it, acc):
        off = pl.multiple_of(it * CNT_BLK, CNT_BLK)
        blk = keys_ref[pl.ds(off, CNT_BLK), :].reshape(CNT_BLK // SUBLANES, SUBLANES, tq)
        ind = jnp.where(pred(blk, off), 1, 0)
        return acc + ind.reshape(-1, CNT_ACC, SUBLANES, tq).sum(axis=0)

    acc = lax.fori_loop(0, n_blocks, body, jnp.zeros((CNT_ACC, SUBLANES, tq), I32))
    tot = jnp.sum(acc.sum(axis=0).astype(F32), axis=0, keepdims=True)
    return jnp.broadcast_to(tot, (SUBLANES, tq)).astype(I32)


def _attn_kernel(itab_ref, jtab_ref, qit_ref, wit_ref, ki_ref, q_ref, kt_ref, v_ref, o_ref,
                 keys_ref, r_ref, gmax_ref, thr_ref, m_ref, acc_ref, *, alibi_c, topk, idx_bits):
    i = itab_ref[pl.program_id(0)]
    j = jtab_ref[pl.program_id(0)]
    tq, tk = ATT_TQ, ATT_TK
    hw = HEAD_DIM + AUG
    q0 = i * tq
    n_kv = (q0 + tq + tk - 1) // tk
    n_blk = (q0 + tq + CNT_BLK - 1) // CNT_BLK

    @pl.when(j == 0)
    def _index_and_threshold():
        m_ref[...] = jnp.full(m_ref.shape, NEG, F32)
        acc_ref[...] = jnp.zeros(acc_ref.shape, F32)

        def chunk_body(c, carry):
            off = pl.multiple_of(c * IDX_TK, IDX_TK)
            r_ref[...] = jnp.dot(ki_ref[pl.ds(off, IDX_TK), :], qit_ref[0],
                                 preferred_element_type=F32)
            s_idx = off + lax.broadcasted_iota(I32, (IDX_TK, LANES), 0)
            for g in range(tq // LANES):
                lanes = slice(g * LANES, (g + 1) * LANES)
                t_idx = q0 + g * LANES + lax.broadcasted_iota(I32, (IDX_TK, LANES), 1)
                acc = jnp.zeros((IDX_TK, LANES), F32)
                for h in range(IDX_HEADS):
                    rr = r_ref[:, h * tq + g * LANES:h * tq + (g + 1) * LANES]
                    acc = acc + jnp.maximum(rr, 0.0) * wit_ref[h:h + 1, lanes]
                score = jnp.where(s_idx <= t_idx, acc, -jnp.inf)
                keys_ref[pl.ds(off, IDX_TK), lanes] = score
                gmax_ref[:, lanes] = jnp.maximum(gmax_ref[:, lanes], score)
            return carry

        gmax_ref[...] = jnp.full(gmax_ref.shape, -jnp.inf, F32)
        lax.fori_loop(0, (q0 + tq) // IDX_TK, chunk_body, 0)

        def fill_body(f, carry):
            off = pl.multiple_of(q0 + tq + f * tq, tq)
            keys_ref[pl.ds(off, tq), :] = jnp.full((tq, tq), -jnp.inf, F32)
            return carry

        lax.fori_loop(0, (n_blk * CNT_BLK - (q0 + tq)) // tq, fill_body, 0)

        t_row = q0 + lax.broadcasted_iota(I32, (SUBLANES, tq), 1)
        kq = jnp.minimum(topk, t_row + 1)

        g = gmax_ref[...].reshape(IDX_TK // SUBLANES, SUBLANES, tq)
        hi0 = _sublane_all(g.max(axis=0), jnp.maximum)
        lo0 = _sublane_all(jnp.where(g == -jnp.inf, jnp.inf, g).min(axis=0), jnp.minimum)

        def bis_cond(st):
            it, _, _, _, _, done = st
            return jnp.logical_and(it < BISECT_CAP, done == 0)

        def bis_body(st):
            it, lo, hi, cnt, fin, _ = st
            mid = 0.5 * lo + 0.5 * hi
            stuck = (mid <= lo) | (mid >= hi)
            cand = jnp.where(stuck, hi, mid)
            tot = _count_keys(keys_ref, n_blk, lambda blk, off: blk >= cand[None])
            take = (tot >= kq) & (fin == 0)
            lo = jnp.where(take, cand, lo)
            cnt = jnp.where(take, tot, cnt)
            hi = jnp.where(take | (fin != 0), hi, cand)
            fin = jnp.where(stuck, 1, fin)
            settled = (cnt == kq) | (fin != 0)
            done = (jnp.min(jnp.where(settled, 1.0, 0.0)) > 0.5).astype(I32)
            return it + 1, lo, hi, cnt, fin, done

        empty = lo0 > hi0
        _, thr, _, cnt, _, _ = lax.while_loop(
            bis_cond, bis_body,
            (jnp.int32(0), lo0, hi0, jnp.where(empty, 0, -1), jnp.where(empty, 1, 0), jnp.int32(0)))
        thr_ref[...] = thr
        cnt = lax.cond(jnp.min(cnt.astype(F32)) < 0.0,
                       lambda: jnp.where(cnt < 0, _count_keys(
                           keys_ref, n_blk, lambda blk, off: blk >= thr[None]), cnt),
                       lambda: cnt)

        @pl.when(jnp.max(jnp.where(cnt > kq, 1.0, 0.0)) > 0.5)
        def _break_ties():
            def row_idx(off):
                return off + lax.broadcasted_iota(I32, (CNT_BLK, tq), 0).reshape(
                    CNT_BLK // SUBLANES, SUBLANES, tq)

            above = _count_keys(keys_ref, n_blk, lambda blk, off: blk > thr[None])
            need = kq - above

            def cut_body(b, p):
                c = p + lax.shift_left(jnp.int32(1), idx_bits - 1 - b)
                below = _count_keys(
                    keys_ref, n_blk, lambda blk, off: (blk == thr[None]) & (row_idx(off) < c[None]))
                return jnp.where(below < need, c, p)

            p = lax.fori_loop(0, idx_bits, cut_body, jnp.zeros((SUBLANES, tq), I32))

            def drop_body(it, carry):
                off = pl.multiple_of(it * CNT_BLK, CNT_BLK)
                blk = keys_ref[pl.ds(off, CNT_BLK), :].reshape(CNT_BLK // SUBLANES, SUBLANES, tq)
                new = jnp.where((blk == thr[None]) & (row_idx(off) > p[None]), -jnp.inf, blk)
                keys_ref[pl.ds(off, CNT_BLK), :] = new.reshape(CNT_BLK, tq)
                return carry

            lax.fori_loop(0, n_blk, drop_body, 0)

    k0 = pl.multiple_of(j * tk, tk)
    nl = tk // LANES
    sel_t = keys_ref[pl.ds(k0, tk), :] >= thr_ref[0:1, :]
    bias = jnp.transpose(jnp.where(sel_t, 0.0, NEG))
    tile_dist = jnp.full((tq, LANES), k0 - q0, I32).astype(F32)
    ones_col = (lax.broadcasted_iota(I32, (tk, LANES), 1) == 0).astype(BF16)

    def logits(h):
        s = jnp.dot(q_ref[:, h * hw:(h + 1) * hw], kt_ref[h * hw:(h + 1) * hw, :],
                    preferred_element_type=F32) + bias
        return s, jnp.max(s, axis=-1, keepdims=True)

    def probs(h, s, s_max):
        shift = tile_dist * alibi_c[h]
        m_prev = m_ref[h]
        m_new = jnp.maximum(m_prev, s_max + shift)
        m_ref[h] = m_new
        p = jnp.exp2(s - _tile_lanes(m_new - shift, nl)).astype(BF16)
        return p, jnp.exp2(m_prev - m_new)

    def accumulate(h, p, alpha):
        v_aug = jnp.concatenate([v_ref[:, h * HEAD_DIM:(h + 1) * HEAD_DIM], ones_col], axis=1)
        acc_ref[h] = _tile_lanes(alpha, 2) * acc_ref[h] + jnp.dot(
            p, v_aug, preferred_element_type=F32)

    st_a, st_b = {}, {}
    for step in range(ATTN_HEADS + 2):
        if 0 <= step - 2 < ATTN_HEADS:
            accumulate(step - 2, *st_b.pop(step - 2))
        if step < ATTN_HEADS:
            st_a[step] = logits(step)
        if 0 <= step - 1 < ATTN_HEADS:
            st_b[step - 1] = probs(step - 1, *st_a.pop(step - 1))

    @pl.when(j == n_kv - 1)
    def _finish():
        for h in range(ATTN_HEADS):
            acc = acc_ref[h]
            o_ref[:, h * HEAD_DIM:(h + 1) * HEAD_DIM] = (
                acc[:, :HEAD_DIM] / acc[:, HEAD_DIM:HEAD_DIM + 1]).astype(o_ref.dtype)


def _alibi_pieces():
    out = []
    for h in range(ATTN_HEADS):
        c = np.float32(LOG2E * 2.0 ** (-8.0 * (h + 1) / ATTN_HEADS))
        c1 = np.float32(c.astype(BF16))
        c2 = np.float32(np.float32(c - c1).astype(BF16))
        c3 = np.float32(np.float32(c - c1 - c2).astype(BF16))
        out.append((float(c1), float(c2), float(c3)))
    return out


def _alibi_augment(qn, kn, pieces):
    s = qn.shape[0]
    cp = jnp.asarray(pieces, F32)
    idx = jnp.arange(AUG)
    t_loc = jnp.arange(s, dtype=I32) % ATT_TQ
    u_loc = jnp.arange(s, dtype=I32) % ATT_TK

    def table(hi, lo, first):
        hi3 = (idx >= first) & (idx < first + 3)
        lo3 = (idx >= first + 3) & (idx < first + 6)
        return (jnp.where(hi3[None], (LANES * (hi // LANES))[:, None], 0)
                + jnp.where(lo3[None], (lo % LANES)[:, None], 0)).astype(F32)

    consts = jnp.zeros((ATTN_HEADS, AUG), F32)
    qa = (consts.at[:, 0:3].set(cp).at[:, 3:6].set(cp)[None]
          + table(t_loc, t_loc, 6)[:, None, :])
    ka = (consts.at[:, 6:9].set(-cp).at[:, 9:12].set(-cp)[:, :, None]
          + jnp.transpose(table(u_loc, u_loc, 0))[None])
    q_aug = jnp.concatenate([qn.reshape(s, ATTN_HEADS, HEAD_DIM), qa.astype(BF16)], axis=-1)
    kt_aug = jnp.concatenate([jnp.transpose(kn).reshape(ATTN_HEADS, HEAD_DIM, s), ka.astype(BF16)], axis=1)
    return q_aug.reshape(s, -1), kt_aug.reshape(-1, s)


def _attention(qit, wit, ki, qn, kn, v):
    s = qn.shape[0]
    tq, tk = ATT_TQ, ATT_TK
    hw = HEAD_DIM + AUG
    assert s % CNT_BLK == 0 and CNT_BLK % tk == 0 and CNT_BLK % tq == 0 and tq % LANES == 0
    topk = min(TOPK_MAX, s // 4)
    idx_bits = max(1, (s - 1).bit_length())
    assert IDX_TK >= topk and tq % IDX_TK == 0
    pieces = _alibi_pieces()
    alibi_c = tuple(float(np.float32(p[0]) + np.float32(p[1]) + np.float32(p[2])) for p in pieces)
    q_aug, kt_aug = _alibi_augment(qn, kn, pieces)

    steps = [(i, j) for i in range(s // tq) for j in range(((i + 1) * tq + tk - 1) // tk)]
    itab = jnp.asarray(np.array([p[0] for p in steps], np.int32))
    jtab = jnp.asarray(np.array([p[1] for p in steps], np.int32))

    grid_spec = pltpu.PrefetchScalarGridSpec(
        num_scalar_prefetch=2,
        grid=(len(steps),),
        in_specs=[pl.BlockSpec((1, IDX_DIM, IDX_HEADS * tq), lambda t, it, jt: (it[t], 0, 0)),
                  pl.BlockSpec((IDX_HEADS, tq), lambda t, it, jt: (0, it[t])),
                  pl.BlockSpec((s, IDX_DIM), lambda t, it, jt: (0, 0)),
                  pl.BlockSpec((tq, ATTN_HEADS * hw), lambda t, it, jt: (it[t], 0)),
                  pl.BlockSpec((ATTN_HEADS * hw, tk), lambda t, it, jt: (0, jt[t])),
                  pl.BlockSpec((tk, ATTN_WIDTH), lambda t, it, jt: (jt[t], 0))],
        out_specs=pl.BlockSpec((tq, ATTN_WIDTH), lambda t, it, jt: (it[t], 0)),
        scratch_shapes=[pltpu.VMEM((s, tq), F32),
                        pltpu.VMEM((IDX_TK, IDX_HEADS * tq), F32),
                        pltpu.VMEM((IDX_TK, tq), F32),
                        pltpu.VMEM((SUBLANES, tq), F32),
                        pltpu.VMEM((ATTN_HEADS, tq, LANES), F32),
                        pltpu.VMEM((ATTN_HEADS, tq, 2 * HEAD_DIM), F32)])
    return pl.pallas_call(
        functools.partial(_attn_kernel, alibi_c=alibi_c, topk=topk, idx_bits=idx_bits),
        grid_spec=grid_spec,
        out_shape=jax.ShapeDtypeStruct((s, ATTN_WIDTH), BF16),
        compiler_params=_cparams(("arbitrary",)),
        name="dsa_attention",
    )(itab, jtab, qit, wit, ki, q_aug, kt_aug, v)


def _merge_kernel(ya_ref, yb_ref, sg_a_ref, sg_b_ref, x_ref, gt_ref, wa_ref, wb_ref, wo_ref, o_ref):
    a = jnp.dot(ya_ref[...], wa_ref[...], preferred_element_type=F32)
    b = jnp.dot(yb_ref[...], wb_ref[...], preferred_element_type=F32)
    merged = sg_a_ref[...].astype(F32) * a + sg_b_ref[...].astype(F32) * b
    o_ref[...] = x_ref[...] + gt_ref[...] * jnp.dot(
        merged.astype(BF16), wo_ref[...], preferred_element_type=F32)


def _merge(ya, yb, sg, x, gt, wa, wb, wo):
    s, d = x.shape
    tm = MERGE_TM
    wdt = ya.shape[1]
    const = lambda i: (0, 0)
    return pl.pallas_call(
        _merge_kernel,
        grid=(s // tm,),
        in_specs=[pl.BlockSpec((tm, wdt), lambda i: (i, 0)),
                  pl.BlockSpec((tm, wdt), lambda i: (i, 0)),
                  pl.BlockSpec((tm, d), lambda i: (i, 0)),
                  pl.BlockSpec((tm, d), lambda i: (i, 1)),
                  pl.BlockSpec((tm, d), lambda i: (i, 0)),
                  pl.BlockSpec((1, d), const),
                  pl.BlockSpec((wdt, d), const),
                  pl.BlockSpec((wdt, d), const),
                  pl.BlockSpec((d, d), const)],
        out_specs=pl.BlockSpec((tm, d), lambda i: (i, 0)),
        out_shape=jax.ShapeDtypeStruct((s, d), F32),
        compiler_params=_cparams(("parallel",)),
        name="merge",
    )(ya, yb, sg, sg, x, gt, wa, wb, wo)


def _pad_cols(w, n):
    return jnp.pad(w, ((0, 0), (0, n - w.shape[1])))


def _layer(x, c, w_ada, b_ada, g_norm1, w1_gate, w1_up, w1_down, g_norm2, w_in, g_sgu, w_spatial,
           b_spatial, g_q, g_k, g_kidx, b_kidx, w_branch_a, w_branch_b, w_out, g_norm3,
           w2_gate, w2_up, w2_down):
    s, d = x.shape
    dff = w1_gate.shape[1]
    dff_p = -(-dff // FFN_TF) * FFN_TF

    ada = _ada(c.reshape(d, 1), w_ada, b_ada.reshape(1, -1))
    sh1, sc1, gt1, sh2, sc2, gt2, sh3, sc3, gt3 = [ada[:, k * d:(k + 1) * d] for k in range(N_ADA)]

    def ffn_weights(wg, wu, wd):
        return (_pad_cols(wg.astype(BF16), dff_p), _pad_cols(wu.astype(BF16), dff_p),
                jnp.pad(wd.astype(BF16), ((0, dff_p - dff), (0, 0))))

    x1 = _ffn(x, g_norm1.reshape(1, d), sh1, sc1, gt1, *ffn_weights(w1_gate, w1_up, w1_down))

    n2 = _normmod(x1, g_norm2.reshape(1, d), sh2, sc2)
    wi_b = w_in.astype(BF16)
    o_u, o_v, o_q = 0, SGU_WIDTH, 2 * SGU_WIDTH
    o_k, o_vv = o_q + ATTN_WIDTH, o_q + 2 * ATTN_WIDTH
    o_qi = o_vv + ATTN_WIDTH
    o_ki = o_qi + IDX_HEADS * IDX_DIM
    o_g = o_ki + IDX_DIM + IDX_HEADS

    ug = _proj(n2, wi_b[:, o_u:o_v], "gelu")
    vn = _proj(n2, wi_b[:, o_v:o_q], "gelu_gnorm", gain=g_sgu.reshape(1, SGU_WIDTH))
    qk_gain = jnp.concatenate([jnp.tile(g_q, ATTN_HEADS), jnp.tile(g_k, ATTN_HEADS)]).reshape(1, -1)
    qn = _proj(n2, wi_b[:, o_q:o_k], "gnorm", gain=qk_gain[:, :ATTN_WIDTH],
               post_scale=HEAD_DIM ** -0.5 * LOG2E)
    kn = _proj(n2, wi_b[:, o_k:o_vv], "gnorm", gain=qk_gain[:, ATTN_WIDTH:])
    vv = _proj(n2, wi_b[:, o_vv:o_qi], "none")
    sg = _proj(n2, wi_b[:, o_g:], "sigmoid")
    qi = _proj(n2, wi_b[:, o_qi:o_ki], "none")
    ki, wi = _kidx(n2, _pad_cols(wi_b[:, o_ki:o_g], LANES),
                   g_kidx.reshape(1, IDX_DIM), b_kidx.reshape(1, IDX_DIM))

    nq = s // ATT_TQ
    qit = qi.reshape(nq, ATT_TQ, IDX_HEADS, IDX_DIM).transpose(0, 3, 2, 1).reshape(
        nq, IDX_DIM, IDX_HEADS * ATT_TQ)

    y_a = _sgu(ug, vn, w_spatial, jnp.transpose(b_spatial))
    y_b = _attention(qit, jnp.transpose(wi), ki, qn, kn, vv)

    x2 = _merge(y_a, y_b, sg, x1, gt2, w_branch_a.astype(BF16), w_branch_b.astype(BF16),
                w_out.astype(BF16))

    return _ffn(x2, g_norm3.reshape(1, d), sh3, sc3, gt3, *ffn_weights(w2_gate, w2_up, w2_down))


def kernel(x, c, w_ada, b_ada, g_norm1, w1_gate, w1_up, w1_down, g_norm2, w_in, g_sgu, w_spatial,
           b_spatial, g_q, g_k, g_kidx, b_kidx, w_branch_a, w_branch_b, w_out, g_norm3,
           w2_gate, w2_up, w2_down):
    batch, depth = x.shape[0], w_ada.shape[0]
    outs = []
    for b in range(batch):
        xb = x[b]
        for l in range(depth):
            xb = _layer(xb, c[b], w_ada[l], b_ada[l], g_norm1[l], w1_gate[l], w1_up[l], w1_down[l],
                        g_norm2[l], w_in[l], g_sgu[l], w_spatial[l], b_spatial[l], g_q[l], g_k[l],
                        g_kidx[l], b_kidx[l], w_branch_a[l], w_branch_b[l], w_out[l], g_norm3[l],
                        w2_gate[l], w2_up[l], w2_down[l])
        outs.append(xb)
    return jnp.stack(outs)
```

```python
import functools

import numpy as np
import jax
import jax.numpy as jnp
from jax import lax
from jax.experimental import pallas as pl
from jax.experimental.pallas import tpu as pltpu

F32 = jnp.float32
BF16 = jnp.bfloat16
I32 = jnp.int32

CHUNK = 128
SGU_GROUPS = 8
SGU_WIDTH = 1024
ATTN_HEADS = 8
HEAD_DIM = 128
ATTN_WIDTH = ATTN_HEADS * HEAD_DIM
IDX_HEADS = 16
IDX_DIM = 64
TOPK_MAX = 256
N_ADA = 9
EPS = 1e-6
NEG = -1e30

LANES = 128
SUBLANES = 8
VMEM_LIMIT = 56 * 1024 * 1024

FFN_TM = 512
FFN_TF = 512
PROJ_TM = 1024
PROJ_TN = 1024
SGU_TM = 512
MERGE_TM = 256
ATT_TQ = 256
ATT_TK = 1024
IDX_TK = 256
CNT_BLK = 1024
CNT_ACC = 8
BISECT_CAP = 512
AUG = 128
LOG2E = 1.4426950408889634


def _cparams(sem):
    return pltpu.CompilerParams(dimension_semantics=sem, vmem_limit_bytes=VMEM_LIMIT)


def _gelu_tanh(x):
    return 0.5 * x * (1.0 + jnp.tanh(0.7978845608028654 * (x + 0.044715 * (x * x * x))))


def _rms_mod(x, g, sh, sc):
    ms = jnp.mean(x * x, axis=-1, keepdims=True)
    y = (x * lax.rsqrt(ms + EPS)) * g
    return y * (1.0 + sc) + sh


def _ada_kernel(c_ref, w_ref, b_ref, o_ref, sb_ref):
    d = w_ref.shape[0]
    tn = w_ref.shape[1]

    @pl.when(pl.program_id(0) == 0)
    def _():
        cc = c_ref[...]
        sb_ref[...] = jnp.broadcast_to(cc * jax.nn.sigmoid(cc), (d, LANES))

    sb = sb_ref[...]
    for cb in range(tn // LANES):
        cols = slice(cb * LANES, (cb + 1) * LANES)
        prod = (w_ref[:, cols] * sb).reshape(d // SUBLANES, SUBLANES, LANES).sum(axis=0)
        o_ref[:, cols] = prod.sum(axis=0, keepdims=True) + b_ref[:, cols]


def _ada(c_col, w, b):
    d, n = w.shape
    tn = 1024
    return pl.pallas_call(
        _ada_kernel,
        grid=(n // tn,),
        in_specs=[pl.BlockSpec((d, 1), lambda j: (0, 0)),
                  pl.BlockSpec((d, tn), lambda j: (0, j)),
                  pl.BlockSpec((1, tn), lambda j: (0, j))],
        out_specs=pl.BlockSpec((1, tn), lambda j: (0, j)),
        out_shape=jax.ShapeDtypeStruct((1, n), F32),
        scratch_shapes=[pltpu.VMEM((d, LANES), F32)],
        compiler_params=_cparams(("arbitrary",)),
        name="ada",
    )(c_col, w, b)


def _ffn_kernel(x_ref, g_ref, sh_ref, sc_ref, gt_ref, wg_ref, wu_ref, wd_ref, o_ref, n_ref):
    j = pl.program_id(1)

    @pl.when(j == 0)
    def _():
        n_ref[...] = _rms_mod(x_ref[...], g_ref[...], sh_ref[...], sc_ref[...]).astype(BF16)

    n = n_ref[...]
    g = jnp.dot(n, wg_ref[...], preferred_element_type=F32)
    u = jnp.dot(n, wu_ref[...], preferred_element_type=F32)
    h = ((g * jax.nn.sigmoid(g)) * u).astype(BF16)
    d = jnp.dot(h, wd_ref[...], preferred_element_type=F32)

    @pl.when(j == 0)
    def _():
        o_ref[...] = d

    @pl.when(j > 0)
    def _():
        o_ref[...] += d

    @pl.when(j == pl.num_programs(1) - 1)
    def _():
        o_ref[...] = x_ref[...] + (0.5 * gt_ref[...]) * o_ref[...]


def _ffn(x, g, sh, sc, gt, wg, wu, wd):
    s, d = x.shape
    f = wg.shape[1]
    tm = min(FFN_TM, s)
    row = pl.BlockSpec((1, d), lambda i, j: (0, 0))
    return pl.pallas_call(
        _ffn_kernel,
        grid=(s // tm, f // FFN_TF),
        in_specs=[pl.BlockSpec((tm, d), lambda i, j: (i, 0)), row, row, row, row,
                  pl.BlockSpec((d, FFN_TF), lambda i, j: (0, j)),
                  pl.BlockSpec((d, FFN_TF), lambda i, j: (0, j)),
                  pl.BlockSpec((FFN_TF, d), lambda i, j: (j, 0))],
        out_specs=pl.BlockSpec((tm, d), lambda i, j: (i, 0)),
        out_shape=jax.ShapeDtypeStruct((s, d), F32),
        scratch_shapes=[pltpu.VMEM((tm, d), BF16)],
        compiler_params=_cparams(("parallel", "arbitrary")),
        name="ffn",
    )(x, g, sh, sc, gt, wg, wu, wd)


def _normmod_kernel(x_ref, g_ref, sh_ref, sc_ref, o_ref):
    o_ref[...] = _rms_mod(x_ref[...], g_ref[...], sh_ref[...], sc_ref[...]).astype(o_ref.dtype)


def _normmod(x, g, sh, sc):
    s, d = x.shape
    tm = 512
    row = pl.BlockSpec((1, d), lambda i: (0, 0))
    return pl.pallas_call(
        _normmod_kernel,
        grid=(s // tm,),
        in_specs=[pl.BlockSpec((tm, d), lambda i: (i, 0)), row, row, row],
        out_specs=pl.BlockSpec((tm, d), lambda i: (i, 0)),
        out_shape=jax.ShapeDtypeStruct((s, d), BF16),
        compiler_params=_cparams(("parallel",)),
        name="normmod",
    )(x, g, sh, sc)


def _group_rms(z, gain, post_scale):
    outs = []
    for gidx in range(z.shape[1] // LANES):
        cols = slice(gidx * LANES, (gidx + 1) * LANES)
        zg = z[:, cols]
        ms = jnp.mean(zg * zg, axis=-1, keepdims=True)
        y = (zg * lax.rsqrt(ms + EPS)) * gain[:, cols]
        if post_scale != 1.0:
            y = y * post_scale
        outs.append(y)
    return jnp.concatenate(outs, axis=-1)


def _proj_kernel(n_ref, w_ref, *rest, mode, post_scale):
    o_ref = rest[-1]
    z = jnp.dot(n_ref[...], w_ref[...], preferred_element_type=F32)
    if mode == "gelu":
        out = _gelu_tanh(z)
    elif mode == "gelu_gnorm":
        out = _group_rms(_gelu_tanh(z), rest[0][...], post_scale)
    elif mode == "gnorm":
        out = _group_rms(z, rest[0][...], post_scale)
    elif mode == "sigmoid":
        out = jax.nn.sigmoid(z)
    else:
        out = z
    o_ref[...] = out.astype(o_ref.dtype)


def _proj(n, w, mode, gain=None, post_scale=1.0, out_dtype=BF16):
    s, d = n.shape
    nout = w.shape[1]
    in_specs = [pl.BlockSpec((PROJ_TM, d), lambda i, j: (i, 0)),
                pl.BlockSpec((d, PROJ_TN), lambda i, j: (0, j))]
    args = [n, w]
    if gain is not None:
        in_specs.append(pl.BlockSpec((1, PROJ_TN), lambda i, j: (0, j)))
        args.append(gain)
    return pl.pallas_call(
        functools.partial(_proj_kernel, mode=mode, post_scale=post_scale),
        grid=(s // PROJ_TM, nout // PROJ_TN),
        in_specs=in_specs,
        out_specs=pl.BlockSpec((PROJ_TM, PROJ_TN), lambda i, j: (i, j)),
        out_shape=jax.ShapeDtypeStruct((s, nout), out_dtype),
        compiler_params=_cparams(("parallel", "arbitrary")),
        name="proj_" + mode,
    )(*args)


def _split_hi_lo(pos):
    return (pos >> 7) << 7, pos & (LANES - 1)


def _attn_q_kernel(n_ref, w_ref, gain_ref, cpos_ref, o_ref, *, post_scale):
    tm = n_ref.shape[0]
    y = _group_rms(jnp.dot(n_ref[...], w_ref[...], preferred_element_type=F32),
                   gain_ref[...], post_scale)
    col = lax.broadcasted_iota(I32, (tm, AUG), 1)
    t_hi, t_lo = _split_hi_lo(lax.broadcasted_iota(I32, (tm, AUG), 0) & (ATT_TQ - 1))
    tab = jnp.where((col >= 6) & (col < 9), t_hi, jnp.where((col >= 9) & (col < 12), t_lo, 0)).astype(F32)
    hw = HEAD_DIM + AUG
    for h in range(ATTN_HEADS):
        o_ref[:, h * hw:h * hw + HEAD_DIM] = y[:, h * HEAD_DIM:(h + 1) * HEAD_DIM].astype(o_ref.dtype)
        o_ref[:, h * hw + HEAD_DIM:(h + 1) * hw] = (tab + cpos_ref[h:h + 1, :]).astype(o_ref.dtype)


def _attn_kt_kernel(n_ref, w_ref, gain_ref, cneg_ref, o_ref):
    tm = n_ref.shape[0]
    y = _group_rms(jnp.dot(n_ref[...], w_ref[...], preferred_element_type=F32), gain_ref[...], 1.0)
    yt = jnp.transpose(y)
    row = lax.broadcasted_iota(I32, (AUG, tm), 0)
    pos = pl.program_id(0) * tm + lax.broadcasted_iota(I32, (AUG, tm), 1)
    u_hi, u_lo = _split_hi_lo(pos & (ATT_TK - 1))
    tab = jnp.where(row < 3, u_hi, jnp.where(row < 6, u_lo, 0)).astype(F32)
    hw = HEAD_DIM + AUG
    for h in range(ATTN_HEADS):
        o_ref[h * hw:h * hw + HEAD_DIM, :] = yt[h * HEAD_DIM:(h + 1) * HEAD_DIM, :].astype(o_ref.dtype)
        o_ref[h * hw + HEAD_DIM:(h + 1) * hw, :] = (tab + cneg_ref[h]).astype(o_ref.dtype)


def _idx_q_kernel(n_ref, w_ref, o_ref):
    zt = jnp.transpose(jnp.dot(n_ref[...], w_ref[...], preferred_element_type=F32))
    for qt in range(n_ref.shape[0] // ATT_TQ):
        for h in range(IDX_HEADS):
            o_ref[qt, :, h * ATT_TQ:(h + 1) * ATT_TQ] = zt[
                h * IDX_DIM:(h + 1) * IDX_DIM, qt * ATT_TQ:(qt + 1) * ATT_TQ].astype(o_ref.dtype)


def _attn_proj(kernel, n, w, extra, extra_specs, out_shape, out_spec, name):
    s, d = n.shape
    return pl.pallas_call(
        kernel,
        grid=(s // PROJ_TM,),
        in_specs=[pl.BlockSpec((PROJ_TM, d), lambda i: (i, 0)),
                  pl.BlockSpec(w.shape, lambda i: (0, 0))] + extra_specs,
        out_specs=out_spec,
        out_shape=out_shape,
        compiler_params=_cparams(("parallel",)),
        name=name,
    )(n, w, *extra)


def _kidx_kernel(n_ref, w_ref, g_ref, b_ref, ki_ref, wit_ref):
    z = jnp.dot(n_ref[...], w_ref[...], preferred_element_type=F32)
    ki = z[:, :IDX_DIM]
    mu = jnp.mean(ki, axis=-1, keepdims=True)
    var = jnp.mean(jnp.square(ki - mu), axis=-1, keepdims=True)
    y = (ki - mu) * lax.rsqrt(var + EPS)
    ki_ref[...] = (y * g_ref[...] + b_ref[...]).astype(ki_ref.dtype)
    wit_ref[...] = jnp.transpose(z)[IDX_DIM:IDX_DIM + IDX_HEADS, :] * (IDX_HEADS ** -0.5 * IDX_DIM ** -0.5)


def _kidx(n, w, g, b):
    s, d = n.shape
    tm = 512
    return pl.pallas_call(
        _kidx_kernel,
        grid=(s // tm,),
        in_specs=[pl.BlockSpec((tm, d), lambda i: (i, 0)),
                  pl.BlockSpec((d, LANES), lambda i: (0, 0)),
                  pl.BlockSpec((1, IDX_DIM), lambda i: (0, 0)),
                  pl.BlockSpec((1, IDX_DIM), lambda i: (0, 0))],
        out_specs=[pl.BlockSpec((tm, IDX_DIM), lambda i: (i, 0)),
                   pl.BlockSpec((IDX_HEADS, tm), lambda i: (0, i))],
        out_shape=[jax.ShapeDtypeStruct((s, IDX_DIM), BF16),
                   jax.ShapeDtypeStruct((IDX_HEADS, s), F32)],
        compiler_params=_cparams(("parallel",)),
        name="proj_kidx",
    )(n, w, g, b)


def _sgu_kernel(u_ref, v_ref, ws_ref, bt_ref, o_ref):
    tm = u_ref.shape[0]
    r = lax.broadcasted_iota(I32, (CHUNK, CHUNK), 0)
    c = lax.broadcasted_iota(I32, (CHUNK, CHUNK), 1)
    causal = c <= r
    for g in range(SGU_GROUPS):
        cols = slice(g * LANES, (g + 1) * LANES)
        w = jnp.where(causal, ws_ref[g], 0.0).astype(BF16)
        bcol = bt_ref[:, g:g + 1]
        for ch in range(tm // CHUNK):
            rows = slice(ch * CHUNK, (ch + 1) * CHUNK)
            sv = jnp.dot(w, v_ref[rows, cols], preferred_element_type=F32) + bcol
            o_ref[rows, cols] = (u_ref[rows, cols].astype(F32) * sv).astype(o_ref.dtype)


def _sgu(u, v, ws, bt):
    s, wdt = u.shape
    return pl.pallas_call(
        _sgu_kernel,
        grid=(s // SGU_TM,),
        in_specs=[pl.BlockSpec((SGU_TM, wdt), lambda i: (i, 0)),
                  pl.BlockSpec((SGU_TM, wdt), lambda i: (i, 0)),
                  pl.BlockSpec((SGU_GROUPS, CHUNK, CHUNK), lambda i: (0, 0, 0)),
                  pl.BlockSpec((CHUNK, SGU_GROUPS), lambda i: (0, 0))],
        out_specs=pl.BlockSpec((SGU_TM, wdt), lambda i: (i, 0)),
        out_shape=jax.ShapeDtypeStruct((s, wdt), BF16),
        compiler_params=_cparams(("parallel",)),
        name="sgu",
    )(u, v, ws, bt)


def _tile_lanes(x, n):
    return x if n == 1 else jnp.concatenate([x] * n, axis=1)


def _sublane_all(x, op):
    for shift in (4, 2, 1):
        x = op(x, pltpu.roll(x, shift, 0))
    return x


def _count_keys(keys_ref, n_blocks, pred):
    tq = keys_ref.shape[1]

    def body(it, acc):
        off = pl.multiple_of(it * CNT_BLK, CNT_BLK)
        blk = keys_ref[pl.ds(off, CNT_BLK), :].reshape(CNT_BLK // SUBLANES, SUBLANES, tq)
        ind = jnp.where(pred(blk, off), 1, 0)
        return acc + ind.reshape(-1, CNT_ACC, SUBLANES, tq).sum(axis=0)

    acc = lax.fori_loop(0, n_blocks, body, jnp.zeros((CNT_ACC, SUBLANES, tq), I32))
    tot = jnp.sum(acc.sum(axis=0).astype(F32), axis=0, keepdims=True)
    return jnp.broadcast_to(tot, (SUBLANES, tq)).astype(I32)


def _attn_kernel(itab_ref, jtab_ref, qit_ref, wit_ref, ki_ref, q_ref, kt_ref, v_ref, o_ref,
                 keys_ref, r_ref, gmax_ref, thr_ref, m_ref, acc_ref, *, alibi_c, topk, idx_bits):
    i = itab_ref[pl.program_id(0)]
    j = jtab_ref[pl.program_id(0)]
    tq, tk = ATT_TQ, ATT_TK
    hw = HEAD_DIM + AUG
    q0 = i * tq
    n_kv = (q0 + tq + tk - 1) // tk
    n_blk = (q0 + tq + CNT_BLK - 1) // CNT_BLK

    @pl.when(j == 0)
    def _index_and_threshold():
        m_ref[...] = jnp.full(m_ref.shape, NEG, F32)
        acc_ref[...] = jnp.zeros(acc_ref.shape, F32)

        def chunk_body(c, carry):
            off = pl.multiple_of(c * IDX_TK, IDX_TK)
            r_ref[...] = jnp.dot(ki_ref[pl.ds(off, IDX_TK), :], qit_ref[0],
                                 preferred_element_type=F32)
            s_idx = off + lax.broadcasted_iota(I32, (IDX_TK, LANES), 0)
            for g in range(tq // LANES):
                lanes = slice(g * LANES, (g + 1) * LANES)
                t_idx = q0 + g * LANES + lax.broadcasted_iota(I32, (IDX_TK, LANES), 1)
                acc = jnp.zeros((IDX_TK, LANES), F32)
                for h in range(IDX_HEADS):
                    rr = r_ref[:, h * tq + g * LANES:h * tq + (g + 1) * LANES]
                    acc = acc + jnp.maximum(rr, 0.0) * wit_ref[h:h + 1, lanes]
                score = jnp.where(s_idx <= t_idx, acc, -jnp.inf)
                keys_ref[pl.ds(off, IDX_TK), lanes] = score
                gmax_ref[:, lanes] = jnp.maximum(gmax_ref[:, lanes], score)
            return carry

        gmax_ref[...] = jnp.full(gmax_ref.shape, -jnp.inf, F32)
        lax.fori_loop(0, (q0 + tq) // IDX_TK, chunk_body, 0)

        def fill_body(f, carry):
            off = pl.multiple_of(q0 + tq + f * tq, tq)
            keys_ref[pl.ds(off, tq), :] = jnp.full((tq, tq), -jnp.inf, F32)
            return carry

        lax.fori_loop(0, (n_blk * CNT_BLK - (q0 + tq)) // tq, fill_body, 0)

        t_row = q0 + lax.broadcasted_iota(I32, (SUBLANES, tq), 1)
        kq = jnp.minimum(topk, t_row + 1)

        g = gmax_ref[...].reshape(IDX_TK // SUBLANES, SUBLANES, tq)
        hi0 = _sublane_all(g.max(axis=0), jnp.maximum)
        lo0 = _sublane_all(jnp.where(g == -jnp.inf, jnp.inf, g).min(axis=0), jnp.minimum)

        def bis_cond(st):
            it, _, _, _, _, done = st
            return jnp.logical_and(it < BISECT_CAP, done == 0)

        def bis_body(st):
            it, lo, hi, cnt, fin, _ = st
            mid = 0.5 * lo + 0.5 * hi
            stuck = (mid <= lo) | (mid >= hi)
            cand = jnp.where(stuck, hi, mid)
            tot = _count_keys(keys_ref, n_blk, lambda blk, off: blk >= cand[None])
            take = (tot >= kq) & (fin == 0)
            lo = jnp.where(take, cand, lo)
            cnt = jnp.where(take, tot, cnt)
            hi = jnp.where(take | (fin != 0), hi, cand)
            fin = jnp.where(stuck, 1, fin)
            settled = (cnt == kq) | (fin != 0)
            done = (jnp.min(jnp.where(settled, 1.0, 0.0)) > 0.5).astype(I32)
            return it + 1, lo, hi, cnt, fin, done

        empty = lo0 > hi0
        _, thr, _, cnt, _, _ = lax.while_loop(
            bis_cond, bis_body,
            (jnp.int32(0), lo0, hi0, jnp.where(empty, 0, -1), jnp.where(empty, 1, 0), jnp.int32(0)))
        thr_ref[...] = thr
        cnt = lax.cond(jnp.min(cnt.astype(F32)) < 0.0,
                       lambda: jnp.where(cnt < 0, _count_keys(
                           keys_ref, n_blk, lambda blk, off: blk >= thr[None]), cnt),
                       lambda: cnt)

        @pl.when(jnp.max(jnp.where(cnt > kq, 1.0, 0.0)) > 0.5)
        def _break_ties():
            def row_idx(off):
                return off + lax.broadcasted_iota(I32, (CNT_BLK, tq), 0).reshape(
                    CNT_BLK // SUBLANES, SUBLANES, tq)

            above = _count_keys(keys_ref, n_blk, lambda blk, off: blk > thr[None])
            need = kq - above

            def cut_body(b, p):
                c = p + lax.shift_left(jnp.int32(1), idx_bits - 1 - b)
                below = _count_keys(
                    keys_ref, n_blk, lambda blk, off: (blk == thr[None]) & (row_idx(off) < c[None]))
                return jnp.where(below < need, c, p)

            p = lax.fori_loop(0, idx_bits, cut_body, jnp.zeros((SUBLANES, tq), I32))

            def drop_body(it, carry):
                off = pl.multiple_of(it * CNT_BLK, CNT_BLK)
                blk = keys_ref[pl.ds(off, CNT_BLK), :].reshape(CNT_BLK // SUBLANES, SUBLANES, tq)
                new = jnp.where((blk == thr[None]) & (row_idx(off) > p[None]), -jnp.inf, blk)
                keys_ref[pl.ds(off, CNT_BLK), :] = new.reshape(CNT_BLK, tq)
                return carry

            lax.fori_loop(0, n_blk, drop_body, 0)

    k0 = pl.multiple_of(j * tk, tk)
    nl = tk // LANES
    sel_t = keys_ref[pl.ds(k0, tk), :] >= thr_ref[0:1, :]
    bias = jnp.transpose(jnp.where(sel_t, 0.0, NEG))
    tile_dist = jnp.full((tq, LANES), k0 - q0, I32).astype(F32)
    ones_col = (lax.broadcasted_iota(I32, (tk, LANES), 1) == 0).astype(BF16)

    def logits(h):
        s = jnp.dot(q_ref[:, h * hw:(h + 1) * hw], kt_ref[h * hw:(h + 1) * hw, :],
                    preferred_element_type=F32) + bias
        return s, jnp.max(s, axis=-1, keepdims=True)

    def probs(h, s, s_max):
        shift = tile_dist * alibi_c[h]
        m_prev = m_ref[h]
        m_new = jnp.maximum(m_prev, s_max + shift)
        m_ref[h] = m_new
        p = jnp.exp2(s - _tile_lanes(m_new - shift, nl)).astype(BF16)
        return p, jnp.exp2(m_prev - m_new)

    def accumulate(h, p, alpha):
        v_aug = jnp.concatenate([v_ref[:, h * HEAD_DIM:(h + 1) * HEAD_DIM], ones_col], axis=1)
        acc_ref[h] = _tile_lanes(alpha, 2) * acc_ref[h] + jnp.dot(
            p, v_aug, preferred_element_type=F32)

    st_a, st_b = {}, {}
    for step in range(ATTN_HEADS + 2):
        if 0 <= step - 2 < ATTN_HEADS:
            accumulate(step - 2, *st_b.pop(step - 2))
        if step < ATTN_HEADS:
            st_a[step] = logits(step)
        if 0 <= step - 1 < ATTN_HEADS:
            st_b[step - 1] = probs(step - 1, *st_a.pop(step - 1))

    @pl.when(j == n_kv - 1)
    def _finish():
        for h in range(ATTN_HEADS):
            acc = acc_ref[h]
            o_ref[:, h * HEAD_DIM:(h + 1) * HEAD_DIM] = (
                acc[:, :HEAD_DIM] / acc[:, HEAD_DIM:HEAD_DIM + 1]).astype(o_ref.dtype)


def _alibi_pieces():
    out = []
    for h in range(ATTN_HEADS):
        c = np.float32(LOG2E * 2.0 ** (-8.0 * (h + 1) / ATTN_HEADS))
        c1 = np.float32(c.astype(BF16))
        c2 = np.float32(np.float32(c - c1).astype(BF16))
        c3 = np.float32(np.float32(c - c1 - c2).astype(BF16))
        out.append((float(c1), float(c2), float(c3)))
    return out


def _alibi_tables():
    pieces = np.asarray(_alibi_pieces(), np.float32)
    cpos = np.zeros((ATTN_HEADS, AUG), np.float32)
    cneg = np.zeros((ATTN_HEADS, AUG, 1), np.float32)
    cpos[:, 0:3] = pieces
    cpos[:, 3:6] = pieces
    cneg[:, 6:9, 0] = -pieces
    cneg[:, 9:12, 0] = -pieces
    alibi_c = tuple(float(p[0] + p[1] + p[2]) for p in pieces)
    return jnp.asarray(cpos), jnp.asarray(cneg), alibi_c


def _attention(qit, wit, ki, q_aug, kt_aug, v, alibi_c):
    s = q_aug.shape[0]
    tq, tk = ATT_TQ, ATT_TK
    hw = HEAD_DIM + AUG
    assert s % CNT_BLK == 0 and CNT_BLK % tk == 0 and CNT_BLK % tq == 0 and tq % LANES == 0
    topk = min(TOPK_MAX, s // 4)
    idx_bits = max(1, (s - 1).bit_length())
    assert IDX_TK >= topk and tq % IDX_TK == 0

    steps = [(i, j) for i in range(s // tq) for j in range(((i + 1) * tq + tk - 1) // tk)]
    itab = jnp.asarray(np.array([p[0] for p in steps], np.int32))
    jtab = jnp.asarray(np.array([p[1] for p in steps], np.int32))

    grid_spec = pltpu.PrefetchScalarGridSpec(
        num_scalar_prefetch=2,
        grid=(len(steps),),
        in_specs=[pl.BlockSpec((1, IDX_DIM, IDX_HEADS * tq), lambda t, it, jt: (it[t], 0, 0)),
                  pl.BlockSpec((IDX_HEADS, tq), lambda t, it, jt: (0, it[t])),
                  pl.BlockSpec((s, IDX_DIM), lambda t, it, jt: (0, 0)),
                  pl.BlockSpec((tq, ATTN_HEADS * hw), lambda t, it, jt: (it[t], 0)),
                  pl.BlockSpec((ATTN_HEADS * hw, tk), lambda t, it, jt: (0, jt[t])),
                  pl.BlockSpec((tk, ATTN_WIDTH), lambda t, it, jt: (jt[t], 0))],
        out_specs=pl.BlockSpec((tq, ATTN_WIDTH), lambda t, it, jt: (it[t], 0)),
        scratch_shapes=[pltpu.VMEM((s, tq), F32),
                        pltpu.VMEM((IDX_TK, IDX_HEADS * tq), F32),
                        pltpu.VMEM((IDX_TK, tq), F32),
                        pltpu.VMEM((SUBLANES, tq), F32),
                        pltpu.VMEM((ATTN_HEADS, tq, LANES), F32),
                        pltpu.VMEM((ATTN_HEADS, tq, 2 * HEAD_DIM), F32)])
    return pl.pallas_call(
        functools.partial(_attn_kernel, alibi_c=alibi_c, topk=topk, idx_bits=idx_bits),
        grid_spec=grid_spec,
        out_shape=jax.ShapeDtypeStruct((s, ATTN_WIDTH), BF16),
        compiler_params=_cparams(("arbitrary",)),
        name="dsa_attention",
    )(itab, jtab, qit, wit, ki, q_aug, kt_aug, v)


def _merge_kernel(ya_ref, yb_ref, sg_a_ref, sg_b_ref, x_ref, gt_ref, wa_ref, wb_ref, wo_ref, o_ref):
    a = jnp.dot(ya_ref[...], wa_ref[...], preferred_element_type=F32)
    b = jnp.dot(yb_ref[...], wb_ref[...], preferred_element_type=F32)
    merged = sg_a_ref[...].astype(F32) * a + sg_b_ref[...].astype(F32) * b
    o_ref[...] = x_ref[...] + gt_ref[...] * jnp.dot(
        merged.astype(BF16), wo_ref[...], preferred_element_type=F32)


def _merge(ya, yb, sg, x, gt, wa, wb, wo):
    s, d = x.shape
    tm = MERGE_TM
    wdt = ya.shape[1]
    const = lambda i: (0, 0)
    return pl.pallas_call(
        _merge_kernel,
        grid=(s // tm,),
        in_specs=[pl.BlockSpec((tm, wdt), lambda i: (i, 0)),
                  pl.BlockSpec((tm, wdt), lambda i: (i, 0)),
                  pl.BlockSpec((tm, d), lambda i: (i, 0)),
                  pl.BlockSpec((tm, d), lambda i: (i, 1)),
                  pl.BlockSpec((tm, d), lambda i: (i, 0)),
                  pl.BlockSpec((1, d), const),
                  pl.BlockSpec((wdt, d), const),
                  pl.BlockSpec((wdt, d), const),
                  pl.BlockSpec((d, d), const)],
        out_specs=pl.BlockSpec((tm, d), lambda i: (i, 0)),
        out_shape=jax.ShapeDtypeStruct((s, d), F32),
        compiler_params=_cparams(("parallel",)),
        name="merge",
    )(ya, yb, sg, sg, x, gt, wa, wb, wo)


def _pad_cols(w, n):
    return jnp.pad(w, ((0, 0), (0, n - w.shape[1])))


def _layer(x, c, w_ada, b_ada, g_norm1, w1_gate, w1_up, w1_down, g_norm2, w_in, g_sgu, w_spatial,
           b_spatial, g_q, g_k, g_kidx, b_kidx, w_branch_a, w_branch_b, w_out, g_norm3,
           w2_gate, w2_up, w2_down):
    s, d = x.shape
    dff = w1_gate.shape[1]
    dff_p = -(-dff // FFN_TF) * FFN_TF

    ada = _ada(c.reshape(d, 1), w_ada, b_ada.reshape(1, -1))
    sh1, sc1, gt1, sh2, sc2, gt2, sh3, sc3, gt3 = [ada[:, k * d:(k + 1) * d] for k in range(N_ADA)]

    def ffn_weights(wg, wu, wd):
        return (_pad_cols(wg.astype(BF16), dff_p), _pad_cols(wu.astype(BF16), dff_p),
                jnp.pad(wd.astype(BF16), ((0, dff_p - dff), (0, 0))))

    x1 = _ffn(x, g_norm1.reshape(1, d), sh1, sc1, gt1, *ffn_weights(w1_gate, w1_up, w1_down))

    n2 = _normmod(x1, g_norm2.reshape(1, d), sh2, sc2)
    wi_b = w_in.astype(BF16)
    o_u, o_v, o_q = 0, SGU_WIDTH, 2 * SGU_WIDTH
    o_k, o_vv = o_q + ATTN_WIDTH, o_q + 2 * ATTN_WIDTH
    o_qi = o_vv + ATTN_WIDTH
    o_ki = o_qi + IDX_HEADS * IDX_DIM
    o_g = o_ki + IDX_DIM + IDX_HEADS

    ug = _proj(n2, wi_b[:, o_u:o_v], "gelu")
    vn = _proj(n2, wi_b[:, o_v:o_q], "gelu_gnorm", gain=g_sgu.reshape(1, SGU_WIDTH))
    vv = _proj(n2, wi_b[:, o_vv:o_qi], "none")
    sg = _proj(n2, wi_b[:, o_g:], "sigmoid")
    ki, wit = _kidx(n2, _pad_cols(wi_b[:, o_ki:o_g], LANES),
                    g_kidx.reshape(1, IDX_DIM), b_kidx.reshape(1, IDX_DIM))

    cpos, cneg, alibi_c = _alibi_tables()
    hw = HEAD_DIM + AUG
    row = lambda i: (0, 0)
    q_aug = _attn_proj(
        functools.partial(_attn_q_kernel, post_scale=HEAD_DIM ** -0.5 * LOG2E),
        n2, wi_b[:, o_q:o_k], [jnp.tile(g_q, ATTN_HEADS).reshape(1, -1), cpos],
        [pl.BlockSpec((1, ATTN_WIDTH), row), pl.BlockSpec((ATTN_HEADS, AUG), row)],
        jax.ShapeDtypeStruct((s, ATTN_HEADS * hw), BF16),
        pl.BlockSpec((PROJ_TM, ATTN_HEADS * hw), lambda i: (i, 0)), "proj_attn_q")
    kt_aug = _attn_proj(
        _attn_kt_kernel, n2, wi_b[:, o_k:o_vv], [jnp.tile(g_k, ATTN_HEADS).reshape(1, -1), cneg],
        [pl.BlockSpec((1, ATTN_WIDTH), row), pl.BlockSpec((ATTN_HEADS, AUG, 1), lambda i: (0, 0, 0))],
        jax.ShapeDtypeStruct((ATTN_HEADS * hw, s), BF16),
        pl.BlockSpec((ATTN_HEADS * hw, PROJ_TM), lambda i: (0, i)), "proj_attn_kt")
    qit = _attn_proj(
        _idx_q_kernel, n2, wi_b[:, o_qi:o_ki], [], [],
        jax.ShapeDtypeStruct((s // ATT_TQ, IDX_DIM, IDX_HEADS * ATT_TQ), BF16),
        pl.BlockSpec((PROJ_TM // ATT_TQ, IDX_DIM, IDX_HEADS * ATT_TQ), lambda i: (i, 0, 0)), "proj_idx_q")

    y_a = _sgu(ug, vn, w_spatial, jnp.transpose(b_spatial))
    y_b = _attention(qit, wit, ki, q_aug, kt_aug, vv, alibi_c)

    x2 = _merge(y_a, y_b, sg, x1, gt2, w_branch_a.astype(BF16), w_branch_b.astype(BF16),
                w_out.astype(BF16))

    return _ffn(x2, g_norm3.reshape(1, d), sh3, sc3, gt3, *ffn_weights(w2_gate, w2_up, w2_down))


def kernel(x, c, w_ada, b_ada, g_norm1, w1_gate, w1_up, w1_down, g_norm2, w_in, g_sgu, w_spatial,
           b_spatial, g_q, g_k, g_kidx, b_kidx, w_branch_a, w_branch_b, w_out, g_norm3,
           w2_gate, w2_up, w2_down):
    batch, depth = x.shape[0], w_ada.shape[0]
    outs = []
    for b in range(batch):
        xb = x[b]
        for l in range(depth):
            xb = _layer(xb, c[b], w_ada[l], b_ada[l], g_norm1[l], w1_gate[l], w1_up[l], w1_down[l],
                        g_norm2[l], w_in[l], g_sgu[l], w_spatial[l], b_spatial[l], g_q[l], g_k[l],
                        g_kidx[l], b_kidx[l], w_branch_a[l], w_branch_b[l], w_out[l], g_norm3[l],
                        w2_gate[l], w2_up[l], w2_down[l])
        outs.append(xb)
    return jnp.stack(outs)
```

```python
import functools

import numpy as np
import jax
import jax.numpy as jnp
from jax import lax
from jax.experimental import pallas as pl
from jax.experimental.pallas import tpu as pltpu

F32 = jnp.float32
BF16 = jnp.bfloat16
I32 = jnp.int32

CHUNK = 128
SGU_GROUPS = 8
SGU_WIDTH = 1024
ATTN_HEADS = 8
HEAD_DIM = 128
ATTN_WIDTH = ATTN_HEADS * HEAD_DIM
IDX_HEADS = 16
IDX_DIM = 64
TOPK_MAX = 256
N_ADA = 9
EPS = 1e-6
NEG = -1e30

LANES = 128
SUBLANES = 8
VMEM_LIMIT = 56 * 1024 * 1024

FFN_TM = 512
FFN_TF = 512
PROJ_TM = 1024
PROJ_TN = 1024
SGU_TM = 512
MERGE_TM = 256
ATT_TQ = 256
ATT_TK = 1024
IDX_TK = 256
IDX_SUB = 2
CNT_BLK = 1024
CNT_ACC = 8
BISECT_CAP = 512
AUG = 128
LOG2E = 1.4426950408889634


def _cparams(sem):
    return pltpu.CompilerParams(dimension_semantics=sem, vmem_limit_bytes=VMEM_LIMIT)


def _gelu_tanh(x):
    return 0.5 * x * (1.0 + jnp.tanh(0.7978845608028654 * (x + 0.044715 * (x * x * x))))


def _rms_mod(x, g, sh, sc):
    ms = jnp.mean(x * x, axis=-1, keepdims=True)
    y = (x * lax.rsqrt(ms + EPS)) * g
    return y * (1.0 + sc) + sh


def _ada_kernel(c_ref, w_ref, b_ref, o_ref, sb_ref):
    d = w_ref.shape[0]
    tn = w_ref.shape[1]

    @pl.when(pl.program_id(0) == 0)
    def _():
        cc = c_ref[...]
        sb_ref[...] = jnp.broadcast_to(cc * jax.nn.sigmoid(cc), (d, LANES))

    sb = sb_ref[...]
    for cb in range(tn // LANES):
        cols = slice(cb * LANES, (cb + 1) * LANES)
        prod = (w_ref[:, cols] * sb).reshape(d // SUBLANES, SUBLANES, LANES).sum(axis=0)
        o_ref[:, cols] = prod.sum(axis=0, keepdims=True) + b_ref[:, cols]


def _ada(c_col, w, b):
    d, n = w.shape
    tn = 1024
    return pl.pallas_call(
        _ada_kernel,
        grid=(n // tn,),
        in_specs=[pl.BlockSpec((d, 1), lambda j: (0, 0)),
                  pl.BlockSpec((d, tn), lambda j: (0, j)),
                  pl.BlockSpec((1, tn), lambda j: (0, j))],
        out_specs=pl.BlockSpec((1, tn), lambda j: (0, j)),
        out_shape=jax.ShapeDtypeStruct((1, n), F32),
        scratch_shapes=[pltpu.VMEM((d, LANES), F32)],
        compiler_params=_cparams(("arbitrary",)),
        name="ada",
    )(c_col, w, b)


def _ffn_kernel(x_ref, g_ref, sh_ref, sc_ref, gt_ref, wg_ref, wu_ref, wd_ref, o_ref, n_ref):
    j = pl.program_id(1)

    @pl.when(j == 0)
    def _():
        n_ref[...] = _rms_mod(x_ref[...], g_ref[...], sh_ref[...], sc_ref[...]).astype(BF16)

    n = n_ref[...]
    g = jnp.dot(n, wg_ref[...], preferred_element_type=F32)
    u = jnp.dot(n, wu_ref[...], preferred_element_type=F32)
    h = ((g * jax.nn.sigmoid(g)) * u).astype(BF16)
    d = jnp.dot(h, wd_ref[...], preferred_element_type=F32)

    @pl.when(j == 0)
    def _():
        o_ref[...] = d

    @pl.when(j > 0)
    def _():
        o_ref[...] += d

    @pl.when(j == pl.num_programs(1) - 1)
    def _():
        o_ref[...] = x_ref[...] + (0.5 * gt_ref[...]) * o_ref[...]


def _ffn(x, g, sh, sc, gt, wg, wu, wd):
    s, d = x.shape
    f = wg.shape[1]
    tm = min(FFN_TM, s)
    row = pl.BlockSpec((1, d), lambda i, j: (0, 0))
    return pl.pallas_call(
        _ffn_kernel,
        grid=(s // tm, f // FFN_TF),
        in_specs=[pl.BlockSpec((tm, d), lambda i, j: (i, 0)), row, row, row, row,
                  pl.BlockSpec((d, FFN_TF), lambda i, j: (0, j)),
                  pl.BlockSpec((d, FFN_TF), lambda i, j: (0, j)),
                  pl.BlockSpec((FFN_TF, d), lambda i, j: (j, 0))],
        out_specs=pl.BlockSpec((tm, d), lambda i, j: (i, 0)),
        out_shape=jax.ShapeDtypeStruct((s, d), F32),
        scratch_shapes=[pltpu.VMEM((tm, d), BF16)],
        compiler_params=_cparams(("parallel", "arbitrary")),
        name="ffn",
    )(x, g, sh, sc, gt, wg, wu, wd)


def _normmod_kernel(x_ref, g_ref, sh_ref, sc_ref, o_ref):
    o_ref[...] = _rms_mod(x_ref[...], g_ref[...], sh_ref[...], sc_ref[...]).astype(o_ref.dtype)


def _normmod(x, g, sh, sc):
    s, d = x.shape
    tm = 512
    row = pl.BlockSpec((1, d), lambda i: (0, 0))
    return pl.pallas_call(
        _normmod_kernel,
        grid=(s // tm,),
        in_specs=[pl.BlockSpec((tm, d), lambda i: (i, 0)), row, row, row],
        out_specs=pl.BlockSpec((tm, d), lambda i: (i, 0)),
        out_shape=jax.ShapeDtypeStruct((s, d), BF16),
        compiler_params=_cparams(("parallel",)),
        name="normmod",
    )(x, g, sh, sc)


def _group_rms(z, gain, post_scale):
    outs = []
    for gidx in range(z.shape[1] // LANES):
        cols = slice(gidx * LANES, (gidx + 1) * LANES)
        zg = z[:, cols]
        ms = jnp.mean(zg * zg, axis=-1, keepdims=True)
        y = (zg * lax.rsqrt(ms + EPS)) * gain[:, cols]
        if post_scale != 1.0:
            y = y * post_scale
        outs.append(y)
    return jnp.concatenate(outs, axis=-1)


def _proj_kernel(n_ref, w_ref, *rest, mode, post_scale):
    o_ref = rest[-1]
    z = jnp.dot(n_ref[...], w_ref[...], preferred_element_type=F32)
    if mode == "gelu":
        out = _gelu_tanh(z)
    elif mode == "gelu_gnorm":
        out = _group_rms(_gelu_tanh(z), rest[0][...], post_scale)
    elif mode == "gnorm":
        out = _group_rms(z, rest[0][...], post_scale)
    elif mode == "sigmoid":
        out = jax.nn.sigmoid(z)
    else:
        out = z
    o_ref[...] = out.astype(o_ref.dtype)


def _proj(n, w, mode, gain=None, post_scale=1.0, out_dtype=BF16):
    s, d = n.shape
    nout = w.shape[1]
    in_specs = [pl.BlockSpec((PROJ_TM, d), lambda i, j: (i, 0)),
                pl.BlockSpec((d, PROJ_TN), lambda i, j: (0, j))]
    args = [n, w]
    if gain is not None:
        in_specs.append(pl.BlockSpec((1, PROJ_TN), lambda i, j: (0, j)))
        args.append(gain)
    return pl.pallas_call(
        functools.partial(_proj_kernel, mode=mode, post_scale=post_scale),
        grid=(s // PROJ_TM, nout // PROJ_TN),
        in_specs=in_specs,
        out_specs=pl.BlockSpec((PROJ_TM, PROJ_TN), lambda i, j: (i, j)),
        out_shape=jax.ShapeDtypeStruct((s, nout), out_dtype),
        compiler_params=_cparams(("parallel", "arbitrary")),
        name="proj_" + mode,
    )(*args)


def _split_hi_lo(pos):
    return (pos >> 7) << 7, pos & (LANES - 1)


def _attn_q_kernel(n_ref, w_ref, gain_ref, cpos_ref, o_ref, *, post_scale):
    tm = n_ref.shape[0]
    y = _group_rms(jnp.dot(n_ref[...], w_ref[...], preferred_element_type=F32),
                   gain_ref[...], post_scale)
    col = lax.broadcasted_iota(I32, (tm, AUG), 1)
    t_hi, t_lo = _split_hi_lo(lax.broadcasted_iota(I32, (tm, AUG), 0) & (ATT_TQ - 1))
    tab = jnp.where((col >= 6) & (col < 9), t_hi, jnp.where((col >= 9) & (col < 12), t_lo, 0)).astype(F32)
    hw = HEAD_DIM + AUG
    for h in range(ATTN_HEADS):
        o_ref[:, h * hw:h * hw + HEAD_DIM] = y[:, h * HEAD_DIM:(h + 1) * HEAD_DIM].astype(o_ref.dtype)
        o_ref[:, h * hw + HEAD_DIM:(h + 1) * hw] = (tab + cpos_ref[h:h + 1, :]).astype(o_ref.dtype)


def _attn_kt_kernel(n_ref, w_ref, gain_ref, cneg_ref, o_ref):
    tm = n_ref.shape[0]
    y = _group_rms(jnp.dot(n_ref[...], w_ref[...], preferred_element_type=F32), gain_ref[...], 1.0)
    yt = jnp.transpose(y)
    row = lax.broadcasted_iota(I32, (AUG, tm), 0)
    pos = pl.program_id(0) * tm + lax.broadcasted_iota(I32, (AUG, tm), 1)
    u_hi, u_lo = _split_hi_lo(pos & (ATT_TK - 1))
    tab = jnp.where(row < 3, u_hi, jnp.where(row < 6, u_lo, 0)).astype(F32)
    hw = HEAD_DIM + AUG
    for h in range(ATTN_HEADS):
        o_ref[h * hw:h * hw + HEAD_DIM, :] = yt[h * HEAD_DIM:(h + 1) * HEAD_DIM, :].astype(o_ref.dtype)
        o_ref[h * hw + HEAD_DIM:(h + 1) * hw, :] = (tab + cneg_ref[h]).astype(o_ref.dtype)


def _idx_q_kernel(n_ref, w_ref, o_ref):
    zt = jnp.transpose(jnp.dot(n_ref[...], w_ref[...], preferred_element_type=F32))
    for qt in range(n_ref.shape[0] // ATT_TQ):
        for h in range(IDX_HEADS):
            o_ref[qt, :, h * ATT_TQ:(h + 1) * ATT_TQ] = zt[
                h * IDX_DIM:(h + 1) * IDX_DIM, qt * ATT_TQ:(qt + 1) * ATT_TQ].astype(o_ref.dtype)


def _attn_proj(kernel, n, w, extra, extra_specs, out_shape, out_spec, name):
    s, d = n.shape
    return pl.pallas_call(
        kernel,
        grid=(s // PROJ_TM,),
        in_specs=[pl.BlockSpec((PROJ_TM, d), lambda i: (i, 0)),
                  pl.BlockSpec(w.shape, lambda i: (0, 0))] + extra_specs,
        out_specs=out_spec,
        out_shape=out_shape,
        compiler_params=_cparams(("parallel",)),
        name=name,
    )(n, w, *extra)


def _kidx_kernel(n_ref, w_ref, g_ref, b_ref, ki_ref, wit_ref):
    z = jnp.dot(n_ref[...], w_ref[...], preferred_element_type=F32)
    ki = z[:, :IDX_DIM]
    mu = jnp.mean(ki, axis=-1, keepdims=True)
    var = jnp.mean(jnp.square(ki - mu), axis=-1, keepdims=True)
    y = (ki - mu) * lax.rsqrt(var + EPS)
    ki_ref[...] = (y * g_ref[...] + b_ref[...]).astype(ki_ref.dtype)
    wit_ref[...] = jnp.transpose(z)[IDX_DIM:IDX_DIM + IDX_HEADS, :] * (IDX_HEADS ** -0.5 * IDX_DIM ** -0.5)


def _kidx(n, w, g, b):
    s, d = n.shape
    tm = 512
    return pl.pallas_call(
        _kidx_kernel,
        grid=(s // tm,),
        in_specs=[pl.BlockSpec((tm, d), lambda i: (i, 0)),
                  pl.BlockSpec((d, LANES), lambda i: (0, 0)),
                  pl.BlockSpec((1, IDX_DIM), lambda i: (0, 0)),
                  pl.BlockSpec((1, IDX_DIM), lambda i: (0, 0))],
        out_specs=[pl.BlockSpec((tm, IDX_DIM), lambda i: (i, 0)),
                   pl.BlockSpec((IDX_HEADS, tm), lambda i: (0, i))],
        out_shape=[jax.ShapeDtypeStruct((s, IDX_DIM), BF16),
                   jax.ShapeDtypeStruct((IDX_HEADS, s), F32)],
        compiler_params=_cparams(("parallel",)),
        name="proj_kidx",
    )(n, w, g, b)


def _sgu_kernel(u_ref, v_ref, ws_ref, bt_ref, o_ref):
    tm = u_ref.shape[0]
    r = lax.broadcasted_iota(I32, (CHUNK, CHUNK), 0)
    c = lax.broadcasted_iota(I32, (CHUNK, CHUNK), 1)
    causal = c <= r
    for g in range(SGU_GROUPS):
        cols = slice(g * LANES, (g + 1) * LANES)
        w = jnp.where(causal, ws_ref[g], 0.0).astype(BF16)
        bcol = bt_ref[:, g:g + 1]
        for ch in range(tm // CHUNK):
            rows = slice(ch * CHUNK, (ch + 1) * CHUNK)
            sv = jnp.dot(w, v_ref[rows, cols], preferred_element_type=F32) + bcol
            o_ref[rows, cols] = (u_ref[rows, cols].astype(F32) * sv).astype(o_ref.dtype)


def _sgu(u, v, ws, bt):
    s, wdt = u.shape
    return pl.pallas_call(
        _sgu_kernel,
        grid=(s // SGU_TM,),
        in_specs=[pl.BlockSpec((SGU_TM, wdt), lambda i: (i, 0)),
                  pl.BlockSpec((SGU_TM, wdt), lambda i: (i, 0)),
                  pl.BlockSpec((SGU_GROUPS, CHUNK, CHUNK), lambda i: (0, 0, 0)),
                  pl.BlockSpec((CHUNK, SGU_GROUPS), lambda i: (0, 0))],
        out_specs=pl.BlockSpec((SGU_TM, wdt), lambda i: (i, 0)),
        out_shape=jax.ShapeDtypeStruct((s, wdt), BF16),
        compiler_params=_cparams(("parallel",)),
        name="sgu",
    )(u, v, ws, bt)


def _tile_lanes(x, n):
    return x if n == 1 else jnp.concatenate([x] * n, axis=1)


def _sublane_all(x, op):
    for shift in (4, 2, 1):
        x = op(x, pltpu.roll(x, shift, 0))
    return x


def _count_keys(keys_ref, n_blocks, pred):
    tq = keys_ref.shape[1]

    def body(it, acc):
        off = pl.multiple_of(it * CNT_BLK, CNT_BLK)
        blk = keys_ref[pl.ds(off, CNT_BLK), :].reshape(CNT_BLK // SUBLANES, SUBLANES, tq)
        ind = jnp.where(pred(blk, off), 1, 0)
        return acc + ind.reshape(-1, CNT_ACC, SUBLANES, tq).sum(axis=0)

    acc = lax.fori_loop(0, n_blocks, body, jnp.zeros((CNT_ACC, SUBLANES, tq), I32))
    tot = jnp.sum(acc.sum(axis=0).astype(F32), axis=0, keepdims=True)
    return jnp.broadcast_to(tot, (SUBLANES, tq)).astype(I32)


def _attn_kernel(itab_ref, jtab_ref, qit_ref, wit_ref, ki_ref, q_ref, kt_ref, v_ref, o_ref,
                 keys_ref, r_ref, gmax_ref, thr_ref, m_ref, acc_ref, *, alibi_c, topk, idx_bits):
    i = itab_ref[pl.program_id(0)]
    j = jtab_ref[pl.program_id(0)]
    tq, tk = ATT_TQ, ATT_TK
    hw = HEAD_DIM + AUG
    q0 = i * tq
    n_kv = (q0 + tq + tk - 1) // tk
    n_blk = (q0 + tq + CNT_BLK - 1) // CNT_BLK

    @pl.when(j == 0)
    def _index_and_threshold():
        m_ref[...] = jnp.full(m_ref.shape, NEG, F32)
        acc_ref[...] = jnp.zeros(acc_ref.shape, F32)

        def index_chunk(off, n_sub):
            rows = n_sub * IDX_TK
            r_ref[0:rows, :] = jnp.dot(ki_ref[pl.ds(off, rows), :], qit_ref[0],
                                       preferred_element_type=F32)
            for sub in range(n_sub):
                r0 = sub * IDX_TK
                s_idx = off + r0 + lax.broadcasted_iota(I32, (IDX_TK, LANES), 0)
                for g in range(tq // LANES):
                    lanes = slice(g * LANES, (g + 1) * LANES)
                    t_idx = q0 + g * LANES + lax.broadcasted_iota(I32, (IDX_TK, LANES), 1)
                    acc = jnp.zeros((IDX_TK, LANES), F32)
                    for h in range(IDX_HEADS):
                        rr = r_ref[r0:r0 + IDX_TK, h * tq + g * LANES:h * tq + (g + 1) * LANES]
                        acc = acc + jnp.maximum(rr, 0.0) * wit_ref[h:h + 1, lanes]
                    score = jnp.where(s_idx <= t_idx, acc, -jnp.inf)
                    keys_ref[pl.ds(pl.multiple_of(off + r0, IDX_TK), IDX_TK), lanes] = score
                    gmax_ref[:, lanes] = jnp.maximum(gmax_ref[:, lanes], score)

        def chunk_body(c, carry):
            index_chunk(pl.multiple_of(c * (IDX_SUB * IDX_TK), IDX_SUB * IDX_TK), IDX_SUB)
            return carry

        gmax_ref[...] = jnp.full(gmax_ref.shape, -jnp.inf, F32)
        n_idx = (q0 + tq) // IDX_TK
        lax.fori_loop(0, n_idx // IDX_SUB, chunk_body, 0)
        for rest in range(1, IDX_SUB):
            @pl.when(n_idx % IDX_SUB == rest)
            def _(rest=rest):
                index_chunk(pl.multiple_of((n_idx - rest) * IDX_TK, IDX_TK), rest)

        def fill_body(f, carry):
            off = pl.multiple_of(q0 + tq + f * tq, tq)
            keys_ref[pl.ds(off, tq), :] = jnp.full((tq, tq), -jnp.inf, F32)
            return carry

        lax.fori_loop(0, (n_blk * CNT_BLK - (q0 + tq)) // tq, fill_body, 0)

        t_row = q0 + lax.broadcasted_iota(I32, (SUBLANES, tq), 1)
        kq = jnp.minimum(topk, t_row + 1)

        g = gmax_ref[...].reshape(IDX_TK // SUBLANES, SUBLANES, tq)
        hi0 = _sublane_all(g.max(axis=0), jnp.maximum)
        lo0 = _sublane_all(jnp.where(g == -jnp.inf, jnp.inf, g).min(axis=0), jnp.minimum)

        def bis_cond(st):
            it, _, _, _, _, done = st
            return jnp.logical_and(it < BISECT_CAP, done == 0)

        def bis_body(st):
            it, lo, hi, cnt, fin, _ = st
            mid = 0.5 * lo + 0.5 * hi
            stuck = (mid <= lo) | (mid >= hi)
            cand = jnp.where(stuck, hi, mid)
            tot = _count_keys(keys_ref, n_blk, lambda blk, off: blk >= cand[None])
            take = (tot >= kq) & (fin == 0)
            lo = jnp.where(take, cand, lo)
            cnt = jnp.where(take, tot, cnt)
            hi = jnp.where(take | (fin != 0), hi, cand)
            fin = jnp.where(stuck, 1, fin)
            settled = (cnt == kq) | (fin != 0)
            done = (jnp.min(jnp.where(settled, 1.0, 0.0)) > 0.5).astype(I32)
            return it + 1, lo, hi, cnt, fin, done

        empty = lo0 > hi0
        _, thr, _, cnt, _, _ = lax.while_loop(
            bis_cond, bis_body,
            (jnp.int32(0), lo0, hi0, jnp.where(empty, 0, -1), jnp.where(empty, 1, 0), jnp.int32(0)))
        thr_ref[...] = thr
        cnt = lax.cond(jnp.min(cnt.astype(F32)) < 0.0,
                       lambda: jnp.where(cnt < 0, _count_keys(
                           keys_ref, n_blk, lambda blk, off: blk >= thr[None]), cnt),
                       lambda: cnt)

        @pl.when(jnp.max(jnp.where(cnt > kq, 1.0, 0.0)) > 0.5)
        def _break_ties():
            def row_idx(off):
                return off + lax.broadcasted_iota(I32, (CNT_BLK, tq), 0).reshape(
                    CNT_BLK // SUBLANES, SUBLANES, tq)

            above = _count_keys(keys_ref, n_blk, lambda blk, off: blk > thr[None])
            need = kq - above

            def cut_body(b, p):
                c = p + lax.shift_left(jnp.int32(1), idx_bits - 1 - b)
                below = _count_keys(
                    keys_ref, n_blk, lambda blk, off: (blk == thr[None]) & (row_idx(off) < c[None]))
                return jnp.where(below < need, c, p)

            p = lax.fori_loop(0, idx_bits, cut_body, jnp.zeros((SUBLANES, tq), I32))

            def drop_body(it, carry):
                off = pl.multiple_of(it * CNT_BLK, CNT_BLK)
                blk = keys_ref[pl.ds(off, CNT_BLK), :].reshape(CNT_BLK // SUBLANES, SUBLANES, tq)
                new = jnp.where((blk == thr[None]) & (row_idx(off) > p[None]), -jnp.inf, blk)
                keys_ref[pl.ds(off, CNT_BLK), :] = new.reshape(CNT_BLK, tq)
                return carry

            lax.fori_loop(0, n_blk, drop_body, 0)

    k0 = pl.multiple_of(j * tk, tk)
    nl = tk // LANES
    sel_t = keys_ref[pl.ds(k0, tk), :] >= thr_ref[0:1, :]
    bias = jnp.transpose(jnp.where(sel_t, 0.0, NEG))
    tile_dist = jnp.full((tq, LANES), k0 - q0, I32).astype(F32)
    ones_col = (lax.broadcasted_iota(I32, (tk, LANES), 1) == 0).astype(BF16)

    def logits(h):
        s = bias + jnp.dot(q_ref[:, h * hw:(h + 1) * hw], kt_ref[h * hw:(h + 1) * hw, :],
                           preferred_element_type=F32)
        return s, jnp.max(s, axis=-1, keepdims=True)

    def probs(h, s, s_max):
        shift = tile_dist * alibi_c[h]
        m_prev = m_ref[h]
        m_new = jnp.maximum(m_prev, s_max + shift)
        m_ref[h] = m_new
        p = jnp.exp2(s - _tile_lanes(m_new - shift, nl)).astype(BF16)
        return p, jnp.exp2(m_prev - m_new)

    def accumulate(h, p, alpha):
        v_aug = jnp.concatenate([v_ref[:, h * HEAD_DIM:(h + 1) * HEAD_DIM], ones_col], axis=1)
        acc_ref[h] = _tile_lanes(alpha, 2) * acc_ref[h] + jnp.dot(
            p, v_aug, preferred_element_type=F32)

    st_a, st_b = {}, {}
    for step in range(ATTN_HEADS + 2):
        if 0 <= step - 2 < ATTN_HEADS:
            accumulate(step - 2, *st_b.pop(step - 2))
        if step < ATTN_HEADS:
            st_a[step] = logits(step)
        if 0 <= step - 1 < ATTN_HEADS:
            st_b[step - 1] = probs(step - 1, *st_a.pop(step - 1))

    @pl.when(j == n_kv - 1)
    def _finish():
        for h in range(ATTN_HEADS):
            acc = acc_ref[h]
            o_ref[:, h * HEAD_DIM:(h + 1) * HEAD_DIM] = (
                acc[:, :HEAD_DIM] / acc[:, HEAD_DIM:HEAD_DIM + 1]).astype(o_ref.dtype)


def _alibi_pieces():
    out = []
    for h in range(ATTN_HEADS):
        c = np.float32(LOG2E * 2.0 ** (-8.0 * (h + 1) / ATTN_HEADS))
        c1 = np.float32(c.astype(BF16))
        c2 = np.float32(np.float32(c - c1).astype(BF16))
        c3 = np.float32(np.float32(c - c1 - c2).astype(BF16))
        out.append((float(c1), float(c2), float(c3)))
    return out


def _alibi_tables():
    pieces = np.asarray(_alibi_pieces(), np.float32)
    cpos = np.zeros((ATTN_HEADS, AUG), np.float32)
    cneg = np.zeros((ATTN_HEADS, AUG, 1), np.float32)
    cpos[:, 0:3] = pieces
    cpos[:, 3:6] = pieces
    cneg[:, 6:9, 0] = -pieces
    cneg[:, 9:12, 0] = -pieces
    alibi_c = tuple(float(p[0] + p[1] + p[2]) for p in pieces)
    return jnp.asarray(cpos), jnp.asarray(cneg), alibi_c


def _attention(qit, wit, ki, q_aug, kt_aug, v, alibi_c):
    s = q_aug.shape[0]
    tq, tk = ATT_TQ, ATT_TK
    hw = HEAD_DIM + AUG
    assert s % CNT_BLK == 0 and CNT_BLK % tk == 0 and CNT_BLK % tq == 0 and tq % LANES == 0
    topk = min(TOPK_MAX, s // 4)
    idx_bits = max(1, (s - 1).bit_length())
    assert IDX_TK >= topk and tq % IDX_TK == 0

    steps = [(i, j) for i in range(s // tq) for j in range(((i + 1) * tq + tk - 1) // tk)]
    itab = jnp.asarray(np.array([p[0] for p in steps], np.int32))
    jtab = jnp.asarray(np.array([p[1] for p in steps], np.int32))

    grid_spec = pltpu.PrefetchScalarGridSpec(
        num_scalar_prefetch=2,
        grid=(len(steps),),
        in_specs=[pl.BlockSpec((1, IDX_DIM, IDX_HEADS * tq), lambda t, it, jt: (it[t], 0, 0)),
                  pl.BlockSpec((IDX_HEADS, tq), lambda t, it, jt: (0, it[t])),
                  pl.BlockSpec((s, IDX_DIM), lambda t, it, jt: (0, 0)),
                  pl.BlockSpec((tq, ATTN_HEADS * hw), lambda t, it, jt: (it[t], 0)),
                  pl.BlockSpec((ATTN_HEADS * hw, tk), lambda t, it, jt: (0, jt[t])),
                  pl.BlockSpec((tk, ATTN_WIDTH), lambda t, it, jt: (jt[t], 0))],
        out_specs=pl.BlockSpec((tq, ATTN_WIDTH), lambda t, it, jt: (it[t], 0)),
        scratch_shapes=[pltpu.VMEM((s, tq), F32),
                        pltpu.VMEM((IDX_SUB * IDX_TK, IDX_HEADS * tq), F32),
                        pltpu.VMEM((IDX_TK, tq), F32),
                        pltpu.VMEM((SUBLANES, tq), F32),
                        pltpu.VMEM((ATTN_HEADS, tq, LANES), F32),
                        pltpu.VMEM((ATTN_HEADS, tq, 2 * HEAD_DIM), F32)])
    return pl.pallas_call(
        functools.partial(_attn_kernel, alibi_c=alibi_c, topk=topk, idx_bits=idx_bits),
        grid_spec=grid_spec,
        out_shape=jax.ShapeDtypeStruct((s, ATTN_WIDTH), BF16),
        compiler_params=_cparams(("arbitrary",)),
        name="dsa_attention",
    )(itab, jtab, qit, wit, ki, q_aug, kt_aug, v)


def _merge_kernel(ya_ref, yb_ref, sg_a_ref, sg_b_ref, x_ref, gt_ref, wa_ref, wb_ref, wo_ref, o_ref):
    a = jnp.dot(ya_ref[...], wa_ref[...], preferred_element_type=F32)
    b = jnp.dot(yb_ref[...], wb_ref[...], preferred_element_type=F32)
    merged = sg_a_ref[...].astype(F32) * a + sg_b_ref[...].astype(F32) * b
    o_ref[...] = x_ref[...] + gt_ref[...] * jnp.dot(
        merged.astype(BF16), wo_ref[...], preferred_element_type=F32)


def _merge(ya, yb, sg, x, gt, wa, wb, wo):
    s, d = x.shape
    tm = MERGE_TM
    wdt = ya.shape[1]
    const = lambda i: (0, 0)
    return pl.pallas_call(
        _merge_kernel,
        grid=(s // tm,),
        in_specs=[pl.BlockSpec((tm, wdt), lambda i: (i, 0)),
                  pl.BlockSpec((tm, wdt), lambda i: (i, 0)),
                  pl.BlockSpec((tm, d), lambda i: (i, 0)),
                  pl.BlockSpec((tm, d), lambda i: (i, 1)),
                  pl.BlockSpec((tm, d), lambda i: (i, 0)),
                  pl.BlockSpec((1, d), const),
                  pl.BlockSpec((wdt, d), const),
                  pl.BlockSpec((wdt, d), const),
                  pl.BlockSpec((d, d), const)],
        out_specs=pl.BlockSpec((tm, d), lambda i: (i, 0)),
        out_shape=jax.ShapeDtypeStruct((s, d), F32),
        compiler_params=_cparams(("parallel",)),
        name="merge",
    )(ya, yb, sg, sg, x, gt, wa, wb, wo)


def _pad_cols(w, n):
    return jnp.pad(w, ((0, 0), (0, n - w.shape[1])))


def _layer(x, c, w_ada, b_ada, g_norm1, w1_gate, w1_up, w1_down, g_norm2, w_in, g_sgu, w_spatial,
           b_spatial, g_q, g_k, g_kidx, b_kidx, w_branch_a, w_branch_b, w_out, g_norm3,
           w2_gate, w2_up, w2_down):
    s, d = x.shape
    dff = w1_gate.shape[1]
    dff_p = -(-dff // FFN_TF) * FFN_TF

    ada = _ada(c.reshape(d, 1), w_ada, b_ada.reshape(1, -1))
    sh1, sc1, gt1, sh2, sc2, gt2, sh3, sc3, gt3 = [ada[:, k * d:(k + 1) * d] for k in range(N_ADA)]

    def ffn_weights(wg, wu, wd):
        return (_pad_cols(wg.astype(BF16), dff_p), _pad_cols(wu.astype(BF16), dff_p),
                jnp.pad(wd.astype(BF16), ((0, dff_p - dff), (0, 0))))

    x1 = _ffn(x, g_norm1.reshape(1, d), sh1, sc1, gt1, *ffn_weights(w1_gate, w1_up, w1_down))

    n2 = _normmod(x1, g_norm2.reshape(1, d), sh2, sc2)
    wi_b = w_in.astype(BF16)
    o_u, o_v, o_q = 0, SGU_WIDTH, 2 * SGU_WIDTH
    o_k, o_vv = o_q + ATTN_WIDTH, o_q + 2 * ATTN_WIDTH
    o_qi = o_vv + ATTN_WIDTH
    o_ki = o_qi + IDX_HEADS * IDX_DIM
    o_g = o_ki + IDX_DIM + IDX_HEADS

    ug = _proj(n2, wi_b[:, o_u:o_v], "gelu")
    vn = _proj(n2, wi_b[:, o_v:o_q], "gelu_gnorm", gain=g_sgu.reshape(1, SGU_WIDTH))
    vv = _proj(n2, wi_b[:, o_vv:o_qi], "none")
    sg = _proj(n2, wi_b[:, o_g:], "sigmoid")
    ki, wit = _kidx(n2, _pad_cols(wi_b[:, o_ki:o_g], LANES),
                    g_kidx.reshape(1, IDX_DIM), b_kidx.reshape(1, IDX_DIM))

    cpos, cneg, alibi_c = _alibi_tables()
    hw = HEAD_DIM + AUG
    row = lambda i: (0, 0)
    q_aug = _attn_proj(
        functools.partial(_attn_q_kernel, post_scale=HEAD_DIM ** -0.5 * LOG2E),
        n2, wi_b[:, o_q:o_k], [jnp.tile(g_q, ATTN_HEADS).reshape(1, -1), cpos],
        [pl.BlockSpec((1, ATTN_WIDTH), row), pl.BlockSpec((ATTN_HEADS, AUG), row)],
        jax.ShapeDtypeStruct((s, ATTN_HEADS * hw), BF16),
        pl.BlockSpec((PROJ_TM, ATTN_HEADS * hw), lambda i: (i, 0)), "proj_attn_q")
    kt_aug = _attn_proj(
        _attn_kt_kernel, n2, wi_b[:, o_k:o_vv], [jnp.tile(g_k, ATTN_HEADS).reshape(1, -1), cneg],
        [pl.BlockSpec((1, ATTN_WIDTH), row), pl.BlockSpec((ATTN_HEADS, AUG, 1), lambda i: (0, 0, 0))],
        jax.ShapeDtypeStruct((ATTN_HEADS * hw, s), BF16),
        pl.BlockSpec((ATTN_HEADS * hw, PROJ_TM), lambda i: (0, i)), "proj_attn_kt")
    qit = _attn_proj(
        _idx_q_kernel, n2, wi_b[:, o_qi:o_ki], [], [],
        jax.ShapeDtypeStruct((s // ATT_TQ, IDX_DIM, IDX_HEADS * ATT_TQ), BF16),
        pl.BlockSpec((PROJ_TM // ATT_TQ, IDX_DIM, IDX_HEADS * ATT_TQ), lambda i: (i, 0, 0)), "proj_idx_q")

    y_a = _sgu(ug, vn, w_spatial, jnp.transpose(b_spatial))
    y_b = _attention(qit, wit, ki, q_aug, kt_aug, vv, alibi_c)

    x2 = _merge(y_a, y_b, sg, x1, gt2, w_branch_a.astype(BF16), w_branch_b.astype(BF16),
                w_out.astype(BF16))

    return _ffn(x2, g_norm3.reshape(1, d), sh3, sc3, gt3, *ffn_weights(w2_gate, w2_up, w2_down))


def kernel(x, c, w_ada, b_ada, g_norm1, w1_gate, w1_up, w1_down, g_norm2, w_in, g_sgu, w_spatial,
           b_spatial, g_q, g_k, g_kidx, b_kidx, w_branch_a, w_branch_b, w_out, g_norm3,
           w2_gate, w2_up, w2_down):
    batch, depth = x.shape[0], w_ada.shape[0]
    outs = []
    for b in range(batch):
        xb = x[b]
        for l in range(depth):
            xb = _layer(xb, c[b], w_ada[l], b_ada[l], g_norm1[l], w1_gate[l], w1_up[l], w1_down[l],
                        g_norm2[l], w_in[l], g_sgu[l], w_spatial[l], b_spatial[l], g_q[l], g_k[l],
                        g_kidx[l], b_kidx[l], w_branch_a[l], w_branch_b[l], w_out[l], g_norm3[l],
                        w2_gate[l], w2_up[l], w2_down[l])
        outs.append(xb[None])
    return outs[0] if batch == 1 else jnp.concatenate(outs)
```

```python
import functools

import numpy as np
import jax
import jax.numpy as jnp
from jax import lax
from jax.experimental import pallas as pl
from jax.experimental.pallas import tpu as pltpu

F32 = jnp.float32
BF16 = jnp.bfloat16
I32 = jnp.int32

CHUNK = 128
SGU_GROUPS = 8
SGU_WIDTH = 1024
ATTN_HEADS = 8
HEAD_DIM = 128
ATTN_WIDTH = ATTN_HEADS * HEAD_DIM
IDX_HEADS = 16
IDX_DIM = 64
TOPK_MAX = 256
N_ADA = 9
EPS = 1e-6
NEG = -1e30

LANES = 128
SUBLANES = 8
VMEM_LIMIT = 56 * 1024 * 1024

FFN_TM = 512
FFN_TF = 512
PROJ_TM = 1024
PROJ_TN = 1024
SGU_TM = 512
MERGE_TM = 256
ATT_TQ = 256
ATT_TK = 1024
IDX_TK = 256
IDX_SUB = 2
CNT_BLK = 1024
CNT_ACC = 8
BISECT_CAP = 512
AUG = 128
LOG2E = 1.4426950408889634


def _cparams(sem):
    return pltpu.CompilerParams(dimension_semantics=sem, vmem_limit_bytes=VMEM_LIMIT)


def _gelu_tanh(x):
    return 0.5 * x * (1.0 + jnp.tanh(0.7978845608028654 * (x + 0.044715 * (x * x * x))))


def _rms_mod(x, g, sh, sc):
    ms = jnp.mean(x * x, axis=-1, keepdims=True)
    y = (x * lax.rsqrt(ms + EPS)) * g
    return y * (1.0 + sc) + sh


def _ada_kernel(c_ref, w_ref, b_ref, o_ref, sb_ref):
    d = w_ref.shape[0]
    tn = w_ref.shape[1]

    @pl.when(pl.program_id(0) == 0)
    def _():
        cc = c_ref[...]
        sb_ref[...] = jnp.broadcast_to(cc * jax.nn.sigmoid(cc), (d, LANES))

    sb = sb_ref[...]
    for cb in range(tn // LANES):
        cols = slice(cb * LANES, (cb + 1) * LANES)
        prod = (w_ref[:, cols] * sb).reshape(d // SUBLANES, SUBLANES, LANES).sum(axis=0)
        o_ref[:, cols] = prod.sum(axis=0, keepdims=True) + b_ref[:, cols]


def _ada(c_col, w, b):
    d, n = w.shape
    tn = 1024
    return pl.pallas_call(
        _ada_kernel,
        grid=(n // tn,),
        in_specs=[pl.BlockSpec((d, 1), lambda j: (0, 0)),
                  pl.BlockSpec((d, tn), lambda j: (0, j)),
                  pl.BlockSpec((1, tn), lambda j: (0, j))],
        out_specs=pl.BlockSpec((1, tn), lambda j: (0, j)),
        out_shape=jax.ShapeDtypeStruct((1, n), F32),
        scratch_shapes=[pltpu.VMEM((d, LANES), F32)],
        compiler_params=_cparams(("arbitrary",)),
        name="ada",
    )(c_col, w, b)


def _swiglu_chunk(n, wg_ref, wu_ref, wd_ref):
    g = jnp.dot(n, wg_ref[...], preferred_element_type=F32)
    u = jnp.dot(n, wu_ref[...], preferred_element_type=F32)
    h = ((g * jax.nn.sigmoid(g)) * u).astype(BF16)
    return jnp.dot(h, wd_ref[...], preferred_element_type=F32)


def _ffn_kernel(x_ref, g_ref, sh_ref, sc_ref, gt_ref, wg_ref, wu_ref, wd_ref, *rest):
    *tail, o_ref, n_ref = rest
    j = pl.program_id(1)

    @pl.when(j == 0)
    def _():
        n_ref[...] = _rms_mod(x_ref[...], g_ref[...], sh_ref[...], sc_ref[...]).astype(BF16)

    n = n_ref[...]
    d = _swiglu_chunk(n, wg_ref, wu_ref, wd_ref)

    @pl.when(j == 0)
    def _():
        o_ref[...] = d

    @pl.when(j > 0)
    def _():
        o_ref[...] += d

    @pl.when(j == pl.num_programs(1) - 1)
    def _():
        y = o_ref[...]
        if tail:
            y = y + _swiglu_chunk(n, *tail)
        o_ref[...] = x_ref[...] + (0.5 * gt_ref[...]) * y


def _ffn(x, g, sh, sc, gt, wg, wu, wd):
    s, d = x.shape
    f = wg.shape[1]
    tm = min(FFN_TM, s)
    n_main = f // FFN_TF
    f_main = n_main * FFN_TF
    row = pl.BlockSpec((1, d), lambda i, j: (0, 0))
    in_specs = [pl.BlockSpec((tm, d), lambda i, j: (i, 0)), row, row, row, row,
                pl.BlockSpec((d, FFN_TF), lambda i, j: (0, j)),
                pl.BlockSpec((d, FFN_TF), lambda i, j: (0, j)),
                pl.BlockSpec((FFN_TF, d), lambda i, j: (j, 0))]
    args = [x, g, sh, sc, gt, wg, wu, wd]
    if f_main < f:
        assert (f - f_main) % LANES == 0
        args += [wg[:, f_main:], wu[:, f_main:], wd[f_main:]]
        in_specs += [pl.BlockSpec((d, f - f_main), lambda i, j: (0, 0)),
                     pl.BlockSpec((d, f - f_main), lambda i, j: (0, 0)),
                     pl.BlockSpec((f - f_main, d), lambda i, j: (0, 0))]
    return pl.pallas_call(
        _ffn_kernel,
        grid=(s // tm, n_main),
        in_specs=in_specs,
        out_specs=pl.BlockSpec((tm, d), lambda i, j: (i, 0)),
        out_shape=jax.ShapeDtypeStruct((s, d), F32),
        scratch_shapes=[pltpu.VMEM((tm, d), BF16)],
        compiler_params=_cparams(("parallel", "arbitrary")),
        name="ffn",
    )(*args)


def _normmod_kernel(x_ref, g_ref, sh_ref, sc_ref, o_ref):
    o_ref[...] = _rms_mod(x_ref[...], g_ref[...], sh_ref[...], sc_ref[...]).astype(o_ref.dtype)


def _normmod(x, g, sh, sc):
    s, d = x.shape
    tm = 512
    row = pl.BlockSpec((1, d), lambda i: (0, 0))
    return pl.pallas_call(
        _normmod_kernel,
        grid=(s // tm,),
        in_specs=[pl.BlockSpec((tm, d), lambda i: (i, 0)), row, row, row],
        out_specs=pl.BlockSpec((tm, d), lambda i: (i, 0)),
        out_shape=jax.ShapeDtypeStruct((s, d), BF16),
        compiler_params=_cparams(("parallel",)),
        name="normmod",
    )(x, g, sh, sc)


def _group_rms(z, gain, post_scale):
    outs = []
    for gidx in range(z.shape[1] // LANES):
        cols = slice(gidx * LANES, (gidx + 1) * LANES)
        zg = z[:, cols]
        ms = jnp.mean(zg * zg, axis=-1, keepdims=True)
        y = (zg * lax.rsqrt(ms + EPS)) * gain[:, cols]
        if post_scale != 1.0:
            y = y * post_scale
        outs.append(y)
    return jnp.concatenate(outs, axis=-1)


def _proj_kernel(n_ref, w_ref, *rest, mode, post_scale):
    o_ref = rest[-1]
    z = jnp.dot(n_ref[...], w_ref[...], preferred_element_type=F32)
    if mode == "gelu":
        out = _gelu_tanh(z)
    elif mode == "gelu_gnorm":
        out = _group_rms(_gelu_tanh(z), rest[0][...], post_scale)
    elif mode == "gnorm":
        out = _group_rms(z, rest[0][...], post_scale)
    elif mode == "sigmoid":
        out = jax.nn.sigmoid(z)
    else:
        out = z
    o_ref[...] = out.astype(o_ref.dtype)


def _proj(n, w, mode, gain=None, post_scale=1.0, out_dtype=BF16):
    s, d = n.shape
    nout = w.shape[1]
    in_specs = [pl.BlockSpec((PROJ_TM, d), lambda i, j: (i, 0)),
                pl.BlockSpec((d, PROJ_TN), lambda i, j: (0, j))]
    args = [n, w]
    if gain is not None:
        in_specs.append(pl.BlockSpec((1, PROJ_TN), lambda i, j: (0, j)))
        args.append(gain)
    return pl.pallas_call(
        functools.partial(_proj_kernel, mode=mode, post_scale=post_scale),
        grid=(s // PROJ_TM, nout // PROJ_TN),
        in_specs=in_specs,
        out_specs=pl.BlockSpec((PROJ_TM, PROJ_TN), lambda i, j: (i, j)),
        out_shape=jax.ShapeDtypeStruct((s, nout), out_dtype),
        compiler_params=_cparams(("parallel", "arbitrary")),
        name="proj_" + mode,
    )(*args)


def _split_hi_lo(pos):
    return (pos >> 7) << 7, pos & (LANES - 1)


def _attn_q_kernel(n_ref, w_ref, gain_ref, cpos_ref, o_ref, *, post_scale):
    tm = n_ref.shape[0]
    y = _group_rms(jnp.dot(n_ref[...], w_ref[...], preferred_element_type=F32),
                   gain_ref[...], post_scale)
    col = lax.broadcasted_iota(I32, (tm, AUG), 1)
    t_hi, t_lo = _split_hi_lo(lax.broadcasted_iota(I32, (tm, AUG), 0) & (ATT_TQ - 1))
    tab = jnp.where((col >= 6) & (col < 9), t_hi, jnp.where((col >= 9) & (col < 12), t_lo, 0)).astype(F32)
    hw = HEAD_DIM + AUG
    for h in range(ATTN_HEADS):
        o_ref[:, h * hw:h * hw + HEAD_DIM] = y[:, h * HEAD_DIM:(h + 1) * HEAD_DIM].astype(o_ref.dtype)
        o_ref[:, h * hw + HEAD_DIM:(h + 1) * hw] = (tab + cpos_ref[h:h + 1, :]).astype(o_ref.dtype)


def _attn_kt_kernel(n_ref, w_ref, gain_ref, cneg_ref, o_ref):
    tm = n_ref.shape[0]
    y = _group_rms(jnp.dot(n_ref[...], w_ref[...], preferred_element_type=F32), gain_ref[...], 1.0)
    yt = jnp.transpose(y)
    row = lax.broadcasted_iota(I32, (AUG, tm), 0)
    pos = pl.program_id(0) * tm + lax.broadcasted_iota(I32, (AUG, tm), 1)
    u_hi, u_lo = _split_hi_lo(pos & (ATT_TK - 1))
    tab = jnp.where(row < 3, u_hi, jnp.where(row < 6, u_lo, 0)).astype(F32)
    hw = HEAD_DIM + AUG
    for h in range(ATTN_HEADS):
        o_ref[h * hw:h * hw + HEAD_DIM, :] = yt[h * HEAD_DIM:(h + 1) * HEAD_DIM, :].astype(o_ref.dtype)
        o_ref[h * hw + HEAD_DIM:(h + 1) * hw, :] = (tab + cneg_ref[h]).astype(o_ref.dtype)


def _idx_q_kernel(n_ref, w_ref, o_ref):
    zt = jnp.transpose(jnp.dot(n_ref[...], w_ref[...], preferred_element_type=F32))
    for qt in range(n_ref.shape[0] // ATT_TQ):
        for h in range(IDX_HEADS):
            o_ref[qt, :, h * ATT_TQ:(h + 1) * ATT_TQ] = zt[
                h * IDX_DIM:(h + 1) * IDX_DIM, qt * ATT_TQ:(qt + 1) * ATT_TQ].astype(o_ref.dtype)


def _attn_proj(kernel, n, w, extra, extra_specs, out_shape, out_spec, name):
    s, d = n.shape
    return pl.pallas_call(
        kernel,
        grid=(s // PROJ_TM,),
        in_specs=[pl.BlockSpec((PROJ_TM, d), lambda i: (i, 0)),
                  pl.BlockSpec(w.shape, lambda i: (0, 0))] + extra_specs,
        out_specs=out_spec,
        out_shape=out_shape,
        compiler_params=_cparams(("parallel",)),
        name=name,
    )(n, w, *extra)


def _kidx_kernel(n_ref, w_ref, g_ref, b_ref, ki_ref, wit_ref):
    z = jnp.dot(n_ref[...], w_ref[...], preferred_element_type=F32)
    ki = z[:, :IDX_DIM]
    mu = jnp.mean(ki, axis=-1, keepdims=True)
    var = jnp.mean(jnp.square(ki - mu), axis=-1, keepdims=True)
    y = (ki - mu) * lax.rsqrt(var + EPS)
    ki_ref[...] = (y * g_ref[...] + b_ref[...]).astype(ki_ref.dtype)
    wit_ref[...] = jnp.transpose(z)[IDX_DIM:IDX_DIM + IDX_HEADS, :] * (IDX_HEADS ** -0.5 * IDX_DIM ** -0.5)


def _kidx(n, w, g, b):
    s, d = n.shape
    tm = 512
    return pl.pallas_call(
        _kidx_kernel,
        grid=(s // tm,),
        in_specs=[pl.BlockSpec((tm, d), lambda i: (i, 0)),
                  pl.BlockSpec((d, LANES), lambda i: (0, 0)),
                  pl.BlockSpec((1, IDX_DIM), lambda i: (0, 0)),
                  pl.BlockSpec((1, IDX_DIM), lambda i: (0, 0))],
        out_specs=[pl.BlockSpec((tm, IDX_DIM), lambda i: (i, 0)),
                   pl.BlockSpec((IDX_HEADS, tm), lambda i: (0, i))],
        out_shape=[jax.ShapeDtypeStruct((s, IDX_DIM), BF16),
                   jax.ShapeDtypeStruct((IDX_HEADS, s), F32)],
        compiler_params=_cparams(("parallel",)),
        name="proj_kidx",
    )(n, w, g, b)


def _sgu_kernel(u_ref, v_ref, ws_ref, bt_ref, o_ref):
    tm = u_ref.shape[0]
    r = lax.broadcasted_iota(I32, (CHUNK, CHUNK), 0)
    c = lax.broadcasted_iota(I32, (CHUNK, CHUNK), 1)
    causal = c <= r
    for g in range(SGU_GROUPS):
        cols = slice(g * LANES, (g + 1) * LANES)
        w = jnp.where(causal, ws_ref[g], 0.0).astype(BF16)
        bcol = bt_ref[:, g:g + 1]
        for ch in range(tm // CHUNK):
            rows = slice(ch * CHUNK, (ch + 1) * CHUNK)
            sv = jnp.dot(w, v_ref[rows, cols], preferred_element_type=F32) + bcol
            o_ref[rows, cols] = (u_ref[rows, cols].astype(F32) * sv).astype(o_ref.dtype)


def _sgu(u, v, ws, bt):
    s, wdt = u.shape
    return pl.pallas_call(
        _sgu_kernel,
        grid=(s // SGU_TM,),
        in_specs=[pl.BlockSpec((SGU_TM, wdt), lambda i: (i, 0)),
                  pl.BlockSpec((SGU_TM, wdt), lambda i: (i, 0)),
                  pl.BlockSpec((SGU_GROUPS, CHUNK, CHUNK), lambda i: (0, 0, 0)),
                  pl.BlockSpec((CHUNK, SGU_GROUPS), lambda i: (0, 0))],
        out_specs=pl.BlockSpec((SGU_TM, wdt), lambda i: (i, 0)),
        out_shape=jax.ShapeDtypeStruct((s, wdt), BF16),
        compiler_params=_cparams(("parallel",)),
        name="sgu",
    )(u, v, ws, bt)


def _tile_lanes(x, n):
    return x if n == 1 else jnp.concatenate([x] * n, axis=1)


def _sublane_all(x, op):
    for shift in (4, 2, 1):
        x = op(x, pltpu.roll(x, shift, 0))
    return x


def _count_keys(keys_ref, n_blocks, pred):
    tq = keys_ref.shape[1]

    def body(it, acc):
        off = pl.multiple_of(it * CNT_BLK, CNT_BLK)
        blk = keys_ref[pl.ds(off, CNT_BLK), :].reshape(CNT_BLK // SUBLANES, SUBLANES, tq)
        ind = jnp.where(pred(blk, off), 1, 0)
        return acc + ind.reshape(-1, CNT_ACC, SUBLANES, tq).sum(axis=0)

    acc = lax.fori_loop(0, n_blocks, body, jnp.zeros((CNT_ACC, SUBLANES, tq), I32))
    tot = jnp.sum(acc.sum(axis=0).astype(F32), axis=0, keepdims=True)
    return jnp.broadcast_to(tot, (SUBLANES, tq)).astype(I32)


def _attn_kernel(itab_ref, jtab_ref, qit_ref, wit_ref, ki_ref, q_ref, kt_ref, v_ref, o_ref,
                 keys_ref, r_ref, gmax_ref, thr_ref, m_ref, acc_ref, *, alibi_c, topk, idx_bits):
    i = itab_ref[pl.program_id(0)]
    j = jtab_ref[pl.program_id(0)]
    tq, tk = ATT_TQ, ATT_TK
    hw = HEAD_DIM + AUG
    q0 = i * tq
    n_kv = (q0 + tq + tk - 1) // tk
    n_blk = (q0 + tq + CNT_BLK - 1) // CNT_BLK

    @pl.when(j == 0)
    def _index_and_threshold():
        m_ref[...] = jnp.full(m_ref.shape, NEG, F32)
        acc_ref[...] = jnp.zeros(acc_ref.shape, F32)

        def index_chunk(off, n_sub):
            rows = n_sub * IDX_TK
            r_ref[0:rows, :] = jnp.dot(ki_ref[pl.ds(off, rows), :], qit_ref[0],
                                       preferred_element_type=F32)
            for sub in range(n_sub):
                r0 = sub * IDX_TK
                s_idx = off + r0 + lax.broadcasted_iota(I32, (IDX_TK, LANES), 0)
                for g in range(tq // LANES):
                    lanes = slice(g * LANES, (g + 1) * LANES)
                    t_idx = q0 + g * LANES + lax.broadcasted_iota(I32, (IDX_TK, LANES), 1)
                    acc = jnp.zeros((IDX_TK, LANES), F32)
                    for h in range(IDX_HEADS):
                        rr = r_ref[r0:r0 + IDX_TK, h * tq + g * LANES:h * tq + (g + 1) * LANES]
                        acc = acc + jnp.maximum(rr, 0.0) * wit_ref[h:h + 1, lanes]
                    score = jnp.where(s_idx <= t_idx, acc, -jnp.inf)
                    keys_ref[pl.ds(pl.multiple_of(off + r0, IDX_TK), IDX_TK), lanes] = score
                    gmax_ref[:, lanes] = jnp.maximum(gmax_ref[:, lanes], score)

        def chunk_body(c, carry):
            index_chunk(pl.multiple_of(c * (IDX_SUB * IDX_TK), IDX_SUB * IDX_TK), IDX_SUB)
            return carry

        gmax_ref[...] = jnp.full(gmax_ref.shape, -jnp.inf, F32)
        n_idx = (q0 + tq) // IDX_TK
        lax.fori_loop(0, n_idx // IDX_SUB, chunk_body, 0)
        for rest in range(1, IDX_SUB):
            @pl.when(n_idx % IDX_SUB == rest)
            def _(rest=rest):
                index_chunk(pl.multiple_of((n_idx - rest) * IDX_TK, IDX_TK), rest)

        def fill_body(f, carry):
            off = pl.multiple_of(q0 + tq + f * tq, tq)
            keys_ref[pl.ds(off, tq), :] = jnp.full((tq, tq), -jnp.inf, F32)
            return carry

        lax.fori_loop(0, (n_blk * CNT_BLK - (q0 + tq)) // tq, fill_body, 0)

        t_row = q0 + lax.broadcasted_iota(I32, (SUBLANES, tq), 1)
        kq = jnp.minimum(topk, t_row + 1)

        g = gmax_ref[...].reshape(IDX_TK // SUBLANES, SUBLANES, tq)
        hi0 = _sublane_all(g.max(axis=0), jnp.maximum)
        lo0 = _sublane_all(jnp.where(g == -jnp.inf, jnp.inf, g).min(axis=0), jnp.minimum)

        def bis_cond(st):
            it, _, _, _, _, done = st
            return jnp.logical_and(it < BISECT_CAP, done == 0)

        def bis_body(st):
            it, lo, hi, cnt, fin, _ = st
            mid = 0.5 * lo + 0.5 * hi
            stuck = (mid <= lo) | (mid >= hi)
            cand = jnp.where(stuck, hi, mid)
            tot = _count_keys(keys_ref, n_blk, lambda blk, off: blk >= cand[None])
            take = (tot >= kq) & (fin == 0)
            lo = jnp.where(take, cand, lo)
            cnt = jnp.where(take, tot, cnt)
            hi = jnp.where(take | (fin != 0), hi, cand)
            fin = jnp.where(stuck, 1, fin)
            settled = (cnt == kq) | (fin != 0)
            done = (jnp.min(jnp.where(settled, 1.0, 0.0)) > 0.5).astype(I32)
            return it + 1, lo, hi, cnt, fin, done

        empty = lo0 > hi0
        _, thr, _, cnt, _, _ = lax.while_loop(
            bis_cond, bis_body,
            (jnp.int32(0), lo0, hi0, jnp.where(empty, 0, -1), jnp.where(empty, 1, 0), jnp.int32(0)))
        thr_ref[...] = thr
        cnt = lax.cond(jnp.min(cnt.astype(F32)) < 0.0,
                       lambda: jnp.where(cnt < 0, _count_keys(
                           keys_ref, n_blk, lambda blk, off: blk >= thr[None]), cnt),
                       lambda: cnt)

        @pl.when(jnp.max(jnp.where(cnt > kq, 1.0, 0.0)) > 0.5)
        def _break_ties():
            def row_idx(off):
                return off + lax.broadcasted_iota(I32, (CNT_BLK, tq), 0).reshape(
                    CNT_BLK // SUBLANES, SUBLANES, tq)

            above = _count_keys(keys_ref, n_blk, lambda blk, off: blk > thr[None])
            need = kq - above

            def cut_body(b, p):
                c = p + lax.shift_left(jnp.int32(1), idx_bits - 1 - b)
                below = _count_keys(
                    keys_ref, n_blk, lambda blk, off: (blk == thr[None]) & (row_idx(off) < c[None]))
                return jnp.where(below < need, c, p)

            p = lax.fori_loop(0, idx_bits, cut_body, jnp.zeros((SUBLANES, tq), I32))

            def drop_body(it, carry):
                off = pl.multiple_of(it * CNT_BLK, CNT_BLK)
                blk = keys_ref[pl.ds(off, CNT_BLK), :].reshape(CNT_BLK // SUBLANES, SUBLANES, tq)
                new = jnp.where((blk == thr[None]) & (row_idx(off) > p[None]), -jnp.inf, blk)
                keys_ref[pl.ds(off, CNT_BLK), :] = new.reshape(CNT_BLK, tq)
                return carry

            lax.fori_loop(0, n_blk, drop_body, 0)

    k0 = pl.multiple_of(j * tk, tk)
    nl = tk // LANES
    sel_t = keys_ref[pl.ds(k0, tk), :] >= thr_ref[0:1, :]
    bias = jnp.transpose(jnp.where(sel_t, 0.0, NEG))
    tile_dist = jnp.full((tq, LANES), k0 - q0, I32).astype(F32)
    ones_col = (lax.broadcasted_iota(I32, (tk, LANES), 1) == 0).astype(BF16)

    def logits(h):
        s = bias + jnp.dot(q_ref[:, h * hw:(h + 1) * hw], kt_ref[h * hw:(h + 1) * hw, :],
                           preferred_element_type=F32)
        return s, jnp.max(s, axis=-1, keepdims=True)

    def probs(h, s, s_max):
        shift = tile_dist * alibi_c[h]
        m_prev = m_ref[h]
        m_new = jnp.maximum(m_prev, s_max + shift)
        m_ref[h] = m_new
        p = jnp.exp2(s - _tile_lanes(m_new - shift, nl)).astype(BF16)
        return p, jnp.exp2(m_prev - m_new)

    def accumulate(h, p, alpha):
        v_aug = jnp.concatenate([v_ref[:, h * HEAD_DIM:(h + 1) * HEAD_DIM], ones_col], axis=1)
        acc_ref[h] = _tile_lanes(alpha, 2) * acc_ref[h] + jnp.dot(
            p, v_aug, preferred_element_type=F32)

    st_a, st_b = {}, {}
    for step in range(ATTN_HEADS + 2):
        if 0 <= step - 2 < ATTN_HEADS:
            accumulate(step - 2, *st_b.pop(step - 2))
        if step < ATTN_HEADS:
            st_a[step] = logits(step)
        if 0 <= step - 1 < ATTN_HEADS:
            st_b[step - 1] = probs(step - 1, *st_a.pop(step - 1))

    @pl.when(j == n_kv - 1)
    def _finish():
        for h in range(ATTN_HEADS):
            acc = acc_ref[h]
            o_ref[:, h * HEAD_DIM:(h + 1) * HEAD_DIM] = (
                acc[:, :HEAD_DIM] / acc[:, HEAD_DIM:HEAD_DIM + 1]).astype(o_ref.dtype)


def _alibi_pieces():
    out = []
    for h in range(ATTN_HEADS):
        c = np.float32(LOG2E * 2.0 ** (-8.0 * (h + 1) / ATTN_HEADS))
        c1 = np.float32(c.astype(BF16))
        c2 = np.float32(np.float32(c - c1).astype(BF16))
        c3 = np.float32(np.float32(c - c1 - c2).astype(BF16))
        out.append((float(c1), float(c2), float(c3)))
    return out


def _alibi_tables():
    pieces = np.asarray(_alibi_pieces(), np.float32)
    cpos = np.zeros((ATTN_HEADS, AUG), np.float32)
    cneg = np.zeros((ATTN_HEADS, AUG, 1), np.float32)
    cpos[:, 0:3] = pieces
    cpos[:, 3:6] = pieces
    cneg[:, 6:9, 0] = -pieces
    cneg[:, 9:12, 0] = -pieces
    alibi_c = tuple(float(p[0] + p[1] + p[2]) for p in pieces)
    return jnp.asarray(cpos), jnp.asarray(cneg), alibi_c


def _attention(qit, wit, ki, q_aug, kt_aug, v, alibi_c):
    s = q_aug.shape[0]
    tq, tk = ATT_TQ, ATT_TK
    hw = HEAD_DIM + AUG
    assert s % CNT_BLK == 0 and CNT_BLK % tk == 0 and CNT_BLK % tq == 0 and tq % LANES == 0
    topk = min(TOPK_MAX, s // 4)
    idx_bits = max(1, (s - 1).bit_length())
    assert IDX_TK >= topk and tq % IDX_TK == 0

    steps = [(i, j) for i in range(s // tq) for j in range(((i + 1) * tq + tk - 1) // tk)]
    itab = jnp.asarray(np.array([p[0] for p in steps], np.int32))
    jtab = jnp.asarray(np.array([p[1] for p in steps], np.int32))

    grid_spec = pltpu.PrefetchScalarGridSpec(
        num_scalar_prefetch=2,
        grid=(len(steps),),
        in_specs=[pl.BlockSpec((1, IDX_DIM, IDX_HEADS * tq), lambda t, it, jt: (it[t], 0, 0)),
                  pl.BlockSpec((IDX_HEADS, tq), lambda t, it, jt: (0, it[t])),
                  pl.BlockSpec((s, IDX_DIM), lambda t, it, jt: (0, 0)),
                  pl.BlockSpec((tq, ATTN_HEADS * hw), lambda t, it, jt: (it[t], 0)),
                  pl.BlockSpec((ATTN_HEADS * hw, tk), lambda t, it, jt: (0, jt[t])),
                  pl.BlockSpec((tk, ATTN_WIDTH), lambda t, it, jt: (jt[t], 0))],
        out_specs=pl.BlockSpec((tq, ATTN_WIDTH), lambda t, it, jt: (it[t], 0)),
        scratch_shapes=[pltpu.VMEM((s, tq), F32),
                        pltpu.VMEM((IDX_SUB * IDX_TK, IDX_HEADS * tq), F32),
                        pltpu.VMEM((IDX_TK, tq), F32),
                        pltpu.VMEM((SUBLANES, tq), F32),
                        pltpu.VMEM((ATTN_HEADS, tq, LANES), F32),
                        pltpu.VMEM((ATTN_HEADS, tq, 2 * HEAD_DIM), F32)])
    return pl.pallas_call(
        functools.partial(_attn_kernel, alibi_c=alibi_c, topk=topk, idx_bits=idx_bits),
        grid_spec=grid_spec,
        out_shape=jax.ShapeDtypeStruct((s, ATTN_WIDTH), BF16),
        compiler_params=_cparams(("arbitrary",)),
        name="dsa_attention",
    )(itab, jtab, qit, wit, ki, q_aug, kt_aug, v)


def _merge_kernel(ya_ref, yb_ref, sg_a_ref, sg_b_ref, x_ref, gt_ref, wa_ref, wb_ref, wo_ref, o_ref):
    a = jnp.dot(ya_ref[...], wa_ref[...], preferred_element_type=F32)
    b = jnp.dot(yb_ref[...], wb_ref[...], preferred_element_type=F32)
    merged = sg_a_ref[...].astype(F32) * a + sg_b_ref[...].astype(F32) * b
    o_ref[...] = x_ref[...] + gt_ref[...] * jnp.dot(
        merged.astype(BF16), wo_ref[...], preferred_element_type=F32)


def _merge(ya, yb, sg, x, gt, wa, wb, wo):
    s, d = x.shape
    tm = MERGE_TM
    wdt = ya.shape[1]
    const = lambda i: (0, 0)
    return pl.pallas_call(
        _merge_kernel,
        grid=(s // tm,),
        in_specs=[pl.BlockSpec((tm, wdt), lambda i: (i, 0)),
                  pl.BlockSpec((tm, wdt), lambda i: (i, 0)),
                  pl.BlockSpec((tm, d), lambda i: (i, 0)),
                  pl.BlockSpec((tm, d), lambda i: (i, 1)),
                  pl.BlockSpec((tm, d), lambda i: (i, 0)),
                  pl.BlockSpec((1, d), const),
                  pl.BlockSpec((wdt, d), const),
                  pl.BlockSpec((wdt, d), const),
                  pl.BlockSpec((d, d), const)],
        out_specs=pl.BlockSpec((tm, d), lambda i: (i, 0)),
        out_shape=jax.ShapeDtypeStruct((s, d), F32),
        compiler_params=_cparams(("parallel",)),
        name="merge",
    )(ya, yb, sg, sg, x, gt, wa, wb, wo)


def _pad_cols(w, n):
    return jnp.pad(w, ((0, 0), (0, n - w.shape[1])))


def _layer(x, c, w_ada, b_ada, g_norm1, w1_gate, w1_up, w1_down, g_norm2, w_in, g_sgu, w_spatial,
           b_spatial, g_q, g_k, g_kidx, b_kidx, w_branch_a, w_branch_b, w_out, g_norm3,
           w2_gate, w2_up, w2_down):
    s, d = x.shape

    ada = _ada(c.reshape(d, 1), w_ada, b_ada.reshape(1, -1))
    sh1, sc1, gt1, sh2, sc2, gt2, sh3, sc3, gt3 = [ada[:, k * d:(k + 1) * d] for k in range(N_ADA)]

    def ffn_weights(wg, wu, wd):
        return wg.astype(BF16), wu.astype(BF16), wd.astype(BF16)

    x1 = _ffn(x, g_norm1.reshape(1, d), sh1, sc1, gt1, *ffn_weights(w1_gate, w1_up, w1_down))

    n2 = _normmod(x1, g_norm2.reshape(1, d), sh2, sc2)
    wi_b = w_in.astype(BF16)
    o_u, o_v, o_q = 0, SGU_WIDTH, 2 * SGU_WIDTH
    o_k, o_vv = o_q + ATTN_WIDTH, o_q + 2 * ATTN_WIDTH
    o_qi = o_vv + ATTN_WIDTH
    o_ki = o_qi + IDX_HEADS * IDX_DIM
    o_g = o_ki + IDX_DIM + IDX_HEADS

    ug = _proj(n2, wi_b[:, o_u:o_v], "gelu")
    vn = _proj(n2, wi_b[:, o_v:o_q], "gelu_gnorm", gain=g_sgu.reshape(1, SGU_WIDTH))
    vv = _proj(n2, wi_b[:, o_vv:o_qi], "none")
    sg = _proj(n2, wi_b[:, o_g:], "sigmoid")
    ki, wit = _kidx(n2, _pad_cols(wi_b[:, o_ki:o_g], LANES),
                    g_kidx.reshape(1, IDX_DIM), b_kidx.reshape(1, IDX_DIM))

    cpos, cneg, alibi_c = _alibi_tables()
    hw = HEAD_DIM + AUG
    row = lambda i: (0, 0)
    q_aug = _attn_proj(
        functools.partial(_attn_q_kernel, post_scale=HEAD_DIM ** -0.5 * LOG2E),
        n2, wi_b[:, o_q:o_k], [jnp.tile(g_q, ATTN_HEADS).reshape(1, -1), cpos],
        [pl.BlockSpec((1, ATTN_WIDTH), row), pl.BlockSpec((ATTN_HEADS, AUG), row)],
        jax.ShapeDtypeStruct((s, ATTN_HEADS * hw), BF16),
        pl.BlockSpec((PROJ_TM, ATTN_HEADS * hw), lambda i: (i, 0)), "proj_attn_q")
    kt_aug = _attn_proj(
        _attn_kt_kernel, n2, wi_b[:, o_k:o_vv], [jnp.tile(g_k, ATTN_HEADS).reshape(1, -1), cneg],
        [pl.BlockSpec((1, ATTN_WIDTH), row), pl.BlockSpec((ATTN_HEADS, AUG, 1), lambda i: (0, 0, 0))],
        jax.ShapeDtypeStruct((ATTN_HEADS * hw, s), BF16),
        pl.BlockSpec((ATTN_HEADS * hw, PROJ_TM), lambda i: (0, i)), "proj_attn_kt")
    qit = _attn_proj(
        _idx_q_kernel, n2, wi_b[:, o_qi:o_ki], [], [],
        jax.ShapeDtypeStruct((s // ATT_TQ, IDX_DIM, IDX_HEADS * ATT_TQ), BF16),
        pl.BlockSpec((PROJ_TM // ATT_TQ, IDX_DIM, IDX_HEADS * ATT_TQ), lambda i: (i, 0, 0)), "proj_idx_q")

    y_a = _sgu(ug, vn, w_spatial, jnp.transpose(b_spatial))
    y_b = _attention(qit, wit, ki, q_aug, kt_aug, vv, alibi_c)

    x2 = _merge(y_a, y_b, sg, x1, gt2, w_branch_a.astype(BF16), w_branch_b.astype(BF16),
                w_out.astype(BF16))

    return _ffn(x2, g_norm3.reshape(1, d), sh3, sc3, gt3, *ffn_weights(w2_gate, w2_up, w2_down))


def kernel(x, c, w_ada, b_ada, g_norm1, w1_gate, w1_up, w1_down, g_norm2, w_in, g_sgu, w_spatial,
           b_spatial, g_q, g_k, g_kidx, b_kidx, w_branch_a, w_branch_b, w_out, g_norm3,
           w2_gate, w2_up, w2_down):
    batch, depth = x.shape[0], w_ada.shape[0]
    outs = []
    for b in range(batch):
        xb = x[b]
        for l in range(depth):
            xb = _layer(xb, c[b], w_ada[l], b_ada[l], g_norm1[l], w1_gate[l], w1_up[l], w1_down[l],
                        g_norm2[l], w_in[l], g_sgu[l], w_spatial[l], b_spatial[l], g_q[l], g_k[l],
                        g_kidx[l], b_kidx[l], w_branch_a[l], w_branch_b[l], w_out[l], g_norm3[l],
                        w2_gate[l], w2_up[l], w2_down[l])
        outs.append(xb[None])
    return outs[0] if batch == 1 else jnp.concatenate(outs)
```

```python
import functools

import numpy as np
import jax
import jax.numpy as jnp
from jax import lax
from jax.experimental import pallas as pl
from jax.experimental.pallas import tpu as pltpu

F32 = jnp.float32
BF16 = jnp.bfloat16
I32 = jnp.int32

CHUNK = 128
SGU_GROUPS = 8
SGU_WIDTH = 1024
ATTN_HEADS = 8
HEAD_DIM = 128
ATTN_WIDTH = ATTN_HEADS * HEAD_DIM
IDX_HEADS = 16
IDX_DIM = 64
TOPK_MAX = 256
N_ADA = 9
EPS = 1e-6
NEG = -1e30

LANES = 128
SUBLANES = 8
VMEM_LIMIT = 56 * 1024 * 1024

FFN_TM = 512
FFN_TF = 1024
PROJ_TM = 1024
PROJ_TN = 1024
SGU_TM = 512
MERGE_TM = 256
ATT_TQ = 256
ATT_TK = 1024
IDX_TK = 256
IDX_SUB = 2
CNT_BLK = 1024
CNT_ACC = 8
BISECT_CAP = 512
AUG = 128
LOG2E = 1.4426950408889634


def _cparams(sem):
    return pltpu.CompilerParams(dimension_semantics=sem, vmem_limit_bytes=VMEM_LIMIT)


def _gelu_tanh(x):
    return 0.5 * x * (1.0 + jnp.tanh(0.7978845608028654 * (x + 0.044715 * (x * x * x))))


def _rms_mod(x, g, sh, sc):
    ms = jnp.mean(x * x, axis=-1, keepdims=True)
    y = (x * lax.rsqrt(ms + EPS)) * g
    return y * (1.0 + sc) + sh


def _ada_kernel(c_ref, w_ref, b_ref, o_ref, sb_ref):
    d = w_ref.shape[0]
    tn = w_ref.shape[1]

    @pl.when(pl.program_id(0) == 0)
    def _():
        cc = c_ref[...]
        sb_ref[...] = jnp.broadcast_to(cc * jax.nn.sigmoid(cc), (d, LANES))

    sb = sb_ref[...]
    for cb in range(tn // LANES):
        cols = slice(cb * LANES, (cb + 1) * LANES)
        prod = (w_ref[:, cols] * sb).reshape(d // SUBLANES, SUBLANES, LANES).sum(axis=0)
        o_ref[:, cols] = prod.sum(axis=0, keepdims=True) + b_ref[:, cols]


def _ada(c_col, w, b):
    d, n = w.shape
    tn = 1024
    return pl.pallas_call(
        _ada_kernel,
        grid=(n // tn,),
        in_specs=[pl.BlockSpec((d, 1), lambda j: (0, 0)),
                  pl.BlockSpec((d, tn), lambda j: (0, j)),
                  pl.BlockSpec((1, tn), lambda j: (0, j))],
        out_specs=pl.BlockSpec((1, tn), lambda j: (0, j)),
        out_shape=jax.ShapeDtypeStruct((1, n), F32),
        scratch_shapes=[pltpu.VMEM((d, LANES), F32)],
        compiler_params=_cparams(("arbitrary",)),
        name="ada",
    )(c_col, w, b)


def _swiglu_chunk(n, wg_ref, wu_ref, wd_ref):
    g = jnp.dot(n, wg_ref[...], preferred_element_type=F32)
    u = jnp.dot(n, wu_ref[...], preferred_element_type=F32)
    h = ((g * jax.nn.sigmoid(g)) * u).astype(BF16)
    return jnp.dot(h, wd_ref[...], preferred_element_type=F32)


def _ffn_kernel(x_ref, g_ref, sh_ref, sc_ref, gt_ref, wg_ref, wu_ref, wd_ref, *rest):
    *tail, o_ref, n_ref = rest
    j = pl.program_id(1)

    @pl.when(j == 0)
    def _():
        n_ref[...] = _rms_mod(x_ref[...], g_ref[...], sh_ref[...], sc_ref[...]).astype(BF16)

    n = n_ref[...]
    d = _swiglu_chunk(n, wg_ref, wu_ref, wd_ref)

    @pl.when(j == 0)
    def _():
        o_ref[...] = d

    @pl.when(j > 0)
    def _():
        o_ref[...] += d

    @pl.when(j == pl.num_programs(1) - 1)
    def _():
        y = o_ref[...]
        if tail:
            y = y + _swiglu_chunk(n, *tail)
        o_ref[...] = x_ref[...] + (0.5 * gt_ref[...]) * y


def _ffn(x, g, sh, sc, gt, wg, wu, wd):
    s, d = x.shape
    f = wg.shape[1]
    tm = min(FFN_TM, s)
    n_main = f // FFN_TF
    f_main = n_main * FFN_TF
    row = pl.BlockSpec((1, d), lambda i, j: (0, 0))
    in_specs = [pl.BlockSpec((tm, d), lambda i, j: (i, 0), pipeline_mode=pl.Buffered(1)),
                row, row, row, row,
                pl.BlockSpec((d, FFN_TF), lambda i, j: (0, j)),
                pl.BlockSpec((d, FFN_TF), lambda i, j: (0, j)),
                pl.BlockSpec((FFN_TF, d), lambda i, j: (j, 0))]
    args = [x, g, sh, sc, gt, wg, wu, wd]
    if f_main < f:
        assert (f - f_main) % LANES == 0
        args += [wg[:, f_main:], wu[:, f_main:], wd[f_main:]]
        once = dict(pipeline_mode=pl.Buffered(1))
        in_specs += [pl.BlockSpec((d, f - f_main), lambda i, j: (0, 0), **once),
                     pl.BlockSpec((d, f - f_main), lambda i, j: (0, 0), **once),
                     pl.BlockSpec((f - f_main, d), lambda i, j: (0, 0), **once)]
    return pl.pallas_call(
        _ffn_kernel,
        grid=(s // tm, n_main),
        in_specs=in_specs,
        out_specs=pl.BlockSpec((tm, d), lambda i, j: (i, 0)),
        out_shape=jax.ShapeDtypeStruct((s, d), F32),
        scratch_shapes=[pltpu.VMEM((tm, d), BF16)],
        compiler_params=_cparams(("parallel", "arbitrary")),
        name="ffn",
    )(*args)


def _normmod_kernel(x_ref, g_ref, sh_ref, sc_ref, o_ref):
    o_ref[...] = _rms_mod(x_ref[...], g_ref[...], sh_ref[...], sc_ref[...]).astype(o_ref.dtype)


def _normmod(x, g, sh, sc):
    s, d = x.shape
    tm = 512
    row = pl.BlockSpec((1, d), lambda i: (0, 0))
    return pl.pallas_call(
        _normmod_kernel,
        grid=(s // tm,),
        in_specs=[pl.BlockSpec((tm, d), lambda i: (i, 0)), row, row, row],
        out_specs=pl.BlockSpec((tm, d), lambda i: (i, 0)),
        out_shape=jax.ShapeDtypeStruct((s, d), BF16),
        compiler_params=_cparams(("parallel",)),
        name="normmod",
    )(x, g, sh, sc)


def _group_rms(z, gain, post_scale):
    outs = []
    for gidx in range(z.shape[1] // LANES):
        cols = slice(gidx * LANES, (gidx + 1) * LANES)
        zg = z[:, cols]
        ms = jnp.mean(zg * zg, axis=-1, keepdims=True)
        y = (zg * lax.rsqrt(ms + EPS)) * gain[:, cols]
        if post_scale != 1.0:
            y = y * post_scale
        outs.append(y)
    return jnp.concatenate(outs, axis=-1)


def _proj_kernel(n_ref, w_ref, *rest, mode, post_scale):
    o_ref = rest[-1]
    z = jnp.dot(n_ref[...], w_ref[...], preferred_element_type=F32)
    if mode == "gelu":
        out = _gelu_tanh(z)
    elif mode == "gelu_gnorm":
        out = _group_rms(_gelu_tanh(z), rest[0][...], post_scale)
    elif mode == "gnorm":
        out = _group_rms(z, rest[0][...], post_scale)
    elif mode == "sigmoid":
        out = jax.nn.sigmoid(z)
    else:
        out = z
    o_ref[...] = out.astype(o_ref.dtype)


def _proj(n, w, mode, gain=None, post_scale=1.0, out_dtype=BF16):
    s, d = n.shape
    nout = w.shape[1]
    in_specs = [pl.BlockSpec((PROJ_TM, d), lambda i, j: (i, 0)),
                pl.BlockSpec((d, PROJ_TN), lambda i, j: (0, j))]
    args = [n, w]
    if gain is not None:
        in_specs.append(pl.BlockSpec((1, PROJ_TN), lambda i, j: (0, j)))
        args.append(gain)
    return pl.pallas_call(
        functools.partial(_proj_kernel, mode=mode, post_scale=post_scale),
        grid=(s // PROJ_TM, nout // PROJ_TN),
        in_specs=in_specs,
        out_specs=pl.BlockSpec((PROJ_TM, PROJ_TN), lambda i, j: (i, j)),
        out_shape=jax.ShapeDtypeStruct((s, nout), out_dtype),
        compiler_params=_cparams(("parallel", "arbitrary")),
        name="proj_" + mode,
    )(*args)


def _split_hi_lo(pos):
    return (pos >> 7) << 7, pos & (LANES - 1)


def _attn_q_kernel(n_ref, w_ref, gain_ref, cpos_ref, o_ref, *, post_scale):
    tm = n_ref.shape[0]
    y = _group_rms(jnp.dot(n_ref[...], w_ref[...], preferred_element_type=F32),
                   gain_ref[...], post_scale)
    col = lax.broadcasted_iota(I32, (tm, AUG), 1)
    t_hi, t_lo = _split_hi_lo(lax.broadcasted_iota(I32, (tm, AUG), 0) & (ATT_TQ - 1))
    tab = jnp.where((col >= 6) & (col < 9), t_hi, jnp.where((col >= 9) & (col < 12), t_lo, 0)).astype(F32)
    hw = HEAD_DIM + AUG
    for h in range(ATTN_HEADS):
        o_ref[:, h * hw:h * hw + HEAD_DIM] = y[:, h * HEAD_DIM:(h + 1) * HEAD_DIM].astype(o_ref.dtype)
        o_ref[:, h * hw + HEAD_DIM:(h + 1) * hw] = (tab + cpos_ref[h:h + 1, :]).astype(o_ref.dtype)


def _attn_kt_kernel(n_ref, w_ref, gain_ref, cneg_ref, o_ref):
    tm = n_ref.shape[0]
    y = _group_rms(jnp.dot(n_ref[...], w_ref[...], preferred_element_type=F32), gain_ref[...], 1.0)
    yt = jnp.transpose(y)
    row = lax.broadcasted_iota(I32, (AUG, tm), 0)
    pos = pl.program_id(0) * tm + lax.broadcasted_iota(I32, (AUG, tm), 1)
    u_hi, u_lo = _split_hi_lo(pos & (ATT_TK - 1))
    tab = jnp.where(row < 3, u_hi, jnp.where(row < 6, u_lo, 0)).astype(F32)
    hw = HEAD_DIM + AUG
    for h in range(ATTN_HEADS):
        o_ref[h * hw:h * hw + HEAD_DIM, :] = yt[h * HEAD_DIM:(h + 1) * HEAD_DIM, :].astype(o_ref.dtype)
        o_ref[h * hw + HEAD_DIM:(h + 1) * hw, :] = (tab + cneg_ref[h]).astype(o_ref.dtype)


def _idx_q_kernel(n_ref, w_ref, o_ref):
    zt = jnp.transpose(jnp.dot(n_ref[...], w_ref[...], preferred_element_type=F32))
    for qt in range(n_ref.shape[0] // ATT_TQ):
        for h in range(IDX_HEADS):
            o_ref[qt, :, h * ATT_TQ:(h + 1) * ATT_TQ] = zt[
                h * IDX_DIM:(h + 1) * IDX_DIM, qt * ATT_TQ:(qt + 1) * ATT_TQ].astype(o_ref.dtype)


def _attn_proj(kernel, n, w, extra, extra_specs, out_shape, out_spec, name):
    s, d = n.shape
    return pl.pallas_call(
        kernel,
        grid=(s // PROJ_TM,),
        in_specs=[pl.BlockSpec((PROJ_TM, d), lambda i: (i, 0)),
                  pl.BlockSpec(w.shape, lambda i: (0, 0))] + extra_specs,
        out_specs=out_spec,
        out_shape=out_shape,
        compiler_params=_cparams(("parallel",)),
        name=name,
    )(n, w, *extra)


def _kidx_kernel(n_ref, w_ref, g_ref, b_ref, ki_ref, wit_ref):
    z = jnp.dot(n_ref[...], w_ref[...], preferred_element_type=F32)
    ki = z[:, :IDX_DIM]
    mu = jnp.mean(ki, axis=-1, keepdims=True)
    var = jnp.mean(jnp.square(ki - mu), axis=-1, keepdims=True)
    y = (ki - mu) * lax.rsqrt(var + EPS)
    ki_ref[...] = (y * g_ref[...] + b_ref[...]).astype(ki_ref.dtype)
    wit_ref[...] = jnp.transpose(z)[IDX_DIM:IDX_DIM + IDX_HEADS, :] * (IDX_HEADS ** -0.5 * IDX_DIM ** -0.5)


def _kidx(n, w, g, b):
    s, d = n.shape
    tm = 512
    return pl.pallas_call(
        _kidx_kernel,
        grid=(s // tm,),
        in_specs=[pl.BlockSpec((tm, d), lambda i: (i, 0)),
                  pl.BlockSpec((d, LANES), lambda i: (0, 0)),
                  pl.BlockSpec((1, IDX_DIM), lambda i: (0, 0)),
                  pl.BlockSpec((1, IDX_DIM), lambda i: (0, 0))],
        out_specs=[pl.BlockSpec((tm, IDX_DIM), lambda i: (i, 0)),
                   pl.BlockSpec((IDX_HEADS, tm), lambda i: (0, i))],
        out_shape=[jax.ShapeDtypeStruct((s, IDX_DIM), BF16),
                   jax.ShapeDtypeStruct((IDX_HEADS, s), F32)],
        compiler_params=_cparams(("parallel",)),
        name="proj_kidx",
    )(n, w, g, b)


def _sgu_kernel(u_ref, v_ref, ws_ref, bt_ref, o_ref):
    tm = u_ref.shape[0]
    r = lax.broadcasted_iota(I32, (CHUNK, CHUNK), 0)
    c = lax.broadcasted_iota(I32, (CHUNK, CHUNK), 1)
    causal = c <= r
    for g in range(SGU_GROUPS):
        cols = slice(g * LANES, (g + 1) * LANES)
        w = jnp.where(causal, ws_ref[g], 0.0).astype(BF16)
        bcol = bt_ref[:, g:g + 1]
        for ch in range(tm // CHUNK):
            rows = slice(ch * CHUNK, (ch + 1) * CHUNK)
            sv = jnp.dot(w, v_ref[rows, cols], preferred_element_type=F32) + bcol
            o_ref[rows, cols] = (u_ref[rows, cols].astype(F32) * sv).astype(o_ref.dtype)


def _sgu(u, v, ws, bt):
    s, wdt = u.shape
    return pl.pallas_call(
        _sgu_kernel,
        grid=(s // SGU_TM,),
        in_specs=[pl.BlockSpec((SGU_TM, wdt), lambda i: (i, 0)),
                  pl.BlockSpec((SGU_TM, wdt), lambda i: (i, 0)),
                  pl.BlockSpec((SGU_GROUPS, CHUNK, CHUNK), lambda i: (0, 0, 0)),
                  pl.BlockSpec((CHUNK, SGU_GROUPS), lambda i: (0, 0))],
        out_specs=pl.BlockSpec((SGU_TM, wdt), lambda i: (i, 0)),
        out_shape=jax.ShapeDtypeStruct((s, wdt), BF16),
        compiler_params=_cparams(("parallel",)),
        name="sgu",
    )(u, v, ws, bt)


def _tile_lanes(x, n):
    return x if n == 1 else jnp.concatenate([x] * n, axis=1)


def _sublane_all(x, op):
    for shift in (4, 2, 1):
        x = op(x, pltpu.roll(x, shift, 0))
    return x


def _count_keys(keys_ref, n_blocks, pred):
    tq = keys_ref.shape[1]

    def body(it, acc):
        off = pl.multiple_of(it * CNT_BLK, CNT_BLK)
        blk = keys_ref[pl.ds(off, CNT_BLK), :].reshape(CNT_BLK // SUBLANES, SUBLANES, tq)
        ind = jnp.where(pred(blk, off), 1, 0)
        return acc + ind.reshape(-1, CNT_ACC, SUBLANES, tq).sum(axis=0)

    acc = lax.fori_loop(0, n_blocks, body, jnp.zeros((CNT_ACC, SUBLANES, tq), I32))
    tot = jnp.sum(acc.sum(axis=0).astype(F32), axis=0, keepdims=True)
    return jnp.broadcast_to(tot, (SUBLANES, tq)).astype(I32)


def _attn_kernel(itab_ref, jtab_ref, qit_ref, wit_ref, ki_ref, q_ref, kt_ref, v_ref, o_ref,
                 keys_ref, r_ref, gmax_ref, thr_ref, m_ref, acc_ref, *, alibi_c, topk, idx_bits):
    i = itab_ref[pl.program_id(0)]
    j = jtab_ref[pl.program_id(0)]
    tq, tk = ATT_TQ, ATT_TK
    hw = HEAD_DIM + AUG
    q0 = i * tq
    n_kv = (q0 + tq + tk - 1) // tk
    n_blk = (q0 + tq + CNT_BLK - 1) // CNT_BLK

    @pl.when(j == 0)
    def _index_and_threshold():
        m_ref[...] = jnp.full(m_ref.shape, NEG, F32)
        acc_ref[...] = jnp.zeros(acc_ref.shape, F32)

        def index_chunk(off, n_sub):
            rows = n_sub * IDX_TK
            r_ref[0:rows, :] = jnp.dot(ki_ref[pl.ds(off, rows), :], qit_ref[0],
                                       preferred_element_type=F32)
            for sub in range(n_sub):
                r0 = sub * IDX_TK
                s_idx = off + r0 + lax.broadcasted_iota(I32, (IDX_TK, LANES), 0)
                for g in range(tq // LANES):
                    lanes = slice(g * LANES, (g + 1) * LANES)
                    t_idx = q0 + g * LANES + lax.broadcasted_iota(I32, (IDX_TK, LANES), 1)
                    acc = jnp.zeros((IDX_TK, LANES), F32)
                    for h in range(IDX_HEADS):
                        rr = r_ref[r0:r0 + IDX_TK, h * tq + g * LANES:h * tq + (g + 1) * LANES]
                        acc = acc + jnp.maximum(rr, 0.0) * wit_ref[h:h + 1, lanes]
                    score = jnp.where(s_idx <= t_idx, acc, -jnp.inf)
                    keys_ref[pl.ds(pl.multiple_of(off + r0, IDX_TK), IDX_TK), lanes] = score
                    gmax_ref[:, lanes] = jnp.maximum(gmax_ref[:, lanes], score)

        def chunk_body(c, carry):
            index_chunk(pl.multiple_of(c * (IDX_SUB * IDX_TK), IDX_SUB * IDX_TK), IDX_SUB)
            return carry

        gmax_ref[...] = jnp.full(gmax_ref.shape, -jnp.inf, F32)
        n_idx = (q0 + tq) // IDX_TK
        lax.fori_loop(0, n_idx // IDX_SUB, chunk_body, 0)
        for rest in range(1, IDX_SUB):
            @pl.when(n_idx % IDX_SUB == rest)
            def _(rest=rest):
                index_chunk(pl.multiple_of((n_idx - rest) * IDX_TK, IDX_TK), rest)

        def fill_body(f, carry):
            off = pl.multiple_of(q0 + tq + f * tq, tq)
            keys_ref[pl.ds(off, tq), :] = jnp.full((tq, tq), -jnp.inf, F32)
            return carry

        lax.fori_loop(0, (n_blk * CNT_BLK - (q0 + tq)) // tq, fill_body, 0)

        t_row = q0 + lax.broadcasted_iota(I32, (SUBLANES, tq), 1)
        kq = jnp.minimum(topk, t_row + 1)

        g = gmax_ref[...].reshape(IDX_TK // SUBLANES, SUBLANES, tq)
        hi0 = _sublane_all(g.max(axis=0), jnp.maximum)
        lo0 = _sublane_all(jnp.where(g == -jnp.inf, jnp.inf, g).min(axis=0), jnp.minimum)

        def bis_cond(st):
            it, _, _, _, _, done = st
            return jnp.logical_and(it < BISECT_CAP, done == 0)

        def bis_body(st):
            it, lo, hi, cnt, fin, _ = st
            mid = 0.5 * lo + 0.5 * hi
            stuck = (mid <= lo) | (mid >= hi)
            cand = jnp.where(stuck, hi, mid)
            tot = _count_keys(keys_ref, n_blk, lambda blk, off: blk >= cand[None])
            take = (tot >= kq) & (fin == 0)
            lo = jnp.where(take, cand, lo)
            cnt = jnp.where(take, tot, cnt)
            hi = jnp.where(take | (fin != 0), hi, cand)
            fin = jnp.where(stuck, 1, fin)
            settled = (cnt == kq) | (fin != 0)
            done = (jnp.min(jnp.where(settled, 1.0, 0.0)) > 0.5).astype(I32)
            return it + 1, lo, hi, cnt, fin, done

        empty = lo0 > hi0
        _, thr, _, cnt, _, _ = lax.while_loop(
            bis_cond, bis_body,
            (jnp.int32(0), lo0, hi0, jnp.where(empty, 0, -1), jnp.where(empty, 1, 0), jnp.int32(0)))
        thr_ref[...] = thr
        cnt = lax.cond(jnp.min(cnt.astype(F32)) < 0.0,
                       lambda: jnp.where(cnt < 0, _count_keys(
                           keys_ref, n_blk, lambda blk, off: blk >= thr[None]), cnt),
                       lambda: cnt)

        @pl.when(jnp.max(jnp.where(cnt > kq, 1.0, 0.0)) > 0.5)
        def _break_ties():
            def row_idx(off):
                return off + lax.broadcasted_iota(I32, (CNT_BLK, tq), 0).reshape(
                    CNT_BLK // SUBLANES, SUBLANES, tq)

            above = _count_keys(keys_ref, n_blk, lambda blk, off: blk > thr[None])
            need = kq - above

            def cut_body(b, p):
                c = p + lax.shift_left(jnp.int32(1), idx_bits - 1 - b)
                below = _count_keys(
                    keys_ref, n_blk, lambda blk, off: (blk == thr[None]) & (row_idx(off) < c[None]))
                return jnp.where(below < need, c, p)

            p = lax.fori_loop(0, idx_bits, cut_body, jnp.zeros((SUBLANES, tq), I32))

            def drop_body(it, carry):
                off = pl.multiple_of(it * CNT_BLK, CNT_BLK)
                blk = keys_ref[pl.ds(off, CNT_BLK), :].reshape(CNT_BLK // SUBLANES, SUBLANES, tq)
                new = jnp.where((blk == thr[None]) & (row_idx(off) > p[None]), -jnp.inf, blk)
                keys_ref[pl.ds(off, CNT_BLK), :] = new.reshape(CNT_BLK, tq)
                return carry

            lax.fori_loop(0, n_blk, drop_body, 0)

    k0 = pl.multiple_of(j * tk, tk)
    nl = tk // LANES
    sel_t = keys_ref[pl.ds(k0, tk), :] >= thr_ref[0:1, :]
    bias = jnp.transpose(jnp.where(sel_t, 0.0, NEG))
    tile_dist = jnp.full((tq, LANES), k0 - q0, I32).astype(F32)
    ones_col = (lax.broadcasted_iota(I32, (tk, LANES), 1) == 0).astype(BF16)

    def logits(h):
        s = bias + jnp.dot(q_ref[:, h * hw:(h + 1) * hw], kt_ref[h * hw:(h + 1) * hw, :],
                           preferred_element_type=F32)
        return s, jnp.max(s, axis=-1, keepdims=True)

    def probs(h, s, s_max):
        shift = tile_dist * alibi_c[h]
        m_prev = m_ref[h]
        m_new = jnp.maximum(m_prev, s_max + shift)
        m_ref[h] = m_new
        p = jnp.exp2(s - _tile_lanes(m_new - shift, nl)).astype(BF16)
        return p, jnp.exp2(m_prev - m_new)

    def accumulate(h, p, alpha):
        v_aug = jnp.concatenate([v_ref[:, h * HEAD_DIM:(h + 1) * HEAD_DIM], ones_col], axis=1)
        acc_ref[h] = _tile_lanes(alpha, 2) * acc_ref[h] + jnp.dot(
            p, v_aug, preferred_element_type=F32)

    st_a, st_b = {}, {}
    for step in range(ATTN_HEADS + 2):
        if 0 <= step - 2 < ATTN_HEADS:
            accumulate(step - 2, *st_b.pop(step - 2))
        if step < ATTN_HEADS:
            st_a[step] = logits(step)
        if 0 <= step - 1 < ATTN_HEADS:
            st_b[step - 1] = probs(step - 1, *st_a.pop(step - 1))

    @pl.when(j == n_kv - 1)
    def _finish():
        for h in range(ATTN_HEADS):
            acc = acc_ref[h]
            o_ref[:, h * HEAD_DIM:(h + 1) * HEAD_DIM] = (
                acc[:, :HEAD_DIM] / acc[:, HEAD_DIM:HEAD_DIM + 1]).astype(o_ref.dtype)


def _alibi_pieces():
    out = []
    for h in range(ATTN_HEADS):
        c = np.float32(LOG2E * 2.0 ** (-8.0 * (h + 1) / ATTN_HEADS))
        c1 = np.float32(c.astype(BF16))
        c2 = np.float32(np.float32(c - c1).astype(BF16))
        c3 = np.float32(np.float32(c - c1 - c2).astype(BF16))
        out.append((float(c1), float(c2), float(c3)))
    return out


def _alibi_tables():
    pieces = np.asarray(_alibi_pieces(), np.float32)
    cpos = np.zeros((ATTN_HEADS, AUG), np.float32)
    cneg = np.zeros((ATTN_HEADS, AUG, 1), np.float32)
    cpos[:, 0:3] = pieces
    cpos[:, 3:6] = pieces
    cneg[:, 6:9, 0] = -pieces
    cneg[:, 9:12, 0] = -pieces
    alibi_c = tuple(float(p[0] + p[1] + p[2]) for p in pieces)
    return jnp.asarray(cpos), jnp.asarray(cneg), alibi_c


def _attention(qit, wit, ki, q_aug, kt_aug, v, alibi_c):
    s = q_aug.shape[0]
    tq, tk = ATT_TQ, ATT_TK
    hw = HEAD_DIM + AUG
    assert s % CNT_BLK == 0 and CNT_BLK % tk == 0 and CNT_BLK % tq == 0 and tq % LANES == 0
    topk = min(TOPK_MAX, s // 4)
    idx_bits = max(1, (s - 1).bit_length())
    assert IDX_TK >= topk and tq % IDX_TK == 0

    steps = [(i, j) for i in range(s // tq) for j in range(((i + 1) * tq + tk - 1) // tk)]
    itab = jnp.asarray(np.array([p[0] for p in steps], np.int32))
    jtab = jnp.asarray(np.array([p[1] for p in steps], np.int32))

    grid_spec = pltpu.PrefetchScalarGridSpec(
        num_scalar_prefetch=2,
        grid=(len(steps),),
        in_specs=[pl.BlockSpec((1, IDX_DIM, IDX_HEADS * tq), lambda t, it, jt: (it[t], 0, 0)),
                  pl.BlockSpec((IDX_HEADS, tq), lambda t, it, jt: (0, it[t])),
                  pl.BlockSpec((s, IDX_DIM), lambda t, it, jt: (0, 0)),
                  pl.BlockSpec((tq, ATTN_HEADS * hw), lambda t, it, jt: (it[t], 0)),
                  pl.BlockSpec((ATTN_HEADS * hw, tk), lambda t, it, jt: (0, jt[t])),
                  pl.BlockSpec((tk, ATTN_WIDTH), lambda t, it, jt: (jt[t], 0))],
        out_specs=pl.BlockSpec((tq, ATTN_WIDTH), lambda t, it, jt: (it[t], 0)),
        scratch_shapes=[pltpu.VMEM((s, tq), F32),
                        pltpu.VMEM((IDX_SUB * IDX_TK, IDX_HEADS * tq), F32),
                        pltpu.VMEM((IDX_TK, tq), F32),
                        pltpu.VMEM((SUBLANES, tq), F32),
                        pltpu.VMEM((ATTN_HEADS, tq, LANES), F32),
                        pltpu.VMEM((ATTN_HEADS, tq, 2 * HEAD_DIM), F32)])
    return pl.pallas_call(
        functools.partial(_attn_kernel, alibi_c=alibi_c, topk=topk, idx_bits=idx_bits),
        grid_spec=grid_spec,
        out_shape=jax.ShapeDtypeStruct((s, ATTN_WIDTH), BF16),
        compiler_params=_cparams(("arbitrary",)),
        name="dsa_attention",
    )(itab, jtab, qit, wit, ki, q_aug, kt_aug, v)


def _merge_kernel(ya_ref, yb_ref, sg_a_ref, sg_b_ref, x_ref, gt_ref, wa_ref, wb_ref, wo_ref, o_ref):
    a = jnp.dot(ya_ref[...], wa_ref[...], preferred_element_type=F32)
    b = jnp.dot(yb_ref[...], wb_ref[...], preferred_element_type=F32)
    merged = sg_a_ref[...].astype(F32) * a + sg_b_ref[...].astype(F32) * b
    o_ref[...] = x_ref[...] + gt_ref[...] * jnp.dot(
        merged.astype(BF16), wo_ref[...], preferred_element_type=F32)


def _merge(ya, yb, sg, x, gt, wa, wb, wo):
    s, d = x.shape
    tm = MERGE_TM
    wdt = ya.shape[1]
    const = lambda i: (0, 0)
    return pl.pallas_call(
        _merge_kernel,
        grid=(s // tm,),
        in_specs=[pl.BlockSpec((tm, wdt), lambda i: (i, 0)),
                  pl.BlockSpec((tm, wdt), lambda i: (i, 0)),
                  pl.BlockSpec((tm, d), lambda i: (i, 0)),
                  pl.BlockSpec((tm, d), lambda i: (i, 1)),
                  pl.BlockSpec((tm, d), lambda i: (i, 0)),
                  pl.BlockSpec((1, d), const),
                  pl.BlockSpec((wdt, d), const),
                  pl.BlockSpec((wdt, d), const),
                  pl.BlockSpec((d, d), const)],
        out_specs=pl.BlockSpec((tm, d), lambda i: (i, 0)),
        out_shape=jax.ShapeDtypeStruct((s, d), F32),
        compiler_params=_cparams(("parallel",)),
        name="merge",
    )(ya, yb, sg, sg, x, gt, wa, wb, wo)


def _pad_cols(w, n):
    return jnp.pad(w, ((0, 0), (0, n - w.shape[1])))


def _layer(x, c, w_ada, b_ada, g_norm1, w1_gate, w1_up, w1_down, g_norm2, w_in, g_sgu, w_spatial,
           b_spatial, g_q, g_k, g_kidx, b_kidx, w_branch_a, w_branch_b, w_out, g_norm3,
           w2_gate, w2_up, w2_down):
    s, d = x.shape

    ada = _ada(c.reshape(d, 1), w_ada, b_ada.reshape(1, -1))
    sh1, sc1, gt1, sh2, sc2, gt2, sh3, sc3, gt3 = [ada[:, k * d:(k + 1) * d] for k in range(N_ADA)]

    def ffn_weights(wg, wu, wd):
        return wg.astype(BF16), wu.astype(BF16), wd.astype(BF16)

    x1 = _ffn(x, g_norm1.reshape(1, d), sh1, sc1, gt1, *ffn_weights(w1_gate, w1_up, w1_down))

    n2 = _normmod(x1, g_norm2.reshape(1, d), sh2, sc2)
    wi_b = w_in.astype(BF16)
    o_u, o_v, o_q = 0, SGU_WIDTH, 2 * SGU_WIDTH
    o_k, o_vv = o_q + ATTN_WIDTH, o_q + 2 * ATTN_WIDTH
    o_qi = o_vv + ATTN_WIDTH
    o_ki = o_qi + IDX_HEADS * IDX_DIM
    o_g = o_ki + IDX_DIM + IDX_HEADS

    ug = _proj(n2, wi_b[:, o_u:o_v], "gelu")
    vn = _proj(n2, wi_b[:, o_v:o_q], "gelu_gnorm", gain=g_sgu.reshape(1, SGU_WIDTH))
    vv = _proj(n2, wi_b[:, o_vv:o_qi], "none")
    sg = _proj(n2, wi_b[:, o_g:], "sigmoid")
    ki, wit = _kidx(n2, _pad_cols(wi_b[:, o_ki:o_g], LANES),
                    g_kidx.reshape(1, IDX_DIM), b_kidx.reshape(1, IDX_DIM))

    cpos, cneg, alibi_c = _alibi_tables()
    hw = HEAD_DIM + AUG
    row = lambda i: (0, 0)
    q_aug = _attn_proj(
        functools.partial(_attn_q_kernel, post_scale=HEAD_DIM ** -0.5 * LOG2E),
        n2, wi_b[:, o_q:o_k], [jnp.tile(g_q, ATTN_HEADS).reshape(1, -1), cpos],
        [pl.BlockSpec((1, ATTN_WIDTH), row), pl.BlockSpec((ATTN_HEADS, AUG), row)],
        jax.ShapeDtypeStruct((s, ATTN_HEADS * hw), BF16),
        pl.BlockSpec((PROJ_TM, ATTN_HEADS * hw), lambda i: (i, 0)), "proj_attn_q")
    kt_aug = _attn_proj(
        _attn_kt_kernel, n2, wi_b[:, o_k:o_vv], [jnp.tile(g_k, ATTN_HEADS).reshape(1, -1), cneg],
        [pl.BlockSpec((1, ATTN_WIDTH), row), pl.BlockSpec((ATTN_HEADS, AUG, 1), lambda i: (0, 0, 0))],
        jax.ShapeDtypeStruct((ATTN_HEADS * hw, s), BF16),
        pl.BlockSpec((ATTN_HEADS * hw, PROJ_TM), lambda i: (0, i)), "proj_attn_kt")
    qit = _attn_proj(
        _idx_q_kernel, n2, wi_b[:, o_qi:o_ki], [], [],
        jax.ShapeDtypeStruct((s // ATT_TQ, IDX_DIM, IDX_HEADS * ATT_TQ), BF16),
        pl.BlockSpec((PROJ_TM // ATT_TQ, IDX_DIM, IDX_HEADS * ATT_TQ), lambda i: (i, 0, 0)), "proj_idx_q")

    y_a = _sgu(ug, vn, w_spatial, jnp.transpose(b_spatial))
    y_b = _attention(qit, wit, ki, q_aug, kt_aug, vv, alibi_c)

    x2 = _merge(y_a, y_b, sg, x1, gt2, w_branch_a.astype(BF16), w_branch_b.astype(BF16),
                w_out.astype(BF16))

    return _ffn(x2, g_norm3.reshape(1, d), sh3, sc3, gt3, *ffn_weights(w2_gate, w2_up, w2_down))


def kernel(x, c, w_ada, b_ada, g_norm1, w1_gate, w1_up, w1_down, g_norm2, w_in, g_sgu, w_spatial,
           b_spatial, g_q, g_k, g_kidx, b_kidx, w_branch_a, w_branch_b, w_out, g_norm3,
           w2_gate, w2_up, w2_down):
    batch, depth = x.shape[0], w_ada.shape[0]
    outs = []
    for b in range(batch):
        xb = x[b]
        for l in range(depth):
            xb = _layer(xb, c[b], w_ada[l], b_ada[l], g_norm1[l], w1_gate[l], w1_up[l], w1_down[l],
                        g_norm2[l], w_in[l], g_sgu[l], w_spatial[l], b_spatial[l], g_q[l], g_k[l],
                        g_kidx[l], b_kidx[l], w_branch_a[l], w_branch_b[l], w_out[l], g_norm3[l],
                        w2_gate[l], w2_up[l], w2_down[l])
        outs.append(xb[None])
    return outs[0] if batch == 1 else jnp.concatenate(outs)
```

```python
import functools

import numpy as np
import jax
import jax.numpy as jnp
from jax import lax
from jax.experimental import pallas as pl
from jax.experimental.pallas import tpu as pltpu

F32 = jnp.float32
BF16 = jnp.bfloat16
I32 = jnp.int32

CHUNK = 128
SGU_GROUPS = 8
SGU_WIDTH = 1024
ATTN_HEADS = 8
HEAD_DIM = 128
ATTN_WIDTH = ATTN_HEADS * HEAD_DIM
IDX_HEADS = 16
IDX_DIM = 64
TOPK_MAX = 256
N_ADA = 9
EPS = 1e-6
NEG = -1e30

LANES = 128
SUBLANES = 8
VMEM_LIMIT = 56 * 1024 * 1024
FFN_VMEM_LIMIT = 60 * 1024 * 1024

FFN_TM = 512
FFN_TF = 1024
PROJ_TM = 1024
PROJ_TN = 1024
SGU_TM = 512
MERGE_TM = 256
ATT_TQ = 256
ATT_TK = 1024
IDX_TK = 256
IDX_SUB = 2
CNT_BLK = 1024
CNT_ACC = 8
BISECT_CAP = 512
AUG = 128
LOG2E = 1.4426950408889634


def _cparams(sem):
    return pltpu.CompilerParams(dimension_semantics=sem, vmem_limit_bytes=VMEM_LIMIT)


def _gelu_tanh(x):
    return 0.5 * x * (1.0 + jnp.tanh(0.7978845608028654 * (x + 0.044715 * (x * x * x))))


def _rms_mod(x, g, sh, sc):
    ms = jnp.mean(x * x, axis=-1, keepdims=True)
    y = (x * lax.rsqrt(ms + EPS)) * g
    return y * (1.0 + sc) + sh


def _ada_kernel(c_ref, w_ref, b_ref, o_ref, sb_ref):
    d = w_ref.shape[0]
    tn = w_ref.shape[1]

    @pl.when(pl.program_id(0) == 0)
    def _():
        cc = c_ref[...]
        sb_ref[...] = jnp.broadcast_to(cc * jax.nn.sigmoid(cc), (d, LANES))

    sb = sb_ref[...]
    for cb in range(tn // LANES):
        cols = slice(cb * LANES, (cb + 1) * LANES)
        prod = (w_ref[:, cols] * sb).reshape(d // SUBLANES, SUBLANES, LANES).sum(axis=0)
        o_ref[:, cols] = prod.sum(axis=0, keepdims=True) + b_ref[:, cols]


def _ada(c_col, w, b):
    d, n = w.shape
    tn = 1024
    return pl.pallas_call(
        _ada_kernel,
        grid=(n // tn,),
        in_specs=[pl.BlockSpec((d, 1), lambda j: (0, 0)),
                  pl.BlockSpec((d, tn), lambda j: (0, j)),
                  pl.BlockSpec((1, tn), lambda j: (0, j))],
        out_specs=pl.BlockSpec((1, tn), lambda j: (0, j)),
        out_shape=jax.ShapeDtypeStruct((1, n), F32),
        scratch_shapes=[pltpu.VMEM((d, LANES), F32)],
        compiler_params=_cparams(("arbitrary",)),
        name="ada",
    )(c_col, w, b)


def _swiglu_chunk(n, wg_ref, wu_ref, wd_ref):
    g = jnp.dot(n, wg_ref[...], preferred_element_type=F32)
    u = jnp.dot(n, wu_ref[...], preferred_element_type=F32)
    h = ((g * jax.nn.sigmoid(g)) * u).astype(BF16)
    return jnp.dot(h, wd_ref[...], preferred_element_type=F32)


def _ffn_kernel(x_ref, g_ref, sh_ref, sc_ref, gt_ref, wg_ref, wu_ref, wd_ref, *rest):
    *tail, o_ref, n_ref = rest
    j = pl.program_id(1)

    @pl.when(j == 0)
    def _():
        n_ref[...] = _rms_mod(x_ref[...], g_ref[...], sh_ref[...], sc_ref[...]).astype(BF16)

    n = n_ref[...]
    d = _swiglu_chunk(n, wg_ref, wu_ref, wd_ref)

    @pl.when(j == 0)
    def _():
        o_ref[...] = d

    @pl.when(j > 0)
    def _():
        o_ref[...] += d

    @pl.when(j == pl.num_programs(1) - 1)
    def _():
        y = o_ref[...]
        if tail:
            y = y + _swiglu_chunk(n, *tail)
        o_ref[...] = x_ref[...] + (0.5 * gt_ref[...]) * y


def _ffn(x, g, sh, sc, gt, wg, wu, wd):
    s, d = x.shape
    f = wg.shape[1]
    tm = min(FFN_TM, s)
    n_main = f // FFN_TF
    f_main = n_main * FFN_TF
    row = pl.BlockSpec((1, d), lambda i, j: (0, 0))
    in_specs = [pl.BlockSpec((tm, d), lambda i, j: (i, 0)), row, row, row, row,
                pl.BlockSpec((d, FFN_TF), lambda i, j: (0, j)),
                pl.BlockSpec((d, FFN_TF), lambda i, j: (0, j)),
                pl.BlockSpec((FFN_TF, d), lambda i, j: (j, 0))]
    args = [x, g, sh, sc, gt, wg, wu, wd]
    if f_main < f:
        assert (f - f_main) % LANES == 0
        args += [wg[:, f_main:], wu[:, f_main:], wd[f_main:]]
        once = dict(pipeline_mode=pl.Buffered(1))
        in_specs += [pl.BlockSpec((d, f - f_main), lambda i, j: (0, 0), **once),
                     pl.BlockSpec((d, f - f_main), lambda i, j: (0, 0), **once),
                     pl.BlockSpec((f - f_main, d), lambda i, j: (0, 0), **once)]
    return pl.pallas_call(
        _ffn_kernel,
        grid=(s // tm, n_main),
        in_specs=in_specs,
        out_specs=pl.BlockSpec((tm, d), lambda i, j: (i, 0)),
        out_shape=jax.ShapeDtypeStruct((s, d), F32),
        scratch_shapes=[pltpu.VMEM((tm, d), BF16)],
        compiler_params=pltpu.CompilerParams(dimension_semantics=("parallel", "arbitrary"),
                                             vmem_limit_bytes=FFN_VMEM_LIMIT),
        name="ffn",
    )(*args)


def _normmod_kernel(x_ref, g_ref, sh_ref, sc_ref, o_ref):
    o_ref[...] = _rms_mod(x_ref[...], g_ref[...], sh_ref[...], sc_ref[...]).astype(o_ref.dtype)


def _normmod(x, g, sh, sc):
    s, d = x.shape
    tm = 512
    row = pl.BlockSpec((1, d), lambda i: (0, 0))
    return pl.pallas_call(
        _normmod_kernel,
        grid=(s // tm,),
        in_specs=[pl.BlockSpec((tm, d), lambda i: (i, 0)), row, row, row],
        out_specs=pl.BlockSpec((tm, d), lambda i: (i, 0)),
        out_shape=jax.ShapeDtypeStruct((s, d), BF16),
        compiler_params=_cparams(("parallel",)),
        name="normmod",
    )(x, g, sh, sc)


def _group_rms(z, gain, post_scale):
    outs = []
    for gidx in range(z.shape[1] // LANES):
        cols = slice(gidx * LANES, (gidx + 1) * LANES)
        zg = z[:, cols]
        ms = jnp.mean(zg * zg, axis=-1, keepdims=True)
        y = (zg * lax.rsqrt(ms + EPS)) * gain[:, cols]
        if post_scale != 1.0:
            y = y * post_scale
        outs.append(y)
    return jnp.concatenate(outs, axis=-1)


def _proj_kernel(n_ref, w_ref, *rest, mode, post_scale):
    o_ref = rest[-1]
    z = jnp.dot(n_ref[...], w_ref[...], preferred_element_type=F32)
    if mode == "gelu":
        out = _gelu_tanh(z)
    elif mode == "gelu_gnorm":
        out = _group_rms(_gelu_tanh(z), rest[0][...], post_scale)
    elif mode == "gnorm":
        out = _group_rms(z, rest[0][...], post_scale)
    elif mode == "sigmoid":
        out = jax.nn.sigmoid(z)
    else:
        out = z
    o_ref[...] = out.astype(o_ref.dtype)


def _proj(n, w, mode, gain=None, post_scale=1.0, out_dtype=BF16):
    s, d = n.shape
    nout = w.shape[1]
    in_specs = [pl.BlockSpec((PROJ_TM, d), lambda i, j: (i, 0)),
                pl.BlockSpec((d, PROJ_TN), lambda i, j: (0, j))]
    args = [n, w]
    if gain is not None:
        in_specs.append(pl.BlockSpec((1, PROJ_TN), lambda i, j: (0, j)))
        args.append(gain)
    return pl.pallas_call(
        functools.partial(_proj_kernel, mode=mode, post_scale=post_scale),
        grid=(s // PROJ_TM, nout // PROJ_TN),
        in_specs=in_specs,
        out_specs=pl.BlockSpec((PROJ_TM, PROJ_TN), lambda i, j: (i, j)),
        out_shape=jax.ShapeDtypeStruct((s, nout), out_dtype),
        compiler_params=_cparams(("parallel", "arbitrary")),
        name="proj_" + mode,
    )(*args)


def _split_hi_lo(pos):
    return (pos >> 7) << 7, pos & (LANES - 1)


def _attn_q_kernel(n_ref, w_ref, gain_ref, cpos_ref, o_ref, *, post_scale):
    tm = n_ref.shape[0]
    y = _group_rms(jnp.dot(n_ref[...], w_ref[...], preferred_element_type=F32),
                   gain_ref[...], post_scale)
    col = lax.broadcasted_iota(I32, (tm, AUG), 1)
    t_hi, t_lo = _split_hi_lo(lax.broadcasted_iota(I32, (tm, AUG), 0) & (ATT_TQ - 1))
    tab = jnp.where((col >= 6) & (col < 9), t_hi, jnp.where((col >= 9) & (col < 12), t_lo, 0)).astype(F32)
    hw = HEAD_DIM + AUG
    for h in range(ATTN_HEADS):
        o_ref[:, h * hw:h * hw + HEAD_DIM] = y[:, h * HEAD_DIM:(h + 1) * HEAD_DIM].astype(o_ref.dtype)
        o_ref[:, h * hw + HEAD_DIM:(h + 1) * hw] = (tab + cpos_ref[h:h + 1, :]).astype(o_ref.dtype)


def _attn_kt_kernel(n_ref, w_ref, gain_ref, cneg_ref, o_ref):
    tm = n_ref.shape[0]
    y = _group_rms(jnp.dot(n_ref[...], w_ref[...], preferred_element_type=F32), gain_ref[...], 1.0)
    yt = jnp.transpose(y)
    row = lax.broadcasted_iota(I32, (AUG, tm), 0)
    pos = pl.program_id(0) * tm + lax.broadcasted_iota(I32, (AUG, tm), 1)
    u_hi, u_lo = _split_hi_lo(pos & (ATT_TK - 1))
    tab = jnp.where(row < 3, u_hi, jnp.where(row < 6, u_lo, 0)).astype(F32)
    hw = HEAD_DIM + AUG
    for h in range(ATTN_HEADS):
        o_ref[h * hw:h * hw + HEAD_DIM, :] = yt[h * HEAD_DIM:(h + 1) * HEAD_DIM, :].astype(o_ref.dtype)
        o_ref[h * hw + HEAD_DIM:(h + 1) * hw, :] = (tab + cneg_ref[h]).astype(o_ref.dtype)


def _idx_q_kernel(n_ref, w_ref, o_ref):
    zt = jnp.transpose(jnp.dot(n_ref[...], w_ref[...], preferred_element_type=F32))
    for qt in range(n_ref.shape[0] // ATT_TQ):
        for h in range(IDX_HEADS):
            o_ref[qt, :, h * ATT_TQ:(h + 1) * ATT_TQ] = zt[
                h * IDX_DIM:(h + 1) * IDX_DIM, qt * ATT_TQ:(qt + 1) * ATT_TQ].astype(o_ref.dtype)


def _attn_proj(kernel, n, w, extra, extra_specs, out_shape, out_spec, name):
    s, d = n.shape
    return pl.pallas_call(
        kernel,
        grid=(s // PROJ_TM,),
        in_specs=[pl.BlockSpec((PROJ_TM, d), lambda i: (i, 0)),
                  pl.BlockSpec(w.shape, lambda i: (0, 0))] + extra_specs,
        out_specs=out_spec,
        out_shape=out_shape,
        compiler_params=_cparams(("parallel",)),
        name=name,
    )(n, w, *extra)


def _kidx_kernel(n_ref, w_ref, g_ref, b_ref, ki_ref, wit_ref):
    z = jnp.dot(n_ref[...], w_ref[...], preferred_element_type=F32)
    ki = z[:, :IDX_DIM]
    mu = jnp.mean(ki, axis=-1, keepdims=True)
    var = jnp.mean(jnp.square(ki - mu), axis=-1, keepdims=True)
    y = (ki - mu) * lax.rsqrt(var + EPS)
    ki_ref[...] = (y * g_ref[...] + b_ref[...]).astype(ki_ref.dtype)
    wit_ref[...] = jnp.transpose(z)[IDX_DIM:IDX_DIM + IDX_HEADS, :] * (IDX_HEADS ** -0.5 * IDX_DIM ** -0.5)


def _kidx(n, w, g, b):
    s, d = n.shape
    tm = 512
    return pl.pallas_call(
        _kidx_kernel,
        grid=(s // tm,),
        in_specs=[pl.BlockSpec((tm, d), lambda i: (i, 0)),
                  pl.BlockSpec((d, LANES), lambda i: (0, 0)),
                  pl.BlockSpec((1, IDX_DIM), lambda i: (0, 0)),
                  pl.BlockSpec((1, IDX_DIM), lambda i: (0, 0))],
        out_specs=[pl.BlockSpec((tm, IDX_DIM), lambda i: (i, 0)),
                   pl.BlockSpec((IDX_HEADS, tm), lambda i: (0, i))],
        out_shape=[jax.ShapeDtypeStruct((s, IDX_DIM), BF16),
                   jax.ShapeDtypeStruct((IDX_HEADS, s), F32)],
        compiler_params=_cparams(("parallel",)),
        name="proj_kidx",
    )(n, w, g, b)


def _sgu_kernel(u_ref, v_ref, ws_ref, bt_ref, o_ref):
    tm = u_ref.shape[0]
    r = lax.broadcasted_iota(I32, (CHUNK, CHUNK), 0)
    c = lax.broadcasted_iota(I32, (CHUNK, CHUNK), 1)
    causal = c <= r
    for g in range(SGU_GROUPS):
        cols = slice(g * LANES, (g + 1) * LANES)
        w = jnp.where(causal, ws_ref[g], 0.0).astype(BF16)
        bcol = bt_ref[:, g:g + 1]
        for ch in range(tm // CHUNK):
            rows = slice(ch * CHUNK, (ch + 1) * CHUNK)
            sv = jnp.dot(w, v_ref[rows, cols], preferred_element_type=F32) + bcol
            o_ref[rows, cols] = (u_ref[rows, cols].astype(F32) * sv).astype(o_ref.dtype)


def _sgu(u, v, ws, bt):
    s, wdt = u.shape
    return pl.pallas_call(
        _sgu_kernel,
        grid=(s // SGU_TM,),
        in_specs=[pl.BlockSpec((SGU_TM, wdt), lambda i: (i, 0)),
                  pl.BlockSpec((SGU_TM, wdt), lambda i: (i, 0)),
                  pl.BlockSpec((SGU_GROUPS, CHUNK, CHUNK), lambda i: (0, 0, 0)),
                  pl.BlockSpec((CHUNK, SGU_GROUPS), lambda i: (0, 0))],
        out_specs=pl.BlockSpec((SGU_TM, wdt), lambda i: (i, 0)),
        out_shape=jax.ShapeDtypeStruct((s, wdt), BF16),
        compiler_params=_cparams(("parallel",)),
        name="sgu",
    )(u, v, ws, bt)


def _tile_lanes(x, n):
    return x if n == 1 else jnp.concatenate([x] * n, axis=1)


def _sublane_all(x, op):
    for shift in (4, 2, 1):
        x = op(x, pltpu.roll(x, shift, 0))
    return x


def _count_keys(keys_ref, n_blocks, pred):
    tq = keys_ref.shape[1]

    def body(it, acc):
        off = pl.multiple_of(it * CNT_BLK, CNT_BLK)
        blk = keys_ref[pl.ds(off, CNT_BLK), :].reshape(CNT_BLK // SUBLANES, SUBLANES, tq)
        ind = jnp.where(pred(blk, off), 1, 0)
        return acc + ind.reshape(-1, CNT_ACC, SUBLANES, tq).sum(axis=0)

    acc = lax.fori_loop(0, n_blocks, body, jnp.zeros((CNT_ACC, SUBLANES, tq), I32))
    tot = jnp.sum(acc.sum(axis=0).astype(F32), axis=0, keepdims=True)
    return jnp.broadcast_to(tot, (SUBLANES, tq)).astype(I32)


def _attn_kernel(itab_ref, jtab_ref, qit_ref, wit_ref, ki_ref, q_ref, kt_ref, v_ref, o_ref,
                 keys_ref, r_ref, gmax_ref, thr_ref, m_ref, acc_ref, *, alibi_c, topk, idx_bits):
    i = itab_ref[pl.program_id(0)]
    j = jtab_ref[pl.program_id(0)]
    tq, tk = ATT_TQ, ATT_TK
    hw = HEAD_DIM + AUG
    q0 = i * tq
    n_kv = (q0 + tq + tk - 1) // tk
    n_blk = (q0 + tq + CNT_BLK - 1) // CNT_BLK

    @pl.when(j == 0)
    def _index_and_threshold():
        m_ref[...] = jnp.full(m_ref.shape, NEG, F32)
        acc_ref[...] = jnp.zeros(acc_ref.shape, F32)

        def index_chunk(off, n_sub):
            rows = n_sub * IDX_TK
            r_ref[0:rows, :] = jnp.dot(ki_ref[pl.ds(off, rows), :], qit_ref[0],
                                       preferred_element_type=F32)
            for sub in range(n_sub):
                r0 = sub * IDX_TK
                s_idx = off + r0 + lax.broadcasted_iota(I32, (IDX_TK, LANES), 0)
                for g in range(tq // LANES):
                    lanes = slice(g * LANES, (g + 1) * LANES)
                    t_idx = q0 + g * LANES + lax.broadcasted_iota(I32, (IDX_TK, LANES), 1)
                    acc = jnp.zeros((IDX_TK, LANES), F32)
                    for h in range(IDX_HEADS):
                        rr = r_ref[r0:r0 + IDX_TK, h * tq + g * LANES:h * tq + (g + 1) * LANES]
                        acc = acc + jnp.maximum(rr, 0.0) * wit_ref[h:h + 1, lanes]
                    score = jnp.where(s_idx <= t_idx, acc, -jnp.inf)
                    keys_ref[pl.ds(pl.multiple_of(off + r0, IDX_TK), IDX_TK), lanes] = score
                    gmax_ref[:, lanes] = jnp.maximum(gmax_ref[:, lanes], score)

        def chunk_body(c, carry):
            index_chunk(pl.multiple_of(c * (IDX_SUB * IDX_TK), IDX_SUB * IDX_TK), IDX_SUB)
            return carry

        gmax_ref[...] = jnp.full(gmax_ref.shape, -jnp.inf, F32)
        n_idx = (q0 + tq) // IDX_TK
        lax.fori_loop(0, n_idx // IDX_SUB, chunk_body, 0)
        for rest in range(1, IDX_SUB):
            @pl.when(n_idx % IDX_SUB == rest)
            def _(rest=rest):
                index_chunk(pl.multiple_of((n_idx - rest) * IDX_TK, IDX_TK), rest)

        def fill_body(f, carry):
            off = pl.multiple_of(q0 + tq + f * tq, tq)
            keys_ref[pl.ds(off, tq), :] = jnp.full((tq, tq), -jnp.inf, F32)
            return carry

        lax.fori_loop(0, (n_blk * CNT_BLK - (q0 + tq)) // tq, fill_body, 0)

        t_row = q0 + lax.broadcasted_iota(I32, (SUBLANES, tq), 1)
        kq = jnp.minimum(topk, t_row + 1)

        g = gmax_ref[...].reshape(IDX_TK // SUBLANES, SUBLANES, tq)
        hi0 = _sublane_all(g.max(axis=0), jnp.maximum)
        lo0 = _sublane_all(jnp.where(g == -jnp.inf, jnp.inf, g).min(axis=0), jnp.minimum)

        def bis_cond(st):
            it, _, _, _, _, done = st
            return jnp.logical_and(it < BISECT_CAP, done == 0)

        def bis_body(st):
            it, lo, hi, cnt, fin, _ = st
            mid = 0.5 * lo + 0.5 * hi
            stuck = (mid <= lo) | (mid >= hi)
            cand = jnp.where(stuck, hi, mid)
            tot = _count_keys(keys_ref, n_blk, lambda blk, off: blk >= cand[None])
            take = (tot >= kq) & (fin == 0)
            lo = jnp.where(take, cand, lo)
            cnt = jnp.where(take, tot, cnt)
            hi = jnp.where(take | (fin != 0), hi, cand)
            fin = jnp.where(stuck, 1, fin)
            settled = (cnt == kq) | (fin != 0)
            done = (jnp.min(jnp.where(settled, 1.0, 0.0)) > 0.5).astype(I32)
            return it + 1, lo, hi, cnt, fin, done

        empty = lo0 > hi0
        _, thr, _, cnt, _, _ = lax.while_loop(
            bis_cond, bis_body,
            (jnp.int32(0), lo0, hi0, jnp.where(empty, 0, -1), jnp.where(empty, 1, 0), jnp.int32(0)))
        thr_ref[...] = thr
        cnt = lax.cond(jnp.min(cnt.astype(F32)) < 0.0,
                       lambda: jnp.where(cnt < 0, _count_keys(
                           keys_ref, n_blk, lambda blk, off: blk >= thr[None]), cnt),
                       lambda: cnt)

        @pl.when(jnp.max(jnp.where(cnt > kq, 1.0, 0.0)) > 0.5)
        def _break_ties():
            def row_idx(off):
                return off + lax.broadcasted_iota(I32, (CNT_BLK, tq), 0).reshape(
                    CNT_BLK // SUBLANES, SUBLANES, tq)

            above = _count_keys(keys_ref, n_blk, lambda blk, off: blk > thr[None])
            need = kq - above

            def cut_body(b, p):
                c = p + lax.shift_left(jnp.int32(1), idx_bits - 1 - b)
                below = _count_keys(
                    keys_ref, n_blk, lambda blk, off: (blk == thr[None]) & (row_idx(off) < c[None]))
                return jnp.where(below < need, c, p)

            p = lax.fori_loop(0, idx_bits, cut_body, jnp.zeros((SUBLANES, tq), I32))

            def drop_body(it, carry):
                off = pl.multiple_of(it * CNT_BLK, CNT_BLK)
                blk = keys_ref[pl.ds(off, CNT_BLK), :].reshape(CNT_BLK // SUBLANES, SUBLANES, tq)
                new = jnp.where((blk == thr[None]) & (row_idx(off) > p[None]), -jnp.inf, blk)
                keys_ref[pl.ds(off, CNT_BLK), :] = new.reshape(CNT_BLK, tq)
                return carry

            lax.fori_loop(0, n_blk, drop_body, 0)

    k0 = pl.multiple_of(j * tk, tk)
    nl = tk // LANES
    sel_t = keys_ref[pl.ds(k0, tk), :] >= thr_ref[0:1, :]
    bias = jnp.transpose(jnp.where(sel_t, 0.0, NEG))
    tile_dist = jnp.full((tq, LANES), k0 - q0, I32).astype(F32)
    ones_col = (lax.broadcasted_iota(I32, (tk, LANES), 1) == 0).astype(BF16)

    def logits(h):
        s = bias + jnp.dot(q_ref[:, h * hw:(h + 1) * hw], kt_ref[h * hw:(h + 1) * hw, :],
                           preferred_element_type=F32)
        return s, jnp.max(s, axis=-1, keepdims=True)

    def probs(h, s, s_max):
        shift = tile_dist * alibi_c[h]
        m_prev = m_ref[h]
        m_new = jnp.maximum(m_prev, s_max + shift)
        m_ref[h] = m_new
        p = jnp.exp2(s - _tile_lanes(m_new - shift, nl)).astype(BF16)
        return p, jnp.exp2(m_prev - m_new)

    def accumulate(h, p, alpha):
        v_aug = jnp.concatenate([v_ref[:, h * HEAD_DIM:(h + 1) * HEAD_DIM], ones_col], axis=1)
        acc_ref[h] = _tile_lanes(alpha, 2) * acc_ref[h] + jnp.dot(
            p, v_aug, preferred_element_type=F32)

    st_a, st_b = {}, {}
    for step in range(ATTN_HEADS + 2):
        if 0 <= step - 2 < ATTN_HEADS:
            accumulate(step - 2, *st_b.pop(step - 2))
        if step < ATTN_HEADS:
            st_a[step] = logits(step)
        if 0 <= step - 1 < ATTN_HEADS:
            st_b[step - 1] = probs(step - 1, *st_a.pop(step - 1))

    @pl.when(j == n_kv - 1)
    def _finish():
        for h in range(ATTN_HEADS):
            acc = acc_ref[h]
            o_ref[:, h * HEAD_DIM:(h + 1) * HEAD_DIM] = (
                acc[:, :HEAD_DIM] / acc[:, HEAD_DIM:HEAD_DIM + 1]).astype(o_ref.dtype)


def _alibi_pieces():
    out = []
    for h in range(ATTN_HEADS):
        c = np.float32(LOG2E * 2.0 ** (-8.0 * (h + 1) / ATTN_HEADS))
        c1 = np.float32(c.astype(BF16))
        c2 = np.float32(np.float32(c - c1).astype(BF16))
        c3 = np.float32(np.float32(c - c1 - c2).astype(BF16))
        out.append((float(c1), float(c2), float(c3)))
    return out


def _alibi_tables():
    pieces = np.asarray(_alibi_pieces(), np.float32)
    cpos = np.zeros((ATTN_HEADS, AUG), np.float32)
    cneg = np.zeros((ATTN_HEADS, AUG, 1), np.float32)
    cpos[:, 0:3] = pieces
    cpos[:, 3:6] = pieces
    cneg[:, 6:9, 0] = -pieces
    cneg[:, 9:12, 0] = -pieces
    alibi_c = tuple(float(p[0] + p[1] + p[2]) for p in pieces)
    return jnp.asarray(cpos), jnp.asarray(cneg), alibi_c


def _attention(qit, wit, ki, q_aug, kt_aug, v, alibi_c):
    s = q_aug.shape[0]
    tq, tk = ATT_TQ, ATT_TK
    hw = HEAD_DIM + AUG
    assert s % CNT_BLK == 0 and CNT_BLK % tk == 0 and CNT_BLK % tq == 0 and tq % LANES == 0
    topk = min(TOPK_MAX, s // 4)
    idx_bits = max(1, (s - 1).bit_length())
    assert IDX_TK >= topk and tq % IDX_TK == 0

    steps = [(i, j) for i in range(s // tq) for j in range(((i + 1) * tq + tk - 1) // tk)]
    itab = jnp.asarray(np.array([p[0] for p in steps], np.int32))
    jtab = jnp.asarray(np.array([p[1] for p in steps], np.int32))

    grid_spec = pltpu.PrefetchScalarGridSpec(
        num_scalar_prefetch=2,
        grid=(len(steps),),
        in_specs=[pl.BlockSpec((1, IDX_DIM, IDX_HEADS * tq), lambda t, it, jt: (it[t], 0, 0)),
                  pl.BlockSpec((IDX_HEADS, tq), lambda t, it, jt: (0, it[t])),
                  pl.BlockSpec((s, IDX_DIM), lambda t, it, jt: (0, 0)),
                  pl.BlockSpec((tq, ATTN_HEADS * hw), lambda t, it, jt: (it[t], 0)),
                  pl.BlockSpec((ATTN_HEADS * hw, tk), lambda t, it, jt: (0, jt[t])),
                  pl.BlockSpec((tk, ATTN_WIDTH), lambda t, it, jt: (jt[t], 0))],
        out_specs=pl.BlockSpec((tq, ATTN_WIDTH), lambda t, it, jt: (it[t], 0)),
        scratch_shapes=[pltpu.VMEM((s, tq), F32),
                        pltpu.VMEM((IDX_SUB * IDX_TK, IDX_HEADS * tq), F32),
                        pltpu.VMEM((IDX_TK, tq), F32),
                        pltpu.VMEM((SUBLANES, tq), F32),
                        pltpu.VMEM((ATTN_HEADS, tq, LANES), F32),
                        pltpu.VMEM((ATTN_HEADS, tq, 2 * HEAD_DIM), F32)])
    return pl.pallas_call(
        functools.partial(_attn_kernel, alibi_c=alibi_c, topk=topk, idx_bits=idx_bits),
        grid_spec=grid_spec,
        out_shape=jax.ShapeDtypeStruct((s, ATTN_WIDTH), BF16),
        compiler_params=_cparams(("arbitrary",)),
        name="dsa_attention",
    )(itab, jtab, qit, wit, ki, q_aug, kt_aug, v)


def _merge_kernel(ya_ref, yb_ref, sg_a_ref, sg_b_ref, x_ref, gt_ref, wa_ref, wb_ref, wo_ref, o_ref):
    a = jnp.dot(ya_ref[...], wa_ref[...], preferred_element_type=F32)
    b = jnp.dot(yb_ref[...], wb_ref[...], preferred_element_type=F32)
    merged = sg_a_ref[...].astype(F32) * a + sg_b_ref[...].astype(F32) * b
    o_ref[...] = x_ref[...] + gt_ref[...] * jnp.dot(
        merged.astype(BF16), wo_ref[...], preferred_element_type=F32)


def _merge(ya, yb, sg, x, gt, wa, wb, wo):
    s, d = x.shape
    tm = MERGE_TM
    wdt = ya.shape[1]
    const = lambda i: (0, 0)
    return pl.pallas_call(
        _merge_kernel,
        grid=(s // tm,),
        in_specs=[pl.BlockSpec((tm, wdt), lambda i: (i, 0)),
                  pl.BlockSpec((tm, wdt), lambda i: (i, 0)),
                  pl.BlockSpec((tm, d), lambda i: (i, 0)),
                  pl.BlockSpec((tm, d), lambda i: (i, 1)),
                  pl.BlockSpec((tm, d), lambda i: (i, 0)),
                  pl.BlockSpec((1, d), const),
                  pl.BlockSpec((wdt, d), const),
                  pl.BlockSpec((wdt, d), const),
                  pl.BlockSpec((d, d), const)],
        out_specs=pl.BlockSpec((tm, d), lambda i: (i, 0)),
        out_shape=jax.ShapeDtypeStruct((s, d), F32),
        compiler_params=_cparams(("parallel",)),
        name="merge",
    )(ya, yb, sg, sg, x, gt, wa, wb, wo)


def _pad_cols(w, n):
    return jnp.pad(w, ((0, 0), (0, n - w.shape[1])))


def _layer(x, c, w_ada, b_ada, g_norm1, w1_gate, w1_up, w1_down, g_norm2, w_in, g_sgu, w_spatial,
           b_spatial, g_q, g_k, g_kidx, b_kidx, w_branch_a, w_branch_b, w_out, g_norm3,
           w2_gate, w2_up, w2_down):
    s, d = x.shape

    ada = _ada(c.reshape(d, 1), w_ada, b_ada.reshape(1, -1))
    sh1, sc1, gt1, sh2, sc2, gt2, sh3, sc3, gt3 = [ada[:, k * d:(k + 1) * d] for k in range(N_ADA)]

    def ffn_weights(wg, wu, wd):
        return wg.astype(BF16), wu.astype(BF16), wd.astype(BF16)

    x1 = _ffn(x, g_norm1.reshape(1, d), sh1, sc1, gt1, *ffn_weights(w1_gate, w1_up, w1_down))

    n2 = _normmod(x1, g_norm2.reshape(1, d), sh2, sc2)
    wi_b = w_in.astype(BF16)
    o_u, o_v, o_q = 0, SGU_WIDTH, 2 * SGU_WIDTH
    o_k, o_vv = o_q + ATTN_WIDTH, o_q + 2 * ATTN_WIDTH
    o_qi = o_vv + ATTN_WIDTH
    o_ki = o_qi + IDX_HEADS * IDX_DIM
    o_g = o_ki + IDX_DIM + IDX_HEADS

    ug = _proj(n2, wi_b[:, o_u:o_v], "gelu")
    vn = _proj(n2, wi_b[:, o_v:o_q], "gelu_gnorm", gain=g_sgu.reshape(1, SGU_WIDTH))
    vv = _proj(n2, wi_b[:, o_vv:o_qi], "none")
    sg = _proj(n2, wi_b[:, o_g:], "sigmoid")
    ki, wit = _kidx(n2, _pad_cols(wi_b[:, o_ki:o_g], LANES),
                    g_kidx.reshape(1, IDX_DIM), b_kidx.reshape(1, IDX_DIM))

    cpos, cneg, alibi_c = _alibi_tables()
    hw = HEAD_DIM + AUG
    row = lambda i: (0, 0)
    q_aug = _attn_proj(
        functools.partial(_attn_q_kernel, post_scale=HEAD_DIM ** -0.5 * LOG2E),
        n2, wi_b[:, o_q:o_k], [jnp.tile(g_q, ATTN_HEADS).reshape(1, -1), cpos],
        [pl.BlockSpec((1, ATTN_WIDTH), row), pl.BlockSpec((ATTN_HEADS, AUG), row)],
        jax.ShapeDtypeStruct((s, ATTN_HEADS * hw), BF16),
        pl.BlockSpec((PROJ_TM, ATTN_HEADS * hw), lambda i: (i, 0)), "proj_attn_q")
    kt_aug = _attn_proj(
        _attn_kt_kernel, n2, wi_b[:, o_k:o_vv], [jnp.tile(g_k, ATTN_HEADS).reshape(1, -1), cneg],
        [pl.BlockSpec((1, ATTN_WIDTH), row), pl.BlockSpec((ATTN_HEADS, AUG, 1), lambda i: (0, 0, 0))],
        jax.ShapeDtypeStruct((ATTN_HEADS * hw, s), BF16),
        pl.BlockSpec((ATTN_HEADS * hw, PROJ_TM), lambda i: (0, i)), "proj_attn_kt")
    qit = _attn_proj(
        _idx_q_kernel, n2, wi_b[:, o_qi:o_ki], [], [],
        jax.ShapeDtypeStruct((s // ATT_TQ, IDX_DIM, IDX_HEADS * ATT_TQ), BF16),
        pl.BlockSpec((PROJ_TM // ATT_TQ, IDX_DIM, IDX_HEADS * ATT_TQ), lambda i: (i, 0, 0)), "proj_idx_q")

    y_a = _sgu(ug, vn, w_spatial, jnp.transpose(b_spatial))
    y_b = _attention(qit, wit, ki, q_aug, kt_aug, vv, alibi_c)

    x2 = _merge(y_a, y_b, sg, x1, gt2, w_branch_a.astype(BF16), w_branch_b.astype(BF16),
                w_out.astype(BF16))

    return _ffn(x2, g_norm3.reshape(1, d), sh3, sc3, gt3, *ffn_weights(w2_gate, w2_up, w2_down))


def kernel(x, c, w_ada, b_ada, g_norm1, w1_gate, w1_up, w1_down, g_norm2, w_in, g_sgu, w_spatial,
           b_spatial, g_q, g_k, g_kidx, b_kidx, w_branch_a, w_branch_b, w_out, g_norm3,
           w2_gate, w2_up, w2_down):
    batch, depth = x.shape[0], w_ada.shape[0]
    outs = []
    for b in range(batch):
        xb = x[b]
        for l in range(depth):
            xb = _layer(xb, c[b], w_ada[l], b_ada[l], g_norm1[l], w1_gate[l], w1_up[l], w1_down[l],
                        g_norm2[l], w_in[l], g_sgu[l], w_spatial[l], b_spatial[l], g_q[l], g_k[l],
                        g_kidx[l], b_kidx[l], w_branch_a[l], w_branch_b[l], w_out[l], g_norm3[l],
                        w2_gate[l], w2_up[l], w2_down[l])
        outs.append(xb[None])
    return outs[0] if batch == 1 else jnp.concatenate(outs)
```

```python
import functools

import numpy as np
import jax
import jax.numpy as jnp
from jax import lax
from jax.experimental import pallas as pl
from jax.experimental.pallas import tpu as pltpu

F32 = jnp.float32
BF16 = jnp.bfloat16
I32 = jnp.int32

CHUNK = 128
SGU_GROUPS = 8
SGU_WIDTH = 1024
ATTN_HEADS = 8
HEAD_DIM = 128
ATTN_WIDTH = ATTN_HEADS * HEAD_DIM
IDX_HEADS = 16
IDX_DIM = 64
TOPK_MAX = 256
N_ADA = 9
EPS = 1e-6
NEG = -1e30

LANES = 128
SUBLANES = 8
VMEM_LIMIT = 56 * 1024 * 1024
FFN_VMEM_LIMIT = 60 * 1024 * 1024

FFN_TM = 512
FFN_TF = 1024
PROJ_TM = 1024
PROJ_TN = 1024
SGU_TM = 512
MERGE_TM = 512
ATT_TQ = 256
ATT_TK = 1024
IDX_TK = 256
IDX_SUB = 4
CNT_BLK = 1024
CNT_ACC = 8
BISECT_CAP = 512
AUG = 128
LOG2E = 1.4426950408889634


def _cparams(sem):
    return pltpu.CompilerParams(dimension_semantics=sem, vmem_limit_bytes=VMEM_LIMIT)


def _gelu_tanh(x):
    return 0.5 * x * (1.0 + jnp.tanh(0.7978845608028654 * (x + 0.044715 * (x * x * x))))


def _rms_mod(x, g, sh, sc):
    ms = jnp.mean(x * x, axis=-1, keepdims=True)
    y = (x * lax.rsqrt(ms + EPS)) * g
    return y * (1.0 + sc) + sh


def _ada_kernel(c_ref, w_ref, b_ref, o_ref, sb_ref):
    d = w_ref.shape[0]
    tn = w_ref.shape[1]

    @pl.when(pl.program_id(0) == 0)
    def _():
        cc = c_ref[...]
        sb_ref[...] = jnp.broadcast_to(cc * jax.nn.sigmoid(cc), (d, LANES))

    sb = sb_ref[...]
    for cb in range(tn // LANES):
        cols = slice(cb * LANES, (cb + 1) * LANES)
        prod = (w_ref[:, cols] * sb).reshape(d // SUBLANES, SUBLANES, LANES).sum(axis=0)
        o_ref[:, cols] = prod.sum(axis=0, keepdims=True) + b_ref[:, cols]


def _ada(c_col, w, b):
    d, n = w.shape
    tn = 1024
    return pl.pallas_call(
        _ada_kernel,
        grid=(n // tn,),
        in_specs=[pl.BlockSpec((d, 1), lambda j: (0, 0)),
                  pl.BlockSpec((d, tn), lambda j: (0, j)),
                  pl.BlockSpec((1, tn), lambda j: (0, j))],
        out_specs=pl.BlockSpec((1, tn), lambda j: (0, j)),
        out_shape=jax.ShapeDtypeStruct((1, n), F32),
        scratch_shapes=[pltpu.VMEM((d, LANES), F32)],
        compiler_params=_cparams(("arbitrary",)),
        name="ada",
    )(c_col, w, b)


def _swiglu_chunk(n, wg_ref, wu_ref, wd_ref):
    g = jnp.dot(n, wg_ref[...], preferred_element_type=F32)
    u = jnp.dot(n, wu_ref[...], preferred_element_type=F32)
    h = ((g * jax.nn.sigmoid(g)) * u).astype(BF16)
    return jnp.dot(h, wd_ref[...], preferred_element_type=F32)


def _ffn_kernel(x_ref, g_ref, sh_ref, sc_ref, gt_ref, wg_ref, wu_ref, wd_ref, *rest):
    *tail, o_ref, n_ref = rest
    j = pl.program_id(1)

    @pl.when(j == 0)
    def _():
        n_ref[...] = _rms_mod(x_ref[...], g_ref[...], sh_ref[...], sc_ref[...]).astype(BF16)

    n = n_ref[...]
    d = _swiglu_chunk(n, wg_ref, wu_ref, wd_ref)

    @pl.when(j == 0)
    def _():
        o_ref[...] = d

    @pl.when(j > 0)
    def _():
        o_ref[...] += d

    @pl.when(j == pl.num_programs(1) - 1)
    def _():
        y = o_ref[...]
        if tail:
            y = y + _swiglu_chunk(n, *tail)
        o_ref[...] = x_ref[...] + (0.5 * gt_ref[...]) * y


def _ffn(x, g, sh, sc, gt, wg, wu, wd):
    s, d = x.shape
    f = wg.shape[1]
    tm = min(FFN_TM, s)
    n_main = f // FFN_TF
    f_main = n_main * FFN_TF
    row = pl.BlockSpec((1, d), lambda i, j: (0, 0))
    in_specs = [pl.BlockSpec((tm, d), lambda i, j: (i, 0)), row, row, row, row,
                pl.BlockSpec((d, FFN_TF), lambda i, j: (0, j)),
                pl.BlockSpec((d, FFN_TF), lambda i, j: (0, j)),
                pl.BlockSpec((FFN_TF, d), lambda i, j: (j, 0))]
    args = [x, g, sh, sc, gt, wg, wu, wd]
    if f_main < f:
        assert (f - f_main) % LANES == 0
        args += [wg[:, f_main:], wu[:, f_main:], wd[f_main:]]
        once = dict(pipeline_mode=pl.Buffered(1))
        in_specs += [pl.BlockSpec((d, f - f_main), lambda i, j: (0, 0), **once),
                     pl.BlockSpec((d, f - f_main), lambda i, j: (0, 0), **once),
                     pl.BlockSpec((f - f_main, d), lambda i, j: (0, 0), **once)]
    return pl.pallas_call(
        _ffn_kernel,
        grid=(s // tm, n_main),
        in_specs=in_specs,
        out_specs=pl.BlockSpec((tm, d), lambda i, j: (i, 0)),
        out_shape=jax.ShapeDtypeStruct((s, d), F32),
        scratch_shapes=[pltpu.VMEM((tm, d), BF16)],
        compiler_params=pltpu.CompilerParams(dimension_semantics=("parallel", "arbitrary"),
                                             vmem_limit_bytes=FFN_VMEM_LIMIT),
        name="ffn",
    )(*args)


def _normmod_kernel(x_ref, g_ref, sh_ref, sc_ref, o_ref):
    o_ref[...] = _rms_mod(x_ref[...], g_ref[...], sh_ref[...], sc_ref[...]).astype(o_ref.dtype)


def _normmod(x, g, sh, sc):
    s, d = x.shape
    tm = 512
    row = pl.BlockSpec((1, d), lambda i: (0, 0))
    return pl.pallas_call(
        _normmod_kernel,
        grid=(s // tm,),
        in_specs=[pl.BlockSpec((tm, d), lambda i: (i, 0)), row, row, row],
        out_specs=pl.BlockSpec((tm, d), lambda i: (i, 0)),
        out_shape=jax.ShapeDtypeStruct((s, d), BF16),
        compiler_params=_cparams(("parallel",)),
        name="normmod",
    )(x, g, sh, sc)


def _group_rms(z, gain, post_scale):
    outs = []
    for gidx in range(z.shape[1] // LANES):
        cols = slice(gidx * LANES, (gidx + 1) * LANES)
        zg = z[:, cols]
        ms = jnp.mean(zg * zg, axis=-1, keepdims=True)
        y = (zg * lax.rsqrt(ms + EPS)) * gain[:, cols]
        if post_scale != 1.0:
            y = y * post_scale
        outs.append(y)
    return jnp.concatenate(outs, axis=-1)


def _proj_kernel(n_ref, w_ref, *rest, mode, post_scale):
    o_ref = rest[-1]
    z = jnp.dot(n_ref[...], w_ref[...], preferred_element_type=F32)
    if mode == "gelu":
        out = _gelu_tanh(z)
    elif mode == "gelu_gnorm":
        out = _group_rms(_gelu_tanh(z), rest[0][...], post_scale)
    elif mode == "gnorm":
        out = _group_rms(z, rest[0][...], post_scale)
    elif mode == "sigmoid":
        out = jax.nn.sigmoid(z)
    else:
        out = z
    o_ref[...] = out.astype(o_ref.dtype)


def _proj(n, w, mode, gain=None, post_scale=1.0, out_dtype=BF16):
    s, d = n.shape
    nout = w.shape[1]
    in_specs = [pl.BlockSpec((PROJ_TM, d), lambda i, j: (i, 0)),
                pl.BlockSpec((d, PROJ_TN), lambda i, j: (0, j))]
    args = [n, w]
    if gain is not None:
        in_specs.append(pl.BlockSpec((1, PROJ_TN), lambda i, j: (0, j)))
        args.append(gain)
    return pl.pallas_call(
        functools.partial(_proj_kernel, mode=mode, post_scale=post_scale),
        grid=(s // PROJ_TM, nout // PROJ_TN),
        in_specs=in_specs,
        out_specs=pl.BlockSpec((PROJ_TM, PROJ_TN), lambda i, j: (i, j)),
        out_shape=jax.ShapeDtypeStruct((s, nout), out_dtype),
        compiler_params=_cparams(("parallel", "arbitrary")),
        name="proj_" + mode,
    )(*args)


def _split_hi_lo(pos):
    return (pos >> 7) << 7, pos & (LANES - 1)


def _attn_q_kernel(n_ref, w_ref, gain_ref, cpos_ref, o_ref, *, post_scale):
    tm = n_ref.shape[0]
    y = _group_rms(jnp.dot(n_ref[...], w_ref[...], preferred_element_type=F32),
                   gain_ref[...], post_scale)
    col = lax.broadcasted_iota(I32, (tm, AUG), 1)
    t_hi, t_lo = _split_hi_lo(lax.broadcasted_iota(I32, (tm, AUG), 0) & (ATT_TQ - 1))
    tab = jnp.where((col >= 6) & (col < 9), t_hi, jnp.where((col >= 9) & (col < 12), t_lo, 0)).astype(F32)
    hw = HEAD_DIM + AUG
    for h in range(ATTN_HEADS):
        o_ref[:, h * hw:h * hw + HEAD_DIM] = y[:, h * HEAD_DIM:(h + 1) * HEAD_DIM].astype(o_ref.dtype)
        o_ref[:, h * hw + HEAD_DIM:(h + 1) * hw] = (tab + cpos_ref[h:h + 1, :]).astype(o_ref.dtype)


def _attn_kt_kernel(n_ref, w_ref, gain_ref, cneg_ref, o_ref):
    tm = n_ref.shape[0]
    y = _group_rms(jnp.dot(n_ref[...], w_ref[...], preferred_element_type=F32), gain_ref[...], 1.0)
    yt = jnp.transpose(y)
    row = lax.broadcasted_iota(I32, (AUG, tm), 0)
    pos = pl.program_id(0) * tm + lax.broadcasted_iota(I32, (AUG, tm), 1)
    u_hi, u_lo = _split_hi_lo(pos & (ATT_TK - 1))
    tab = jnp.where(row < 3, u_hi, jnp.where(row < 6, u_lo, 0)).astype(F32)
    hw = HEAD_DIM + AUG
    for h in range(ATTN_HEADS):
        o_ref[h * hw:h * hw + HEAD_DIM, :] = yt[h * HEAD_DIM:(h + 1) * HEAD_DIM, :].astype(o_ref.dtype)
        o_ref[h * hw + HEAD_DIM:(h + 1) * hw, :] = (tab + cneg_ref[h]).astype(o_ref.dtype)


def _idx_q_kernel(n_ref, w_ref, o_ref):
    zt = jnp.transpose(jnp.dot(n_ref[...], w_ref[...], preferred_element_type=F32))
    for qt in range(n_ref.shape[0] // ATT_TQ):
        for h in range(IDX_HEADS):
            o_ref[qt, :, h * ATT_TQ:(h + 1) * ATT_TQ] = zt[
                h * IDX_DIM:(h + 1) * IDX_DIM, qt * ATT_TQ:(qt + 1) * ATT_TQ].astype(o_ref.dtype)


def _attn_proj(kernel, n, w, extra, extra_specs, out_shape, out_spec, name):
    s, d = n.shape
    return pl.pallas_call(
        kernel,
        grid=(s // PROJ_TM,),
        in_specs=[pl.BlockSpec((PROJ_TM, d), lambda i: (i, 0)),
                  pl.BlockSpec(w.shape, lambda i: (0, 0))] + extra_specs,
        out_specs=out_spec,
        out_shape=out_shape,
        compiler_params=_cparams(("parallel",)),
        name=name,
    )(n, w, *extra)


def _kidx_kernel(n_ref, w_ref, g_ref, b_ref, ki_ref, wit_ref):
    z = jnp.dot(n_ref[...], w_ref[...], preferred_element_type=F32)
    ki = z[:, :IDX_DIM]
    mu = jnp.mean(ki, axis=-1, keepdims=True)
    var = jnp.mean(jnp.square(ki - mu), axis=-1, keepdims=True)
    y = (ki - mu) * lax.rsqrt(var + EPS)
    ki_ref[...] = (y * g_ref[...] + b_ref[...]).astype(ki_ref.dtype)
    wit_ref[...] = jnp.transpose(z)[IDX_DIM:IDX_DIM + IDX_HEADS, :] * (IDX_HEADS ** -0.5 * IDX_DIM ** -0.5)


def _kidx(n, w, g, b):
    s, d = n.shape
    tm = 512
    return pl.pallas_call(
        _kidx_kernel,
        grid=(s // tm,),
        in_specs=[pl.BlockSpec((tm, d), lambda i: (i, 0)),
                  pl.BlockSpec((d, LANES), lambda i: (0, 0)),
                  pl.BlockSpec((1, IDX_DIM), lambda i: (0, 0)),
                  pl.BlockSpec((1, IDX_DIM), lambda i: (0, 0))],
        out_specs=[pl.BlockSpec((tm, IDX_DIM), lambda i: (i, 0)),
                   pl.BlockSpec((IDX_HEADS, tm), lambda i: (0, i))],
        out_shape=[jax.ShapeDtypeStruct((s, IDX_DIM), BF16),
                   jax.ShapeDtypeStruct((IDX_HEADS, s), F32)],
        compiler_params=_cparams(("parallel",)),
        name="proj_kidx",
    )(n, w, g, b)


def _sgu_kernel(u_ref, v_ref, ws_ref, bt_ref, o_ref):
    tm = u_ref.shape[0]
    r = lax.broadcasted_iota(I32, (CHUNK, CHUNK), 0)
    c = lax.broadcasted_iota(I32, (CHUNK, CHUNK), 1)
    causal = c <= r
    for g in range(SGU_GROUPS):
        cols = slice(g * LANES, (g + 1) * LANES)
        w = jnp.where(causal, ws_ref[g], 0.0).astype(BF16)
        bcol = bt_ref[:, g:g + 1]
        for ch in range(tm // CHUNK):
            rows = slice(ch * CHUNK, (ch + 1) * CHUNK)
            sv = jnp.dot(w, v_ref[rows, cols], preferred_element_type=F32) + bcol
            o_ref[rows, cols] = (u_ref[rows, cols].astype(F32) * sv).astype(o_ref.dtype)


def _sgu(u, v, ws, bt):
    s, wdt = u.shape
    return pl.pallas_call(
        _sgu_kernel,
        grid=(s // SGU_TM,),
        in_specs=[pl.BlockSpec((SGU_TM, wdt), lambda i: (i, 0)),
                  pl.BlockSpec((SGU_TM, wdt), lambda i: (i, 0)),
                  pl.BlockSpec((SGU_GROUPS, CHUNK, CHUNK), lambda i: (0, 0, 0)),
                  pl.BlockSpec((CHUNK, SGU_GROUPS), lambda i: (0, 0))],
        out_specs=pl.BlockSpec((SGU_TM, wdt), lambda i: (i, 0)),
        out_shape=jax.ShapeDtypeStruct((s, wdt), BF16),
        compiler_params=_cparams(("parallel",)),
        name="sgu",
    )(u, v, ws, bt)


def _tile_lanes(x, n):
    return x if n == 1 else jnp.concatenate([x] * n, axis=1)


def _sublane_all(x, op):
    for shift in (4, 2, 1):
        x = op(x, pltpu.roll(x, shift, 0))
    return x


def _count_keys(keys_ref, n_blocks, pred):
    tq = keys_ref.shape[1]

    def body(it, acc):
        off = pl.multiple_of(it * CNT_BLK, CNT_BLK)
        blk = keys_ref[pl.ds(off, CNT_BLK), :].reshape(CNT_BLK // SUBLANES, SUBLANES, tq)
        ind = jnp.where(pred(blk, off), 1, 0)
        return acc + ind.reshape(-1, CNT_ACC, SUBLANES, tq).sum(axis=0)

    acc = lax.fori_loop(0, n_blocks, body, jnp.zeros((CNT_ACC, SUBLANES, tq), I32))
    tot = jnp.sum(acc.sum(axis=0).astype(F32), axis=0, keepdims=True)
    return jnp.broadcast_to(tot, (SUBLANES, tq)).astype(I32)


def _attn_kernel(itab_ref, jtab_ref, qit_ref, wit_ref, ki_ref, q_ref, kt_ref, v_ref, o_ref,
                 keys_ref, r_ref, gmax_ref, thr_ref, m_ref, acc_ref, *, alibi_c, topk, idx_bits):
    i = itab_ref[pl.program_id(0)]
    j = jtab_ref[pl.program_id(0)]
    tq, tk = ATT_TQ, ATT_TK
    hw = HEAD_DIM + AUG
    q0 = i * tq
    n_kv = (q0 + tq + tk - 1) // tk
    n_blk = (q0 + tq + CNT_BLK - 1) // CNT_BLK

    @pl.when(j == 0)
    def _index_and_threshold():
        m_ref[...] = jnp.full(m_ref.shape, NEG, F32)
        acc_ref[...] = jnp.zeros(acc_ref.shape, F32)

        def index_chunk(off, n_sub):
            rows = n_sub * IDX_TK
            r_ref[0:rows, :] = jnp.dot(ki_ref[pl.ds(off, rows), :], qit_ref[0],
                                       preferred_element_type=F32)
            for sub in range(n_sub):
                r0 = sub * IDX_TK
                s_idx = off + r0 + lax.broadcasted_iota(I32, (IDX_TK, LANES), 0)
                for g in range(tq // LANES):
                    lanes = slice(g * LANES, (g + 1) * LANES)
                    t_idx = q0 + g * LANES + lax.broadcasted_iota(I32, (IDX_TK, LANES), 1)
                    acc = jnp.zeros((IDX_TK, LANES), F32)
                    for h in range(IDX_HEADS):
                        rr = r_ref[r0:r0 + IDX_TK, h * tq + g * LANES:h * tq + (g + 1) * LANES]
                        acc = acc + jnp.maximum(rr, 0.0) * wit_ref[h:h + 1, lanes]
                    score = jnp.where(s_idx <= t_idx, acc, -jnp.inf)
                    keys_ref[pl.ds(pl.multiple_of(off + r0, IDX_TK), IDX_TK), lanes] = score
                    gmax_ref[:, lanes] = jnp.maximum(gmax_ref[:, lanes], score)

        def chunk_body(c, carry):
            index_chunk(pl.multiple_of(c * (IDX_SUB * IDX_TK), IDX_SUB * IDX_TK), IDX_SUB)
            return carry

        gmax_ref[...] = jnp.full(gmax_ref.shape, -jnp.inf, F32)
        n_idx = (q0 + tq) // IDX_TK
        lax.fori_loop(0, n_idx // IDX_SUB, chunk_body, 0)
        for rest in range(1, IDX_SUB):
            @pl.when(n_idx % IDX_SUB == rest)
            def _(rest=rest):
                index_chunk(pl.multiple_of((n_idx - rest) * IDX_TK, IDX_TK), rest)

        def fill_body(f, carry):
            off = pl.multiple_of(q0 + tq + f * tq, tq)
            keys_ref[pl.ds(off, tq), :] = jnp.full((tq, tq), -jnp.inf, F32)
            return carry

        lax.fori_loop(0, (n_blk * CNT_BLK - (q0 + tq)) // tq, fill_body, 0)

        t_row = q0 + lax.broadcasted_iota(I32, (SUBLANES, tq), 1)
        kq = jnp.minimum(topk, t_row + 1)

        g = gmax_ref[...].reshape(IDX_TK // SUBLANES, SUBLANES, tq)
        hi0 = _sublane_all(g.max(axis=0), jnp.maximum)
        lo0 = _sublane_all(jnp.where(g == -jnp.inf, jnp.inf, g).min(axis=0), jnp.minimum)

        def bis_cond(st):
            it, _, _, _, _, done = st
            return jnp.logical_and(it < BISECT_CAP, done == 0)

        def bis_body(st):
            it, lo, hi, cnt, fin, _ = st
            mid = 0.5 * lo + 0.5 * hi
            stuck = (mid <= lo) | (mid >= hi)
            cand = jnp.where(stuck, hi, mid)
            tot = _count_keys(keys_ref, n_blk, lambda blk, off: blk >= cand[None])
            take = (tot >= kq) & (fin == 0)
            lo = jnp.where(take, cand, lo)
            cnt = jnp.where(take, tot, cnt)
            hi = jnp.where(take | (fin != 0), hi, cand)
            fin = jnp.where(stuck, 1, fin)
            settled = (cnt == kq) | (fin != 0)
            done = (jnp.min(jnp.where(settled, 1.0, 0.0)) > 0.5).astype(I32)
            return it + 1, lo, hi, cnt, fin, done

        empty = lo0 > hi0
        _, thr, _, cnt, _, _ = lax.while_loop(
            bis_cond, bis_body,
            (jnp.int32(0), lo0, hi0, jnp.where(empty, 0, -1), jnp.where(empty, 1, 0), jnp.int32(0)))
        thr_ref[...] = thr
        cnt = lax.cond(jnp.min(cnt.astype(F32)) < 0.0,
                       lambda: jnp.where(cnt < 0, _count_keys(
                           keys_ref, n_blk, lambda blk, off: blk >= thr[None]), cnt),
                       lambda: cnt)

        @pl.when(jnp.max(jnp.where(cnt > kq, 1.0, 0.0)) > 0.5)
        def _break_ties():
            def row_idx(off):
                return off + lax.broadcasted_iota(I32, (CNT_BLK, tq), 0).reshape(
                    CNT_BLK // SUBLANES, SUBLANES, tq)

            above = _count_keys(keys_ref, n_blk, lambda blk, off: blk > thr[None])
            need = kq - above

            def cut_body(b, p):
                c = p + lax.shift_left(jnp.int32(1), idx_bits - 1 - b)
                below = _count_keys(
                    keys_ref, n_blk, lambda blk, off: (blk == thr[None]) & (row_idx(off) < c[None]))
                return jnp.where(below < need, c, p)

            p = lax.fori_loop(0, idx_bits, cut_body, jnp.zeros((SUBLANES, tq), I32))

            def drop_body(it, carry):
                off = pl.multiple_of(it * CNT_BLK, CNT_BLK)
                blk = keys_ref[pl.ds(off, CNT_BLK), :].reshape(CNT_BLK // SUBLANES, SUBLANES, tq)
                new = jnp.where((blk == thr[None]) & (row_idx(off) > p[None]), -jnp.inf, blk)
                keys_ref[pl.ds(off, CNT_BLK), :] = new.reshape(CNT_BLK, tq)
                return carry

            lax.fori_loop(0, n_blk, drop_body, 0)

    k0 = pl.multiple_of(j * tk, tk)
    nl = tk // LANES
    sel_t = keys_ref[pl.ds(k0, tk), :] >= thr_ref[0:1, :]
    bias = jnp.transpose(jnp.where(sel_t, 0.0, NEG))
    tile_dist = jnp.full((tq, LANES), k0 - q0, I32).astype(F32)
    ones_col = (lax.broadcasted_iota(I32, (tk, LANES), 1) == 0).astype(BF16)

    def logits(h):
        s = bias + jnp.dot(q_ref[:, h * hw:(h + 1) * hw], kt_ref[h * hw:(h + 1) * hw, :],
                           preferred_element_type=F32)
        return s, jnp.max(s, axis=-1, keepdims=True)

    def probs(h, s, s_max):
        shift = tile_dist * alibi_c[h]
        m_prev = m_ref[h]
        m_new = jnp.maximum(m_prev, s_max + shift)
        m_ref[h] = m_new
        p = jnp.exp2(s - _tile_lanes(m_new - shift, nl)).astype(BF16)
        return p, jnp.exp2(m_prev - m_new)

    def accumulate(h, p, alpha):
        v_aug = jnp.concatenate([v_ref[:, h * HEAD_DIM:(h + 1) * HEAD_DIM], ones_col], axis=1)
        acc_ref[h] = _tile_lanes(alpha, 2) * acc_ref[h] + jnp.dot(
            p, v_aug, preferred_element_type=F32)

    st_a, st_b = {}, {}
    for step in range(ATTN_HEADS + 2):
        if 0 <= step - 2 < ATTN_HEADS:
            accumulate(step - 2, *st_b.pop(step - 2))
        if step < ATTN_HEADS:
            st_a[step] = logits(step)
        if 0 <= step - 1 < ATTN_HEADS:
            st_b[step - 1] = probs(step - 1, *st_a.pop(step - 1))

    @pl.when(j == n_kv - 1)
    def _finish():
        for h in range(ATTN_HEADS):
            acc = acc_ref[h]
            o_ref[:, h * HEAD_DIM:(h + 1) * HEAD_DIM] = (
                acc[:, :HEAD_DIM] / acc[:, HEAD_DIM:HEAD_DIM + 1]).astype(o_ref.dtype)


def _alibi_pieces():
    out = []
    for h in range(ATTN_HEADS):
        c = np.float32(LOG2E * 2.0 ** (-8.0 * (h + 1) / ATTN_HEADS))
        c1 = np.float32(c.astype(BF16))
        c2 = np.float32(np.float32(c - c1).astype(BF16))
        c3 = np.float32(np.float32(c - c1 - c2).astype(BF16))
        out.append((float(c1), float(c2), float(c3)))
    return out


def _alibi_tables():
    pieces = np.asarray(_alibi_pieces(), np.float32)
    cpos = np.zeros((ATTN_HEADS, AUG), np.float32)
    cneg = np.zeros((ATTN_HEADS, AUG, 1), np.float32)
    cpos[:, 0:3] = pieces
    cpos[:, 3:6] = pieces
    cneg[:, 6:9, 0] = -pieces
    cneg[:, 9:12, 0] = -pieces
    alibi_c = tuple(float(p[0] + p[1] + p[2]) for p in pieces)
    return jnp.asarray(cpos), jnp.asarray(cneg), alibi_c


def _attention(qit, wit, ki, q_aug, kt_aug, v, alibi_c):
    s = q_aug.shape[0]
    tq, tk = ATT_TQ, ATT_TK
    hw = HEAD_DIM + AUG
    assert s % CNT_BLK == 0 and CNT_BLK % tk == 0 and CNT_BLK % tq == 0 and tq % LANES == 0
    topk = min(TOPK_MAX, s // 4)
    idx_bits = max(1, (s - 1).bit_length())
    assert IDX_TK >= topk and tq % IDX_TK == 0

    steps = [(i, j) for i in range(s // tq) for j in range(((i + 1) * tq + tk - 1) // tk)]
    itab = jnp.asarray(np.array([p[0] for p in steps], np.int32))
    jtab = jnp.asarray(np.array([p[1] for p in steps], np.int32))

    grid_spec = pltpu.PrefetchScalarGridSpec(
        num_scalar_prefetch=2,
        grid=(len(steps),),
        in_specs=[pl.BlockSpec((1, IDX_DIM, IDX_HEADS * tq), lambda t, it, jt: (it[t], 0, 0)),
                  pl.BlockSpec((IDX_HEADS, tq), lambda t, it, jt: (0, it[t])),
                  pl.BlockSpec((s, IDX_DIM), lambda t, it, jt: (0, 0)),
                  pl.BlockSpec((tq, ATTN_HEADS * hw), lambda t, it, jt: (it[t], 0)),
                  pl.BlockSpec((ATTN_HEADS * hw, tk), lambda t, it, jt: (0, jt[t])),
                  pl.BlockSpec((tk, ATTN_WIDTH), lambda t, it, jt: (jt[t], 0))],
        out_specs=pl.BlockSpec((tq, ATTN_WIDTH), lambda t, it, jt: (it[t], 0)),
        scratch_shapes=[pltpu.VMEM((s, tq), F32),
                        pltpu.VMEM((IDX_SUB * IDX_TK, IDX_HEADS * tq), F32),
                        pltpu.VMEM((IDX_TK, tq), F32),
                        pltpu.VMEM((SUBLANES, tq), F32),
                        pltpu.VMEM((ATTN_HEADS, tq, LANES), F32),
                        pltpu.VMEM((ATTN_HEADS, tq, 2 * HEAD_DIM), F32)])
    return pl.pallas_call(
        functools.partial(_attn_kernel, alibi_c=alibi_c, topk=topk, idx_bits=idx_bits),
        grid_spec=grid_spec,
        out_shape=jax.ShapeDtypeStruct((s, ATTN_WIDTH), BF16),
        compiler_params=_cparams(("arbitrary",)),
        name="dsa_attention",
    )(itab, jtab, qit, wit, ki, q_aug, kt_aug, v)


def _merge_kernel(ya_ref, yb_ref, sg_a_ref, sg_b_ref, x_ref, gt_ref, wa_ref, wb_ref, wo_ref, o_ref):
    a = jnp.dot(ya_ref[...], wa_ref[...], preferred_element_type=F32)
    b = jnp.dot(yb_ref[...], wb_ref[...], preferred_element_type=F32)
    merged = sg_a_ref[...].astype(F32) * a + sg_b_ref[...].astype(F32) * b
    o_ref[...] = x_ref[...] + gt_ref[...] * jnp.dot(
        merged.astype(BF16), wo_ref[...], preferred_element_type=F32)


def _merge(ya, yb, sg, x, gt, wa, wb, wo):
    s, d = x.shape
    tm = MERGE_TM
    wdt = ya.shape[1]
    const = lambda i: (0, 0)
    return pl.pallas_call(
        _merge_kernel,
        grid=(s // tm,),
        in_specs=[pl.BlockSpec((tm, wdt), lambda i: (i, 0)),
                  pl.BlockSpec((tm, wdt), lambda i: (i, 0)),
                  pl.BlockSpec((tm, d), lambda i: (i, 0)),
                  pl.BlockSpec((tm, d), lambda i: (i, 1)),
                  pl.BlockSpec((tm, d), lambda i: (i, 0)),
                  pl.BlockSpec((1, d), const),
                  pl.BlockSpec((wdt, d), const, pipeline_mode=pl.Buffered(1)),
                  pl.BlockSpec((wdt, d), const, pipeline_mode=pl.Buffered(1)),
                  pl.BlockSpec((d, d), const, pipeline_mode=pl.Buffered(1))],
        out_specs=pl.BlockSpec((tm, d), lambda i: (i, 0)),
        out_shape=jax.ShapeDtypeStruct((s, d), F32),
        compiler_params=_cparams(("parallel",)),
        name="merge",
    )(ya, yb, sg, sg, x, gt, wa, wb, wo)


def _pad_cols(w, n):
    return jnp.pad(w, ((0, 0), (0, n - w.shape[1])))


def _layer(x, c, w_ada, b_ada, g_norm1, w1_gate, w1_up, w1_down, g_norm2, w_in, g_sgu, w_spatial,
           b_spatial, g_q, g_k, g_kidx, b_kidx, w_branch_a, w_branch_b, w_out, g_norm3,
           w2_gate, w2_up, w2_down):
    s, d = x.shape

    ada = _ada(c.reshape(d, 1), w_ada, b_ada.reshape(1, -1))
    sh1, sc1, gt1, sh2, sc2, gt2, sh3, sc3, gt3 = [ada[:, k * d:(k + 1) * d] for k in range(N_ADA)]

    def ffn_weights(wg, wu, wd):
        return wg.astype(BF16), wu.astype(BF16), wd.astype(BF16)

    x1 = _ffn(x, g_norm1.reshape(1, d), sh1, sc1, gt1, *ffn_weights(w1_gate, w1_up, w1_down))

    n2 = _normmod(x1, g_norm2.reshape(1, d), sh2, sc2)
    wi_b = w_in.astype(BF16)
    o_u, o_v, o_q = 0, SGU_WIDTH, 2 * SGU_WIDTH
    o_k, o_vv = o_q + ATTN_WIDTH, o_q + 2 * ATTN_WIDTH
    o_qi = o_vv + ATTN_WIDTH
    o_ki = o_qi + IDX_HEADS * IDX_DIM
    o_g = o_ki + IDX_DIM + IDX_HEADS

    ug = _proj(n2, wi_b[:, o_u:o_v], "gelu")
    vn = _proj(n2, wi_b[:, o_v:o_q], "gelu_gnorm", gain=g_sgu.reshape(1, SGU_WIDTH))
    vv = _proj(n2, wi_b[:, o_vv:o_qi], "none")
    sg = _proj(n2, wi_b[:, o_g:], "sigmoid")
    ki, wit = _kidx(n2, _pad_cols(wi_b[:, o_ki:o_g], LANES),
                    g_kidx.reshape(1, IDX_DIM), b_kidx.reshape(1, IDX_DIM))

    cpos, cneg, alibi_c = _alibi_tables()
    hw = HEAD_DIM + AUG
    row = lambda i: (0, 0)
    q_aug = _attn_proj(
        functools.partial(_attn_q_kernel, post_scale=HEAD_DIM ** -0.5 * LOG2E),
        n2, wi_b[:, o_q:o_k], [jnp.tile(g_q, ATTN_HEADS).reshape(1, -1), cpos],
        [pl.BlockSpec((1, ATTN_WIDTH), row), pl.BlockSpec((ATTN_HEADS, AUG), row)],
        jax.ShapeDtypeStruct((s, ATTN_HEADS * hw), BF16),
        pl.BlockSpec((PROJ_TM, ATTN_HEADS * hw), lambda i: (i, 0)), "proj_attn_q")
    kt_aug = _attn_proj(
        _attn_kt_kernel, n2, wi_b[:, o_k:o_vv], [jnp.tile(g_k, ATTN_HEADS).reshape(1, -1), cneg],
        [pl.BlockSpec((1, ATTN_WIDTH), row), pl.BlockSpec((ATTN_HEADS, AUG, 1), lambda i: (0, 0, 0))],
        jax.ShapeDtypeStruct((ATTN_HEADS * hw, s), BF16),
        pl.BlockSpec((ATTN_HEADS * hw, PROJ_TM), lambda i: (0, i)), "proj_attn_kt")
    qit = _attn_proj(
        _idx_q_kernel, n2, wi_b[:, o_qi:o_ki], [], [],
        jax.ShapeDtypeStruct((s // ATT_TQ, IDX_DIM, IDX_HEADS * ATT_TQ), BF16),
        pl.BlockSpec((PROJ_TM // ATT_TQ, IDX_DIM, IDX_HEADS * ATT_TQ), lambda i: (i, 0, 0)), "proj_idx_q")

    y_a = _sgu(ug, vn, w_spatial, jnp.transpose(b_spatial))
    y_b = _attention(qit, wit, ki, q_aug, kt_aug, vv, alibi_c)

    x2 = _merge(y_a, y_b, sg, x1, gt2, w_branch_a.astype(BF16), w_branch_b.astype(BF16),
                w_out.astype(BF16))

    return _ffn(x2, g_norm3.reshape(1, d), sh3, sc3, gt3, *ffn_weights(w2_gate, w2_up, w2_down))


def kernel(x, c, w_ada, b_ada, g_norm1, w1_gate, w1_up, w1_down, g_norm2, w_in, g_sgu, w_spatial,
           b_spatial, g_q, g_k, g_kidx, b_kidx, w_branch_a, w_branch_b, w_out, g_norm3,
           w2_gate, w2_up, w2_down):
    batch, depth = x.shape[0], w_ada.shape[0]
    outs = []
    for b in range(batch):
        xb = x[b]
        for l in range(depth):
            xb = _layer(xb, c[b], w_ada[l], b_ada[l], g_norm1[l], w1_gate[l], w1_up[l], w1_down[l],
                        g_norm2[l], w_in[l], g_sgu[l], w_spatial[l], b_spatial[l], g_q[l], g_k[l],
                        g_kidx[l], b_kidx[l], w_branch_a[l], w_branch_b[l], w_out[l], g_norm3[l],
                        w2_gate[l], w2_up[l], w2_down[l])
        outs.append(xb[None])
    return outs[0] if batch == 1 else jnp.concatenate(outs)
```

```python
import functools

import numpy as np
import jax
import jax.numpy as jnp
from jax import lax
from jax.experimental import pallas as pl
from jax.experimental.pallas import tpu as pltpu

F32 = jnp.float32
BF16 = jnp.bfloat16
I32 = jnp.int32

CHUNK = 128
SGU_GROUPS = 8
SGU_WIDTH = 1024
ATTN_HEADS = 8
HEAD_DIM = 128
ATTN_WIDTH = ATTN_HEADS * HEAD_DIM
IDX_HEADS = 16
IDX_DIM = 64
TOPK_MAX = 256
N_ADA = 9
EPS = 1e-6
NEG = -1e30

LANES = 128
SUBLANES = 8
VMEM_LIMIT = 56 * 1024 * 1024
FFN_VMEM_LIMIT = 60 * 1024 * 1024

FFN_TM = 512
FFN_TF = 1024
PROJ_TM = 1024
PROJ_TN = 1024
SGU_TM = 512
MERGE_TM = 512
ATT_TQ = 256
ATT_TK = 1024
IDX_TK = 256
IDX_SUB = 4
CNT_BLK = 1024
CNT_ACC = 8
BISECT_CAP = 512
AUG = 128
N_PIECES = 3
LOG2E = 1.4426950408889634


def _cparams(sem):
    return pltpu.CompilerParams(dimension_semantics=sem, vmem_limit_bytes=VMEM_LIMIT)


def _gelu_tanh(x):
    return 0.5 * x * (1.0 + jnp.tanh(0.7978845608028654 * (x + 0.044715 * (x * x * x))))


def _rms_mod(x, g, sh, sc):
    ms = jnp.mean(x * x, axis=-1, keepdims=True)
    y = (x * lax.rsqrt(ms + EPS)) * g
    return y * (1.0 + sc) + sh


def _ada_kernel(c_ref, w_ref, b_ref, o_ref, sb_ref):
    d = w_ref.shape[0]
    tn = w_ref.shape[1]

    @pl.when(pl.program_id(0) == 0)
    def _():
        cc = c_ref[...]
        sb_ref[...] = jnp.broadcast_to(cc * jax.nn.sigmoid(cc), (d, LANES))

    sb = sb_ref[...]
    for cb in range(tn // LANES):
        cols = slice(cb * LANES, (cb + 1) * LANES)
        prod = (w_ref[:, cols] * sb).reshape(d // SUBLANES, SUBLANES, LANES).sum(axis=0)
        o_ref[:, cols] = prod.sum(axis=0, keepdims=True) + b_ref[:, cols]


def _ada(c_col, w, b):
    d, n = w.shape
    tn = 1024
    return pl.pallas_call(
        _ada_kernel,
        grid=(n // tn,),
        in_specs=[pl.BlockSpec((d, 1), lambda j: (0, 0)),
                  pl.BlockSpec((d, tn), lambda j: (0, j)),
                  pl.BlockSpec((1, tn), lambda j: (0, j))],
        out_specs=pl.BlockSpec((1, tn), lambda j: (0, j)),
        out_shape=jax.ShapeDtypeStruct((1, n), F32),
        scratch_shapes=[pltpu.VMEM((d, LANES), F32)],
        compiler_params=_cparams(("arbitrary",)),
        name="ada",
    )(c_col, w, b)


def _swiglu_chunk(n, wg_ref, wu_ref, wd_ref):
    g = jnp.dot(n, wg_ref[...], preferred_element_type=F32)
    u = jnp.dot(n, wu_ref[...], preferred_element_type=F32)
    h = ((g * jax.nn.sigmoid(g)) * u).astype(BF16)
    return jnp.dot(h, wd_ref[...], preferred_element_type=F32)


def _ffn_kernel(x_ref, g_ref, sh_ref, sc_ref, gt_ref, wg_ref, wu_ref, wd_ref, *rest):
    *tail, o_ref, n_ref = rest
    j = pl.program_id(1)

    @pl.when(j == 0)
    def _():
        n_ref[...] = _rms_mod(x_ref[...], g_ref[...], sh_ref[...], sc_ref[...]).astype(BF16)

    n = n_ref[...]
    d = _swiglu_chunk(n, wg_ref, wu_ref, wd_ref)

    @pl.when(j == 0)
    def _():
        o_ref[...] = d

    @pl.when(j > 0)
    def _():
        o_ref[...] += d

    @pl.when(j == pl.num_programs(1) - 1)
    def _():
        y = o_ref[...]
        if tail:
            y = y + _swiglu_chunk(n, *tail)
        o_ref[...] = x_ref[...] + (0.5 * gt_ref[...]) * y


def _ffn(x, g, sh, sc, gt, wg, wu, wd):
    s, d = x.shape
    f = wg.shape[1]
    tm = min(FFN_TM, s)
    n_main = f // FFN_TF
    f_main = n_main * FFN_TF
    row = pl.BlockSpec((1, d), lambda i, j: (0, 0))
    in_specs = [pl.BlockSpec((tm, d), lambda i, j: (i, 0)), row, row, row, row,
                pl.BlockSpec((d, FFN_TF), lambda i, j: (0, j)),
                pl.BlockSpec((d, FFN_TF), lambda i, j: (0, j)),
                pl.BlockSpec((FFN_TF, d), lambda i, j: (j, 0))]
    args = [x, g, sh, sc, gt, wg, wu, wd]
    if f_main < f:
        assert (f - f_main) % LANES == 0
        args += [wg[:, f_main:], wu[:, f_main:], wd[f_main:]]
        once = dict(pipeline_mode=pl.Buffered(1))
        in_specs += [pl.BlockSpec((d, f - f_main), lambda i, j: (0, 0), **once),
                     pl.BlockSpec((d, f - f_main), lambda i, j: (0, 0), **once),
                     pl.BlockSpec((f - f_main, d), lambda i, j: (0, 0), **once)]
    return pl.pallas_call(
        _ffn_kernel,
        grid=(s // tm, n_main),
        in_specs=in_specs,
        out_specs=pl.BlockSpec((tm, d), lambda i, j: (i, 0)),
        out_shape=jax.ShapeDtypeStruct((s, d), F32),
        scratch_shapes=[pltpu.VMEM((tm, d), BF16)],
        compiler_params=pltpu.CompilerParams(dimension_semantics=("parallel", "arbitrary"),
                                             vmem_limit_bytes=FFN_VMEM_LIMIT),
        name="ffn",
    )(*args)


def _normmod_kernel(x_ref, g_ref, sh_ref, sc_ref, o_ref):
    o_ref[...] = _rms_mod(x_ref[...], g_ref[...], sh_ref[...], sc_ref[...]).astype(o_ref.dtype)


def _normmod(x, g, sh, sc):
    s, d = x.shape
    tm = 512
    row = pl.BlockSpec((1, d), lambda i: (0, 0))
    return pl.pallas_call(
        _normmod_kernel,
        grid=(s // tm,),
        in_specs=[pl.BlockSpec((tm, d), lambda i: (i, 0)), row, row, row],
        out_specs=pl.BlockSpec((tm, d), lambda i: (i, 0)),
        out_shape=jax.ShapeDtypeStruct((s, d), BF16),
        compiler_params=_cparams(("parallel",)),
        name="normmod",
    )(x, g, sh, sc)


def _group_rms(z, gain, post_scale):
    outs = []
    for gidx in range(z.shape[1] // LANES):
        cols = slice(gidx * LANES, (gidx + 1) * LANES)
        zg = z[:, cols]
        ms = jnp.mean(zg * zg, axis=-1, keepdims=True)
        y = (zg * lax.rsqrt(ms + EPS)) * gain[:, cols]
        if post_scale != 1.0:
            y = y * post_scale
        outs.append(y)
    return jnp.concatenate(outs, axis=-1)


def _proj_kernel(n_ref, w_ref, *rest, mode):
    o_ref = rest[-1]
    z = jnp.dot(n_ref[...], w_ref[...], preferred_element_type=F32)
    if mode == "gelu":
        out = _gelu_tanh(z)
    elif mode == "gelu_gnorm":
        out = _group_rms(_gelu_tanh(z), rest[0][...], 1.0)
    elif mode == "sigmoid":
        out = jax.nn.sigmoid(z)
    else:
        assert mode == "none"
        out = z
    o_ref[...] = out.astype(o_ref.dtype)


def _proj(n, w, mode, gain=None):
    s, d = n.shape
    nout = w.shape[1]
    in_specs = [pl.BlockSpec((PROJ_TM, d), lambda i, j: (i, 0)),
                pl.BlockSpec((d, PROJ_TN), lambda i, j: (0, j))]
    args = [n, w]
    if gain is not None:
        in_specs.append(pl.BlockSpec((1, PROJ_TN), lambda i, j: (0, j)))
        args.append(gain)
    return pl.pallas_call(
        functools.partial(_proj_kernel, mode=mode),
        grid=(s // PROJ_TM, nout // PROJ_TN),
        in_specs=in_specs,
        out_specs=pl.BlockSpec((PROJ_TM, PROJ_TN), lambda i, j: (i, j)),
        out_shape=jax.ShapeDtypeStruct((s, nout), BF16),
        compiler_params=_cparams(("parallel", "arbitrary")),
        name="proj_" + mode,
    )(*args)


def _split_hi_lo(pos):
    lo = pos & (LANES - 1)
    return pos - lo, lo


def _piece_slot(idx, k):
    return (idx >= k * N_PIECES) & (idx < (k + 1) * N_PIECES)


def _attn_q_kernel(n_ref, w_ref, gain_ref, cpos_ref, o_ref, *, post_scale):
    tm = n_ref.shape[0]
    y = _group_rms(jnp.dot(n_ref[...], w_ref[...], preferred_element_type=F32),
                   gain_ref[...], post_scale)
    col = lax.broadcasted_iota(I32, (tm, AUG), 1)
    t_hi, t_lo = _split_hi_lo(lax.broadcasted_iota(I32, (tm, AUG), 0) & (ATT_TQ - 1))
    tab = jnp.where(_piece_slot(col, 2), t_hi, jnp.where(_piece_slot(col, 3), t_lo, 0)).astype(F32)
    hw = HEAD_DIM + AUG
    for h in range(ATTN_HEADS):
        o_ref[:, h * hw:h * hw + HEAD_DIM] = y[:, h * HEAD_DIM:(h + 1) * HEAD_DIM].astype(o_ref.dtype)
        o_ref[:, h * hw + HEAD_DIM:(h + 1) * hw] = (tab + cpos_ref[h:h + 1, :]).astype(o_ref.dtype)


def _attn_kt_kernel(n_ref, w_ref, gain_ref, cneg_ref, o_ref):
    tm = n_ref.shape[0]
    y = _group_rms(jnp.dot(n_ref[...], w_ref[...], preferred_element_type=F32), gain_ref[...], 1.0)
    yt = jnp.transpose(y)
    row = lax.broadcasted_iota(I32, (AUG, tm), 0)
    pos = pl.program_id(0) * tm + lax.broadcasted_iota(I32, (AUG, tm), 1)
    u_hi, u_lo = _split_hi_lo(pos & (ATT_TK - 1))
    tab = jnp.where(_piece_slot(row, 0), u_hi, jnp.where(_piece_slot(row, 1), u_lo, 0)).astype(F32)
    hw = HEAD_DIM + AUG
    for h in range(ATTN_HEADS):
        o_ref[h * hw:h * hw + HEAD_DIM, :] = yt[h * HEAD_DIM:(h + 1) * HEAD_DIM, :].astype(o_ref.dtype)
        o_ref[h * hw + HEAD_DIM:(h + 1) * hw, :] = (tab + cneg_ref[h]).astype(o_ref.dtype)


def _idx_q_kernel(n_ref, w_ref, o_ref):
    zt = jnp.transpose(jnp.dot(n_ref[...], w_ref[...], preferred_element_type=F32))
    for qt in range(n_ref.shape[0] // ATT_TQ):
        for h in range(IDX_HEADS):
            o_ref[qt, :, h * ATT_TQ:(h + 1) * ATT_TQ] = zt[
                h * IDX_DIM:(h + 1) * IDX_DIM, qt * ATT_TQ:(qt + 1) * ATT_TQ].astype(o_ref.dtype)


def _attn_proj(kernel, n, w, extra, extra_specs, out_shape, out_spec, name):
    s, d = n.shape
    return pl.pallas_call(
        kernel,
        grid=(s // PROJ_TM,),
        in_specs=[pl.BlockSpec((PROJ_TM, d), lambda i: (i, 0)),
                  pl.BlockSpec(w.shape, lambda i: (0, 0))] + extra_specs,
        out_specs=out_spec,
        out_shape=out_shape,
        compiler_params=_cparams(("parallel",)),
        name=name,
    )(n, w, *extra)


def _kidx_kernel(n_ref, w_ref, g_ref, b_ref, ki_ref, wit_ref):
    z = jnp.dot(n_ref[...], w_ref[...], preferred_element_type=F32)
    ki = z[:, :IDX_DIM]
    mu = jnp.mean(ki, axis=-1, keepdims=True)
    var = jnp.mean(jnp.square(ki - mu), axis=-1, keepdims=True)
    y = (ki - mu) * lax.rsqrt(var + EPS)
    ki_ref[...] = (y * g_ref[...] + b_ref[...]).astype(ki_ref.dtype)
    wit_ref[...] = jnp.transpose(z)[IDX_DIM:IDX_DIM + IDX_HEADS, :] * (IDX_HEADS ** -0.5 * IDX_DIM ** -0.5)


def _kidx(n, w, g, b):
    s, d = n.shape
    tm = 512
    return pl.pallas_call(
        _kidx_kernel,
        grid=(s // tm,),
        in_specs=[pl.BlockSpec((tm, d), lambda i: (i, 0)),
                  pl.BlockSpec((d, LANES), lambda i: (0, 0)),
                  pl.BlockSpec((1, IDX_DIM), lambda i: (0, 0)),
                  pl.BlockSpec((1, IDX_DIM), lambda i: (0, 0))],
        out_specs=[pl.BlockSpec((tm, IDX_DIM), lambda i: (i, 0)),
                   pl.BlockSpec((IDX_HEADS, tm), lambda i: (0, i))],
        out_shape=[jax.ShapeDtypeStruct((s, IDX_DIM), BF16),
                   jax.ShapeDtypeStruct((IDX_HEADS, s), F32)],
        compiler_params=_cparams(("parallel",)),
        name="proj_kidx",
    )(n, w, g, b)


def _sgu_kernel(u_ref, v_ref, ws_ref, bt_ref, o_ref):
    tm = u_ref.shape[0]
    r = lax.broadcasted_iota(I32, (CHUNK, CHUNK), 0)
    c = lax.broadcasted_iota(I32, (CHUNK, CHUNK), 1)
    causal = c <= r
    for g in range(SGU_GROUPS):
        cols = slice(g * LANES, (g + 1) * LANES)
        w = jnp.where(causal, ws_ref[g], 0.0).astype(BF16)
        bcol = bt_ref[:, g:g + 1]
        for ch in range(tm // CHUNK):
            rows = slice(ch * CHUNK, (ch + 1) * CHUNK)
            sv = jnp.dot(w, v_ref[rows, cols], preferred_element_type=F32) + bcol
            o_ref[rows, cols] = (u_ref[rows, cols].astype(F32) * sv).astype(o_ref.dtype)


def _sgu(u, v, ws, bt):
    s, wdt = u.shape
    return pl.pallas_call(
        _sgu_kernel,
        grid=(s // SGU_TM,),
        in_specs=[pl.BlockSpec((SGU_TM, wdt), lambda i: (i, 0)),
                  pl.BlockSpec((SGU_TM, wdt), lambda i: (i, 0)),
                  pl.BlockSpec((SGU_GROUPS, CHUNK, CHUNK), lambda i: (0, 0, 0)),
                  pl.BlockSpec((CHUNK, SGU_GROUPS), lambda i: (0, 0))],
        out_specs=pl.BlockSpec((SGU_TM, wdt), lambda i: (i, 0)),
        out_shape=jax.ShapeDtypeStruct((s, wdt), BF16),
        compiler_params=_cparams(("parallel",)),
        name="sgu",
    )(u, v, ws, bt)


def _tile_lanes(x, n):
    return x if n == 1 else jnp.concatenate([x] * n, axis=1)


def _sublane_all(x, op):
    shift = SUBLANES // 2
    while shift:
        x = op(x, pltpu.roll(x, shift, 0))
        shift //= 2
    return x


def _count_keys(score_ref, n_blocks, pred):
    tq = score_ref.shape[1]

    def body(it, acc):
        off = pl.multiple_of(it * CNT_BLK, CNT_BLK)
        blk = score_ref[pl.ds(off, CNT_BLK), :].reshape(CNT_BLK // SUBLANES, SUBLANES, tq)
        ind = jnp.where(pred(blk, off), 1, 0)
        return acc + ind.reshape(-1, CNT_ACC, SUBLANES, tq).sum(axis=0)

    acc = lax.fori_loop(0, n_blocks, body, jnp.zeros((CNT_ACC, SUBLANES, tq), I32))
    tot = jnp.sum(acc.sum(axis=0).astype(F32), axis=0, keepdims=True)
    return jnp.broadcast_to(tot, (SUBLANES, tq)).astype(I32)


def _attn_kernel(itab_ref, jtab_ref, qit_ref, wit_ref, ki_ref, q_ref, kt_ref, v_ref, o_ref,
                 score_ref, r_ref, gmax_ref, thr_ref, m_ref, acc_ref, *, alibi_c, topk, idx_bits):
    i = itab_ref[pl.program_id(0)]
    j = jtab_ref[pl.program_id(0)]
    tq, tk = ATT_TQ, ATT_TK
    hw = HEAD_DIM + AUG
    q0 = i * tq
    n_kv = (q0 + tq + tk - 1) // tk
    n_blk = (q0 + tq + CNT_BLK - 1) // CNT_BLK

    @pl.when(j == 0)
    def _index_and_threshold():
        m_ref[...] = jnp.full(m_ref.shape, NEG, F32)
        acc_ref[...] = jnp.zeros(acc_ref.shape, F32)

        def index_chunk(off, n_sub):
            rows = n_sub * IDX_TK
            r_ref[0:rows, :] = jnp.dot(ki_ref[pl.ds(off, rows), :], qit_ref[0],
                                       preferred_element_type=F32)
            for sub in range(n_sub):
                r0 = sub * IDX_TK
                s_idx = off + r0 + lax.broadcasted_iota(I32, (IDX_TK, LANES), 0)
                for g in range(tq // LANES):
                    lanes = slice(g * LANES, (g + 1) * LANES)
                    t_idx = q0 + g * LANES + lax.broadcasted_iota(I32, (IDX_TK, LANES), 1)
                    acc = jnp.zeros((IDX_TK, LANES), F32)
                    for h in range(IDX_HEADS):
                        rr = r_ref[r0:r0 + IDX_TK, h * tq + g * LANES:h * tq + (g + 1) * LANES]
                        acc = acc + jnp.maximum(rr, 0.0) * wit_ref[h:h + 1, lanes]
                    score = jnp.where(s_idx <= t_idx, acc, -jnp.inf)
                    score_ref[pl.ds(pl.multiple_of(off + r0, IDX_TK), IDX_TK), lanes] = score
                    gmax_ref[:, lanes] = jnp.maximum(gmax_ref[:, lanes], score)

        def chunk_body(c, carry):
            index_chunk(pl.multiple_of(c * (IDX_SUB * IDX_TK), IDX_SUB * IDX_TK), IDX_SUB)
            return carry

        gmax_ref[...] = jnp.full(gmax_ref.shape, -jnp.inf, F32)
        n_idx = (q0 + tq) // IDX_TK
        lax.fori_loop(0, n_idx // IDX_SUB, chunk_body, 0)
        for rest in range(1, IDX_SUB):
            @pl.when(n_idx % IDX_SUB == rest)
            def _(rest=rest):
                index_chunk(pl.multiple_of((n_idx - rest) * IDX_TK, IDX_TK), rest)

        def fill_body(f, carry):
            off = pl.multiple_of(q0 + tq + f * tq, tq)
            score_ref[pl.ds(off, tq), :] = jnp.full((tq, tq), -jnp.inf, F32)
            return carry

        lax.fori_loop(0, (n_blk * CNT_BLK - (q0 + tq)) // tq, fill_body, 0)

        t_row = q0 + lax.broadcasted_iota(I32, (SUBLANES, tq), 1)
        kq = jnp.minimum(topk, t_row + 1)

        g = gmax_ref[...].reshape(IDX_TK // SUBLANES, SUBLANES, tq)
        hi0 = _sublane_all(g.max(axis=0), jnp.maximum)
        lo0 = _sublane_all(jnp.where(g == -jnp.inf, jnp.inf, g).min(axis=0), jnp.minimum)

        def bis_cond(st):
            it, _, _, _, _, done = st
            return jnp.logical_and(it < BISECT_CAP, done == 0)

        def bis_body(st):
            it, lo, hi, cnt, fin, _ = st
            mid = 0.5 * lo + 0.5 * hi
            stuck = (mid <= lo) | (mid >= hi)
            cand = jnp.where(stuck, hi, mid)
            tot = _count_keys(score_ref, n_blk, lambda blk, off: blk >= cand[None])
            take = (tot >= kq) & (fin == 0)
            lo = jnp.where(take, cand, lo)
            cnt = jnp.where(take, tot, cnt)
            hi = jnp.where(take | (fin != 0), hi, cand)
            fin = jnp.where(stuck, 1, fin)
            settled = (cnt == kq) | (fin != 0)
            done = (jnp.min(jnp.where(settled, 1.0, 0.0)) > 0.5).astype(I32)
            return it + 1, lo, hi, cnt, fin, done

        empty = lo0 > hi0
        _, thr, _, cnt, _, _ = lax.while_loop(
            bis_cond, bis_body,
            (jnp.int32(0), lo0, hi0, jnp.where(empty, 0, -1), jnp.where(empty, 1, 0), jnp.int32(0)))
        thr_ref[...] = thr
        cnt = lax.cond(jnp.min(cnt.astype(F32)) < 0.0,
                       lambda: jnp.where(cnt < 0, _count_keys(
                           score_ref, n_blk, lambda blk, off: blk >= thr[None]), cnt),
                       lambda: cnt)

        @pl.when(jnp.max(jnp.where(cnt > kq, 1.0, 0.0)) > 0.5)
        def _break_ties():
            def row_idx(off):
                return off + lax.broadcasted_iota(I32, (CNT_BLK, tq), 0).reshape(
                    CNT_BLK // SUBLANES, SUBLANES, tq)

            above = _count_keys(score_ref, n_blk, lambda blk, off: blk > thr[None])
            need = kq - above

            def cut_body(b, p):
                c = p + lax.shift_left(jnp.int32(1), idx_bits - 1 - b)
                below = _count_keys(
                    score_ref, n_blk, lambda blk, off: (blk == thr[None]) & (row_idx(off) < c[None]))
                return jnp.where(below < need, c, p)

            p = lax.fori_loop(0, idx_bits, cut_body, jnp.zeros((SUBLANES, tq), I32))

            def drop_body(it, carry):
                off = pl.multiple_of(it * CNT_BLK, CNT_BLK)
                blk = score_ref[pl.ds(off, CNT_BLK), :].reshape(CNT_BLK // SUBLANES, SUBLANES, tq)
                new = jnp.where((blk == thr[None]) & (row_idx(off) > p[None]), -jnp.inf, blk)
                score_ref[pl.ds(off, CNT_BLK), :] = new.reshape(CNT_BLK, tq)
                return carry

            lax.fori_loop(0, n_blk, drop_body, 0)

    k0 = pl.multiple_of(j * tk, tk)
    nl = tk // LANES
    sel_t = score_ref[pl.ds(k0, tk), :] >= thr_ref[0:1, :]
    bias = jnp.transpose(jnp.where(sel_t, 0.0, NEG))
    tile_dist = jnp.full((tq, LANES), k0 - q0, I32).astype(F32)
    ones_col = (lax.broadcasted_iota(I32, (tk, LANES), 1) == 0).astype(BF16)

    def logits(h):
        s = bias + jnp.dot(q_ref[:, h * hw:(h + 1) * hw], kt_ref[h * hw:(h + 1) * hw, :],
                           preferred_element_type=F32)
        return s, jnp.max(s, axis=-1, keepdims=True)

    def probs(h, s, s_max):
        shift = tile_dist * alibi_c[h]
        m_prev = m_ref[h]
        m_new = jnp.maximum(m_prev, s_max + shift)
        m_ref[h] = m_new
        p = jnp.exp2(s - _tile_lanes(m_new - shift, nl)).astype(BF16)
        return p, jnp.exp2(m_prev - m_new)

    def accumulate(h, p, alpha):
        v_aug = jnp.concatenate([v_ref[:, h * HEAD_DIM:(h + 1) * HEAD_DIM], ones_col], axis=1)
        acc_ref[h] = _tile_lanes(alpha, 2) * acc_ref[h] + jnp.dot(
            p, v_aug, preferred_element_type=F32)

    st_a, st_b = {}, {}
    for step in range(ATTN_HEADS + 2):
        if 0 <= step - 2 < ATTN_HEADS:
            accumulate(step - 2, *st_b.pop(step - 2))
        if step < ATTN_HEADS:
            st_a[step] = logits(step)
        if 0 <= step - 1 < ATTN_HEADS:
            st_b[step - 1] = probs(step - 1, *st_a.pop(step - 1))

    @pl.when(j == n_kv - 1)
    def _finish():
        for h in range(ATTN_HEADS):
            acc = acc_ref[h]
            o_ref[:, h * HEAD_DIM:(h + 1) * HEAD_DIM] = (
                acc[:, :HEAD_DIM] / acc[:, HEAD_DIM:HEAD_DIM + 1]).astype(o_ref.dtype)


def _alibi_pieces():
    out = []
    for h in range(ATTN_HEADS):
        rest = np.float32(LOG2E * 2.0 ** (-8.0 * (h + 1) / ATTN_HEADS))
        pieces = []
        for _ in range(N_PIECES):
            piece = np.float32(rest.astype(BF16))
            pieces.append(piece)
            rest = np.float32(rest - piece)
        out.append(pieces)
    return np.asarray(out, np.float32)


def _alibi_tables():
    pieces = _alibi_pieces()
    n = N_PIECES
    assert 4 * n <= AUG
    cpos = np.zeros((ATTN_HEADS, AUG), np.float32)
    cneg = np.zeros((ATTN_HEADS, AUG, 1), np.float32)
    cpos[:, 0 * n:1 * n] = pieces
    cpos[:, 1 * n:2 * n] = pieces
    cneg[:, 2 * n:3 * n, 0] = -pieces
    cneg[:, 3 * n:4 * n, 0] = -pieces
    alibi_c = tuple(float(p.sum(dtype=np.float32)) for p in pieces)
    return jnp.asarray(cpos), jnp.asarray(cneg), alibi_c


def _attention(qit, wit, ki, q_aug, kt_aug, v, alibi_c):
    s = q_aug.shape[0]
    tq, tk = ATT_TQ, ATT_TK
    hw = HEAD_DIM + AUG
    assert s % CNT_BLK == 0 and CNT_BLK % tk == 0 and CNT_BLK % tq == 0 and tq % LANES == 0
    assert tq & (tq - 1) == 0 and tk & (tk - 1) == 0 and PROJ_TM % tq == 0
    topk = min(TOPK_MAX, s // 4)
    idx_bits = max(1, (s - 1).bit_length())
    assert IDX_TK >= topk and tq % IDX_TK == 0

    steps = [(i, j) for i in range(s // tq) for j in range(((i + 1) * tq + tk - 1) // tk)]
    itab = jnp.asarray(np.array([p[0] for p in steps], np.int32))
    jtab = jnp.asarray(np.array([p[1] for p in steps], np.int32))

    grid_spec = pltpu.PrefetchScalarGridSpec(
        num_scalar_prefetch=2,
        grid=(len(steps),),
        in_specs=[pl.BlockSpec((1, IDX_DIM, IDX_HEADS * tq), lambda t, it, jt: (it[t], 0, 0)),
                  pl.BlockSpec((IDX_HEADS, tq), lambda t, it, jt: (0, it[t])),
                  pl.BlockSpec((s, IDX_DIM), lambda t, it, jt: (0, 0)),
                  pl.BlockSpec((tq, ATTN_HEADS * hw), lambda t, it, jt: (it[t], 0)),
                  pl.BlockSpec((ATTN_HEADS * hw, tk), lambda t, it, jt: (0, jt[t])),
                  pl.BlockSpec((tk, ATTN_WIDTH), lambda t, it, jt: (jt[t], 0))],
        out_specs=pl.BlockSpec((tq, ATTN_WIDTH), lambda t, it, jt: (it[t], 0)),
        scratch_shapes=[pltpu.VMEM((s, tq), F32),
                        pltpu.VMEM((IDX_SUB * IDX_TK, IDX_HEADS * tq), F32),
                        pltpu.VMEM((IDX_TK, tq), F32),
                        pltpu.VMEM((SUBLANES, tq), F32),
                        pltpu.VMEM((ATTN_HEADS, tq, LANES), F32),
                        pltpu.VMEM((ATTN_HEADS, tq, 2 * HEAD_DIM), F32)])
    return pl.pallas_call(
        functools.partial(_attn_kernel, alibi_c=alibi_c, topk=topk, idx_bits=idx_bits),
        grid_spec=grid_spec,
        out_shape=jax.ShapeDtypeStruct((s, ATTN_WIDTH), BF16),
        compiler_params=_cparams(("arbitrary",)),
        name="dsa_attention",
    )(itab, jtab, qit, wit, ki, q_aug, kt_aug, v)


def _merge_kernel(ya_ref, yb_ref, sg_a_ref, sg_b_ref, x_ref, gt_ref, wa_ref, wb_ref, wo_ref, o_ref):
    a = jnp.dot(ya_ref[...], wa_ref[...], preferred_element_type=F32)
    b = jnp.dot(yb_ref[...], wb_ref[...], preferred_element_type=F32)
    merged = sg_a_ref[...].astype(F32) * a + sg_b_ref[...].astype(F32) * b
    o_ref[...] = x_ref[...] + gt_ref[...] * jnp.dot(
        merged.astype(BF16), wo_ref[...], preferred_element_type=F32)


def _merge(ya, yb, sg, x, gt, wa, wb, wo):
    s, d = x.shape
    tm = MERGE_TM
    wdt = ya.shape[1]
    const = lambda i: (0, 0)
    return pl.pallas_call(
        _merge_kernel,
        grid=(s // tm,),
        in_specs=[pl.BlockSpec((tm, wdt), lambda i: (i, 0)),
                  pl.BlockSpec((tm, wdt), lambda i: (i, 0)),
                  pl.BlockSpec((tm, d), lambda i: (i, 0)),
                  pl.BlockSpec((tm, d), lambda i: (i, 1)),
                  pl.BlockSpec((tm, d), lambda i: (i, 0)),
                  pl.BlockSpec((1, d), const),
                  pl.BlockSpec((wdt, d), const, pipeline_mode=pl.Buffered(1)),
                  pl.BlockSpec((wdt, d), const, pipeline_mode=pl.Buffered(1)),
                  pl.BlockSpec((d, d), const, pipeline_mode=pl.Buffered(1))],
        out_specs=pl.BlockSpec((tm, d), lambda i: (i, 0)),
        out_shape=jax.ShapeDtypeStruct((s, d), F32),
        compiler_params=_cparams(("parallel",)),
        name="merge",
    )(ya, yb, sg, sg, x, gt, wa, wb, wo)


def _pad_cols(w, n):
    return jnp.pad(w, ((0, 0), (0, n - w.shape[1])))


def _layer(x, c, w_ada, b_ada, g_norm1, w1_gate, w1_up, w1_down, g_norm2, w_in, g_sgu, w_spatial,
           b_spatial, g_q, g_k, g_kidx, b_kidx, w_branch_a, w_branch_b, w_out, g_norm3,
           w2_gate, w2_up, w2_down):
    s, d = x.shape

    ada = _ada(c.reshape(d, 1), w_ada, b_ada.reshape(1, -1))
    sh1, sc1, gt1, sh2, sc2, gt2, sh3, sc3, gt3 = [ada[:, k * d:(k + 1) * d] for k in range(N_ADA)]

    def ffn_weights(wg, wu, wd):
        return wg.astype(BF16), wu.astype(BF16), wd.astype(BF16)

    x1 = _ffn(x, g_norm1.reshape(1, d), sh1, sc1, gt1, *ffn_weights(w1_gate, w1_up, w1_down))

    n2 = _normmod(x1, g_norm2.reshape(1, d), sh2, sc2)
    wi_b = w_in.astype(BF16)
    o_u, o_v, o_q = 0, SGU_WIDTH, 2 * SGU_WIDTH
    o_k, o_vv = o_q + ATTN_WIDTH, o_q + 2 * ATTN_WIDTH
    o_qi = o_vv + ATTN_WIDTH
    o_ki = o_qi + IDX_HEADS * IDX_DIM
    o_g = o_ki + IDX_DIM + IDX_HEADS

    ug = _proj(n2, wi_b[:, o_u:o_v], "gelu")
    vn = _proj(n2, wi_b[:, o_v:o_q], "gelu_gnorm", gain=g_sgu.reshape(1, SGU_WIDTH))
    vv = _proj(n2, wi_b[:, o_vv:o_qi], "none")
    sg = _proj(n2, wi_b[:, o_g:], "sigmoid")
    ki, wit = _kidx(n2, _pad_cols(wi_b[:, o_ki:o_g], LANES),
                    g_kidx.reshape(1, IDX_DIM), b_kidx.reshape(1, IDX_DIM))

    cpos, cneg, alibi_c = _alibi_tables()
    hw = HEAD_DIM + AUG
    row = lambda i: (0, 0)
    q_aug = _attn_proj(
        functools.partial(_attn_q_kernel, post_scale=HEAD_DIM ** -0.5 * LOG2E),
        n2, wi_b[:, o_q:o_k], [jnp.tile(g_q, ATTN_HEADS).reshape(1, -1), cpos],
        [pl.BlockSpec((1, ATTN_WIDTH), row), pl.BlockSpec((ATTN_HEADS, AUG), row)],
        jax.ShapeDtypeStruct((s, ATTN_HEADS * hw), BF16),
        pl.BlockSpec((PROJ_TM, ATTN_HEADS * hw), lambda i: (i, 0)), "proj_attn_q")
    kt_aug = _attn_proj(
        _attn_kt_kernel, n2, wi_b[:, o_k:o_vv], [jnp.tile(g_k, ATTN_HEADS).reshape(1, -1), cneg],
        [pl.BlockSpec((1, ATTN_WIDTH), row), pl.BlockSpec((ATTN_HEADS, AUG, 1), lambda i: (0, 0, 0))],
        jax.ShapeDtypeStruct((ATTN_HEADS * hw, s), BF16),
        pl.BlockSpec((ATTN_HEADS * hw, PROJ_TM), lambda i: (0, i)), "proj_attn_kt")
    qit = _attn_proj(
        _idx_q_kernel, n2, wi_b[:, o_qi:o_ki], [], [],
        jax.ShapeDtypeStruct((s // ATT_TQ, IDX_DIM, IDX_HEADS * ATT_TQ), BF16),
        pl.BlockSpec((PROJ_TM // ATT_TQ, IDX_DIM, IDX_HEADS * ATT_TQ), lambda i: (i, 0, 0)), "proj_idx_q")

    y_a = _sgu(ug, vn, w_spatial, jnp.transpose(b_spatial))
    y_b = _attention(qit, wit, ki, q_aug, kt_aug, vv, alibi_c)

    x2 = _merge(y_a, y_b, sg, x1, gt2, w_branch_a.astype(BF16), w_branch_b.astype(BF16),
                w_out.astype(BF16))

    return _ffn(x2, g_norm3.reshape(1, d), sh3, sc3, gt3, *ffn_weights(w2_gate, w2_up, w2_down))


def kernel(x, c, w_ada, b_ada, g_norm1, w1_gate, w1_up, w1_down, g_norm2, w_in, g_sgu, w_spatial,
           b_spatial, g_q, g_k, g_kidx, b_kidx, w_branch_a, w_branch_b, w_out, g_norm3,
           w2_gate, w2_up, w2_down):
    batch, depth = x.shape[0], w_ada.shape[0]
    outs = []
    for b in range(batch):
        xb = x[b]
        for l in range(depth):
            xb = _layer(xb, c[b], w_ada[l], b_ada[l], g_norm1[l], w1_gate[l], w1_up[l], w1_down[l],
                        g_norm2[l], w_in[l], g_sgu[l], w_spatial[l], b_spatial[l], g_q[l], g_k[l],
                        g_kidx[l], b_kidx[l], w_branch_a[l], w_branch_b[l], w_out[l], g_norm3[l],
                        w2_gate[l], w2_up[l], w2_down[l])
        outs.append(xb[None])
    return outs[0] if batch == 1 else jnp.concatenate(outs)
```

```python
import functools

import numpy as np
import jax
import jax.numpy as jnp
from jax import lax
from jax.experimental import pallas as pl
from jax.experimental.pallas import tpu as pltpu

F32 = jnp.float32
BF16 = jnp.bfloat16
I32 = jnp.int32

CHUNK = 128
SGU_GROUPS = 8
SGU_WIDTH = 1024
ATTN_HEADS = 8
HEAD_DIM = 128
ATTN_WIDTH = ATTN_HEADS * HEAD_DIM
IDX_HEADS = 16
IDX_DIM = 64
TOPK_MAX = 256
N_ADA = 9
EPS = 1e-6
NEG = -1e30

LANES = 128
SUBLANES = 8
VMEM_LIMIT = 56 * 1024 * 1024
FFN_VMEM_LIMIT = 60 * 1024 * 1024

FFN_TM = 512
FFN_TF = 1024
PROJ_TM = 1024
PROJ_TN = 1024
SGU_TM = 512
MERGE_TM = 512
ATT_TQ = 256
ATT_TK = 1024
IDX_TK = 256
IDX_SUB = 4
CNT_BLK = 1024
CNT_ACC = 8
BISECT_CAP = 512
AUG = 128
N_PIECES = 3
LOG2E = 1.4426950408889634


def _cparams(sem):
    return pltpu.CompilerParams(dimension_semantics=sem, vmem_limit_bytes=VMEM_LIMIT)


def _gelu_tanh(x):
    return 0.5 * x * (1.0 + jnp.tanh(0.7978845608028654 * (x + 0.044715 * (x * x * x))))


def _rms_mod(x, g, sh, sc):
    ms = jnp.mean(x * x, axis=-1, keepdims=True)
    y = (x * lax.rsqrt(ms + EPS)) * g
    return y * (1.0 + sc) + sh


def _ada_kernel(c_ref, w_ref, b_ref, o_ref, sb_ref):
    d = w_ref.shape[0]
    tn = w_ref.shape[1]

    @pl.when(pl.program_id(0) == 0)
    def _():
        cc = c_ref[...]
        sb_ref[...] = jnp.broadcast_to(cc * jax.nn.sigmoid(cc), (d, LANES))

    sb = sb_ref[...]
    for cb in range(tn // LANES):
        cols = slice(cb * LANES, (cb + 1) * LANES)
        prod = (w_ref[:, cols] * sb).reshape(d // SUBLANES, SUBLANES, LANES).sum(axis=0)
        o_ref[:, cols] = prod.sum(axis=0, keepdims=True) + b_ref[:, cols]


def _ada(c_col, w, b):
    d, n = w.shape
    tn = 1024
    return pl.pallas_call(
        _ada_kernel,
        grid=(n // tn,),
        in_specs=[pl.BlockSpec((d, 1), lambda j: (0, 0)),
                  pl.BlockSpec((d, tn), lambda j: (0, j)),
                  pl.BlockSpec((1, tn), lambda j: (0, j))],
        out_specs=pl.BlockSpec((1, tn), lambda j: (0, j)),
        out_shape=jax.ShapeDtypeStruct((1, n), F32),
        scratch_shapes=[pltpu.VMEM((d, LANES), F32)],
        compiler_params=_cparams(("arbitrary",)),
        name="ada",
    )(c_col, w, b)


def _gate_up(n, wg_ref, wu_ref):
    g = jnp.dot(n, wg_ref[...], preferred_element_type=F32)
    u = jnp.dot(n, wu_ref[...], preferred_element_type=F32)
    return ((g * jax.nn.sigmoid(g)) * u).astype(BF16)


def _ffn_kernel(x_ref, g_ref, sh_ref, sc_ref, gt_ref, wg_ref, wu_ref, wd_ref, *rest):
    *tail, o_ref, n_ref, h_ref = rest
    j = pl.program_id(1)
    last = pl.num_programs(1) - 1

    @pl.when(j == 0)
    def _():
        n_ref[...] = _rms_mod(x_ref[...], g_ref[...], sh_ref[...], sc_ref[...]).astype(BF16)
        o_ref[...] = jnp.zeros(o_ref.shape, F32)
        h_ref[0] = _gate_up(n_ref[...], wg_ref, wu_ref)

    @pl.when((j > 0) & (j < last))
    def _():
        o_ref[...] += jnp.dot(h_ref[(j - 1) % 2], wd_ref[...], preferred_element_type=F32)
        h_ref[j % 2] = _gate_up(n_ref[...], wg_ref, wu_ref)

    @pl.when(j == last)
    def _():
        y = o_ref[...] + jnp.dot(h_ref[(j - 1) % 2], wd_ref[...], preferred_element_type=F32)
        if tail:
            wg_t, wu_t, wd_t = tail
            y = y + jnp.dot(_gate_up(n_ref[...], wg_t, wu_t), wd_t[...], preferred_element_type=F32)
        o_ref[...] = x_ref[...] + (0.5 * gt_ref[...]) * y


def _ffn(x, g, sh, sc, gt, wg, wu, wd):
    s, d = x.shape
    f = wg.shape[1]
    tm = min(FFN_TM, s)
    n_main = f // FFN_TF
    f_main = n_main * FFN_TF
    row = pl.BlockSpec((1, d), lambda i, j: (0, 0))
    assert n_main >= 1
    in_specs = [pl.BlockSpec((tm, d), lambda i, j: (i, 0)), row, row, row, row,
                pl.BlockSpec((d, FFN_TF), lambda i, j: (0, jnp.minimum(j, n_main - 1))),
                pl.BlockSpec((d, FFN_TF), lambda i, j: (0, jnp.minimum(j, n_main - 1))),
                pl.BlockSpec((FFN_TF, d), lambda i, j: (jnp.maximum(j - 1, 0), 0))]
    args = [x, g, sh, sc, gt, wg, wu, wd]
    if f_main < f:
        assert (f - f_main) % LANES == 0
        args += [wg[:, f_main:], wu[:, f_main:], wd[f_main:]]
        once = dict(pipeline_mode=pl.Buffered(1))
        in_specs += [pl.BlockSpec((d, f - f_main), lambda i, j: (0, 0), **once),
                     pl.BlockSpec((d, f - f_main), lambda i, j: (0, 0), **once),
                     pl.BlockSpec((f - f_main, d), lambda i, j: (0, 0), **once)]
    return pl.pallas_call(
        _ffn_kernel,
        grid=(s // tm, n_main + 1),
        in_specs=in_specs,
        out_specs=pl.BlockSpec((tm, d), lambda i, j: (i, 0)),
        out_shape=jax.ShapeDtypeStruct((s, d), F32),
        scratch_shapes=[pltpu.VMEM((tm, d), BF16), pltpu.VMEM((2, tm, FFN_TF), BF16)],
        compiler_params=pltpu.CompilerParams(dimension_semantics=("parallel", "arbitrary"),
                                             vmem_limit_bytes=FFN_VMEM_LIMIT),
        name="ffn",
    )(*args)


def _normmod_kernel(x_ref, g_ref, sh_ref, sc_ref, o_ref):
    o_ref[...] = _rms_mod(x_ref[...], g_ref[...], sh_ref[...], sc_ref[...]).astype(o_ref.dtype)


def _normmod(x, g, sh, sc):
    s, d = x.shape
    tm = 512
    row = pl.BlockSpec((1, d), lambda i: (0, 0))
    return pl.pallas_call(
        _normmod_kernel,
        grid=(s // tm,),
        in_specs=[pl.BlockSpec((tm, d), lambda i: (i, 0)), row, row, row],
        out_specs=pl.BlockSpec((tm, d), lambda i: (i, 0)),
        out_shape=jax.ShapeDtypeStruct((s, d), BF16),
        compiler_params=_cparams(("parallel",)),
        name="normmod",
    )(x, g, sh, sc)


def _group_rms(z, gain, post_scale):
    outs = []
    for gidx in range(z.shape[1] // LANES):
        cols = slice(gidx * LANES, (gidx + 1) * LANES)
        zg = z[:, cols]
        ms = jnp.mean(zg * zg, axis=-1, keepdims=True)
        y = (zg * lax.rsqrt(ms + EPS)) * gain[:, cols]
        if post_scale != 1.0:
            y = y * post_scale
        outs.append(y)
    return jnp.concatenate(outs, axis=-1)


def _proj_kernel(n_ref, w_ref, *rest, mode):
    o_ref = rest[-1]
    z = jnp.dot(n_ref[...], w_ref[...], preferred_element_type=F32)
    if mode == "gelu":
        out = _gelu_tanh(z)
    elif mode == "gelu_gnorm":
        out = _group_rms(_gelu_tanh(z), rest[0][...], 1.0)
    elif mode == "sigmoid":
        out = jax.nn.sigmoid(z)
    else:
        assert mode == "none"
        out = z
    o_ref[...] = out.astype(o_ref.dtype)


def _proj(n, w, mode, gain=None):
    s, d = n.shape
    nout = w.shape[1]
    in_specs = [pl.BlockSpec((PROJ_TM, d), lambda i, j: (i, 0)),
                pl.BlockSpec((d, PROJ_TN), lambda i, j: (0, j))]
    args = [n, w]
    if gain is not None:
        in_specs.append(pl.BlockSpec((1, PROJ_TN), lambda i, j: (0, j)))
        args.append(gain)
    return pl.pallas_call(
        functools.partial(_proj_kernel, mode=mode),
        grid=(s // PROJ_TM, nout // PROJ_TN),
        in_specs=in_specs,
        out_specs=pl.BlockSpec((PROJ_TM, PROJ_TN), lambda i, j: (i, j)),
        out_shape=jax.ShapeDtypeStruct((s, nout), BF16),
        compiler_params=_cparams(("parallel", "arbitrary")),
        name="proj_" + mode,
    )(*args)


def _split_hi_lo(pos):
    lo = pos & (LANES - 1)
    return pos - lo, lo


def _piece_slot(idx, k):
    return (idx >= k * N_PIECES) & (idx < (k + 1) * N_PIECES)


def _attn_q_kernel(n_ref, w_ref, gain_ref, cpos_ref, o_ref, *, post_scale):
    tm = n_ref.shape[0]
    y = _group_rms(jnp.dot(n_ref[...], w_ref[...], preferred_element_type=F32),
                   gain_ref[...], post_scale)
    col = lax.broadcasted_iota(I32, (tm, AUG), 1)
    t_hi, t_lo = _split_hi_lo(lax.broadcasted_iota(I32, (tm, AUG), 0) & (ATT_TQ - 1))
    tab = jnp.where(_piece_slot(col, 2), t_hi, jnp.where(_piece_slot(col, 3), t_lo, 0)).astype(F32)
    hw = HEAD_DIM + AUG
    for h in range(ATTN_HEADS):
        o_ref[:, h * hw:h * hw + HEAD_DIM] = y[:, h * HEAD_DIM:(h + 1) * HEAD_DIM].astype(o_ref.dtype)
        o_ref[:, h * hw + HEAD_DIM:(h + 1) * hw] = (tab + cpos_ref[h:h + 1, :]).astype(o_ref.dtype)


def _attn_kt_kernel(n_ref, w_ref, gain_ref, cneg_ref, o_ref):
    tm = n_ref.shape[0]
    y = _group_rms(jnp.dot(n_ref[...], w_ref[...], preferred_element_type=F32), gain_ref[...], 1.0)
    yt = jnp.transpose(y)
    row = lax.broadcasted_iota(I32, (AUG, tm), 0)
    pos = pl.program_id(0) * tm + lax.broadcasted_iota(I32, (AUG, tm), 1)
    u_hi, u_lo = _split_hi_lo(pos & (ATT_TK - 1))
    tab = jnp.where(_piece_slot(row, 0), u_hi, jnp.where(_piece_slot(row, 1), u_lo, 0)).astype(F32)
    hw = HEAD_DIM + AUG
    for h in range(ATTN_HEADS):
        o_ref[h * hw:h * hw + HEAD_DIM, :] = yt[h * HEAD_DIM:(h + 1) * HEAD_DIM, :].astype(o_ref.dtype)
        o_ref[h * hw + HEAD_DIM:(h + 1) * hw, :] = (tab + cneg_ref[h]).astype(o_ref.dtype)


def _idx_q_kernel(n_ref, w_ref, o_ref):
    zt = jnp.transpose(jnp.dot(n_ref[...], w_ref[...], preferred_element_type=F32))
    for qt in range(n_ref.shape[0] // ATT_TQ):
        for h in range(IDX_HEADS):
            o_ref[qt, :, h * ATT_TQ:(h + 1) * ATT_TQ] = zt[
                h * IDX_DIM:(h + 1) * IDX_DIM, qt * ATT_TQ:(qt + 1) * ATT_TQ].astype(o_ref.dtype)


def _attn_proj(kernel, n, w, extra, extra_specs, out_shape, out_spec, name):
    s, d = n.shape
    return pl.pallas_call(
        kernel,
        grid=(s // PROJ_TM,),
        in_specs=[pl.BlockSpec((PROJ_TM, d), lambda i: (i, 0)),
                  pl.BlockSpec(w.shape, lambda i: (0, 0))] + extra_specs,
        out_specs=out_spec,
        out_shape=out_shape,
        compiler_params=_cparams(("parallel",)),
        name=name,
    )(n, w, *extra)


def _kidx_kernel(n_ref, w_ref, g_ref, b_ref, ki_ref, wit_ref):
    z = jnp.dot(n_ref[...], w_ref[...], preferred_element_type=F32)
    ki = z[:, :IDX_DIM]
    mu = jnp.mean(ki, axis=-1, keepdims=True)
    var = jnp.mean(jnp.square(ki - mu), axis=-1, keepdims=True)
    y = (ki - mu) * lax.rsqrt(var + EPS)
    ki_ref[...] = (y * g_ref[...] + b_ref[...]).astype(ki_ref.dtype)
    wit_ref[...] = jnp.transpose(z)[IDX_DIM:IDX_DIM + IDX_HEADS, :] * (IDX_HEADS ** -0.5 * IDX_DIM ** -0.5)


def _kidx(n, w, g, b):
    s, d = n.shape
    tm = 512
    return pl.pallas_call(
        _kidx_kernel,
        grid=(s // tm,),
        in_specs=[pl.BlockSpec((tm, d), lambda i: (i, 0)),
                  pl.BlockSpec((d, LANES), lambda i: (0, 0)),
                  pl.BlockSpec((1, IDX_DIM), lambda i: (0, 0)),
                  pl.BlockSpec((1, IDX_DIM), lambda i: (0, 0))],
        out_specs=[pl.BlockSpec((tm, IDX_DIM), lambda i: (i, 0)),
                   pl.BlockSpec((IDX_HEADS, tm), lambda i: (0, i))],
        out_shape=[jax.ShapeDtypeStruct((s, IDX_DIM), BF16),
                   jax.ShapeDtypeStruct((IDX_HEADS, s), F32)],
        compiler_params=_cparams(("parallel",)),
        name="proj_kidx",
    )(n, w, g, b)


def _sgu_kernel(u_ref, v_ref, ws_ref, bt_ref, o_ref):
    tm = u_ref.shape[0]
    r = lax.broadcasted_iota(I32, (CHUNK, CHUNK), 0)
    c = lax.broadcasted_iota(I32, (CHUNK, CHUNK), 1)
    causal = c <= r
    for g in range(SGU_GROUPS):
        cols = slice(g * LANES, (g + 1) * LANES)
        w = jnp.where(causal, ws_ref[g], 0.0).astype(BF16)
        bcol = bt_ref[:, g:g + 1]
        for ch in range(tm // CHUNK):
            rows = slice(ch * CHUNK, (ch + 1) * CHUNK)
            sv = jnp.dot(w, v_ref[rows, cols], preferred_element_type=F32) + bcol
            o_ref[rows, cols] = (u_ref[rows, cols].astype(F32) * sv).astype(o_ref.dtype)


def _sgu(u, v, ws, bt):
    s, wdt = u.shape
    return pl.pallas_call(
        _sgu_kernel,
        grid=(s // SGU_TM,),
        in_specs=[pl.BlockSpec((SGU_TM, wdt), lambda i: (i, 0)),
                  pl.BlockSpec((SGU_TM, wdt), lambda i: (i, 0)),
                  pl.BlockSpec((SGU_GROUPS, CHUNK, CHUNK), lambda i: (0, 0, 0)),
                  pl.BlockSpec((CHUNK, SGU_GROUPS), lambda i: (0, 0))],
        out_specs=pl.BlockSpec((SGU_TM, wdt), lambda i: (i, 0)),
        out_shape=jax.ShapeDtypeStruct((s, wdt), BF16),
        compiler_params=_cparams(("parallel",)),
        name="sgu",
    )(u, v, ws, bt)


def _tile_lanes(x, n):
    return x if n == 1 else jnp.concatenate([x] * n, axis=1)


def _sublane_all(x, op):
    shift = SUBLANES // 2
    while shift:
        x = op(x, pltpu.roll(x, shift, 0))
        shift //= 2
    return x


def _count_keys(score_ref, n_blocks, pred):
    tq = score_ref.shape[1]

    def body(it, acc):
        off = pl.multiple_of(it * CNT_BLK, CNT_BLK)
        blk = score_ref[pl.ds(off, CNT_BLK), :].reshape(CNT_BLK // SUBLANES, SUBLANES, tq)
        ind = jnp.where(pred(blk, off), 1, 0)
        return acc + ind.reshape(-1, CNT_ACC, SUBLANES, tq).sum(axis=0)

    acc = lax.fori_loop(0, n_blocks, body, jnp.zeros((CNT_ACC, SUBLANES, tq), I32))
    tot = jnp.sum(acc.sum(axis=0).astype(F32), axis=0, keepdims=True)
    return jnp.broadcast_to(tot, (SUBLANES, tq)).astype(I32)


def _attn_kernel(itab_ref, jtab_ref, qit_ref, wit_ref, ki_ref, q_ref, kt_ref, v_ref, o_ref,
                 score_ref, r_ref, gmax_ref, thr_ref, m_ref, acc_ref, *, alibi_c, topk, idx_bits):
    i = itab_ref[pl.program_id(0)]
    j = jtab_ref[pl.program_id(0)]
    tq, tk = ATT_TQ, ATT_TK
    hw = HEAD_DIM + AUG
    q0 = i * tq
    n_kv = (q0 + tq + tk - 1) // tk
    n_blk = (q0 + tq + CNT_BLK - 1) // CNT_BLK

    @pl.when(j == 0)
    def _index_and_threshold():
        m_ref[...] = jnp.full(m_ref.shape, NEG, F32)
        acc_ref[...] = jnp.zeros(acc_ref.shape, F32)

        def index_chunk(off, n_sub):
            rows = n_sub * IDX_TK
            r_ref[0:rows, :] = jnp.dot(ki_ref[pl.ds(off, rows), :], qit_ref[0],
                                       preferred_element_type=F32)
            for sub in range(n_sub):
                r0 = sub * IDX_TK
                s_idx = off + r0 + lax.broadcasted_iota(I32, (IDX_TK, LANES), 0)
                for g in range(tq // LANES):
                    lanes = slice(g * LANES, (g + 1) * LANES)
                    t_idx = q0 + g * LANES + lax.broadcasted_iota(I32, (IDX_TK, LANES), 1)
                    acc = jnp.zeros((IDX_TK, LANES), F32)
                    for h in range(IDX_HEADS):
                        rr = r_ref[r0:r0 + IDX_TK, h * tq + g * LANES:h * tq + (g + 1) * LANES]
                        acc = acc + jnp.maximum(rr, 0.0) * wit_ref[h:h + 1, lanes]
                    score = jnp.where(s_idx <= t_idx, acc, -jnp.inf)
                    score_ref[pl.ds(pl.multiple_of(off + r0, IDX_TK), IDX_TK), lanes] = score
                    gmax_ref[:, lanes] = jnp.maximum(gmax_ref[:, lanes], score)

        def chunk_body(c, carry):
            index_chunk(pl.multiple_of(c * (IDX_SUB * IDX_TK), IDX_SUB * IDX_TK), IDX_SUB)
            return carry

        gmax_ref[...] = jnp.full(gmax_ref.shape, -jnp.inf, F32)
        n_idx = (q0 + tq) // IDX_TK
        lax.fori_loop(0, n_idx // IDX_SUB, chunk_body, 0)
        for rest in range(1, IDX_SUB):
            @pl.when(n_idx % IDX_SUB == rest)
            def _(rest=rest):
                index_chunk(pl.multiple_of((n_idx - rest) * IDX_TK, IDX_TK), rest)

        def fill_body(f, carry):
            off = pl.multiple_of(q0 + tq + f * tq, tq)
            score_ref[pl.ds(off, tq), :] = jnp.full((tq, tq), -jnp.inf, F32)
            return carry

        lax.fori_loop(0, (n_blk * CNT_BLK - (q0 + tq)) // tq, fill_body, 0)

        t_row = q0 + lax.broadcasted_iota(I32, (SUBLANES, tq), 1)
        kq = jnp.minimum(topk, t_row + 1)

        g = gmax_ref[...].reshape(IDX_TK // SUBLANES, SUBLANES, tq)
        hi0 = _sublane_all(g.max(axis=0), jnp.maximum)
        lo0 = _sublane_all(jnp.where(g == -jnp.inf, jnp.inf, g).min(axis=0), jnp.minimum)

        def bis_cond(st):
            it, _, _, _, _, done = st
            return jnp.logical_and(it < BISECT_CAP, done == 0)

        def bis_body(st):
            it, lo, hi, cnt, fin, _ = st
            mid = 0.5 * lo + 0.5 * hi
            stuck = (mid <= lo) | (mid >= hi)
            cand = jnp.where(stuck, hi, mid)
            tot = _count_keys(score_ref, n_blk, lambda blk, off: blk >= cand[None])
            take = (tot >= kq) & (fin == 0)
            lo = jnp.where(take, cand, lo)
            cnt = jnp.where(take, tot, cnt)
            hi = jnp.where(take | (fin != 0), hi, cand)
            fin = jnp.where(stuck, 1, fin)
            settled = (cnt == kq) | (fin != 0)
            done = (jnp.min(jnp.where(settled, 1.0, 0.0)) > 0.5).astype(I32)
            return it + 1, lo, hi, cnt, fin, done

        empty = lo0 > hi0
        _, thr, _, cnt, _, _ = lax.while_loop(
            bis_cond, bis_body,
            (jnp.int32(0), lo0, hi0, jnp.where(empty, 0, -1), jnp.where(empty, 1, 0), jnp.int32(0)))
        thr_ref[...] = thr
        cnt = lax.cond(jnp.min(cnt.astype(F32)) < 0.0,
                       lambda: jnp.where(cnt < 0, _count_keys(
                           score_ref, n_blk, lambda blk, off: blk >= thr[None]), cnt),
                       lambda: cnt)

        @pl.when(jnp.max(jnp.where(cnt > kq, 1.0, 0.0)) > 0.5)
        def _break_ties():
            def row_idx(off):
                return off + lax.broadcasted_iota(I32, (CNT_BLK, tq), 0).reshape(
                    CNT_BLK // SUBLANES, SUBLANES, tq)

            above = _count_keys(score_ref, n_blk, lambda blk, off: blk > thr[None])
            need = kq - above

            def cut_body(b, p):
                c = p + lax.shift_left(jnp.int32(1), idx_bits - 1 - b)
                below = _count_keys(
                    score_ref, n_blk, lambda blk, off: (blk == thr[None]) & (row_idx(off) < c[None]))
                return jnp.where(below < need, c, p)

            p = lax.fori_loop(0, idx_bits, cut_body, jnp.zeros((SUBLANES, tq), I32))

            def drop_body(it, carry):
                off = pl.multiple_of(it * CNT_BLK, CNT_BLK)
                blk = score_ref[pl.ds(off, CNT_BLK), :].reshape(CNT_BLK // SUBLANES, SUBLANES, tq)
                new = jnp.where((blk == thr[None]) & (row_idx(off) > p[None]), -jnp.inf, blk)
                score_ref[pl.ds(off, CNT_BLK), :] = new.reshape(CNT_BLK, tq)
                return carry

            lax.fori_loop(0, n_blk, drop_body, 0)

    k0 = pl.multiple_of(j * tk, tk)
    nl = tk // LANES
    sel_t = score_ref[pl.ds(k0, tk), :] >= thr_ref[0:1, :]
    bias = jnp.transpose(jnp.where(sel_t, 0.0, NEG))
    tile_dist = jnp.full((tq, LANES), k0 - q0, I32).astype(F32)
    ones_col = (lax.broadcasted_iota(I32, (tk, LANES), 1) == 0).astype(BF16)

    def logits(h):
        s = bias + jnp.dot(q_ref[:, h * hw:(h + 1) * hw], kt_ref[h * hw:(h + 1) * hw, :],
                           preferred_element_type=F32)
        return s, jnp.max(s, axis=-1, keepdims=True)

    def probs(h, s, s_max):
        shift = tile_dist * alibi_c[h]
        m_prev = m_ref[h]
        m_new = jnp.maximum(m_prev, s_max + shift)
        m_ref[h] = m_new
        p = jnp.exp2(s - _tile_lanes(m_new - shift, nl)).astype(BF16)
        return p, jnp.exp2(m_prev - m_new)

    def accumulate(h, p, alpha):
        v_aug = jnp.concatenate([v_ref[:, h * HEAD_DIM:(h + 1) * HEAD_DIM], ones_col], axis=1)
        acc_ref[h] = _tile_lanes(alpha, 2) * acc_ref[h] + jnp.dot(
            p, v_aug, preferred_element_type=F32)

    st_a, st_b = {}, {}
    for step in range(ATTN_HEADS + 2):
        if 0 <= step - 2 < ATTN_HEADS:
            accumulate(step - 2, *st_b.pop(step - 2))
        if step < ATTN_HEADS:
            st_a[step] = logits(step)
        if 0 <= step - 1 < ATTN_HEADS:
            st_b[step - 1] = probs(step - 1, *st_a.pop(step - 1))

    @pl.when(j == n_kv - 1)
    def _finish():
        for h in range(ATTN_HEADS):
            acc = acc_ref[h]
            o_ref[:, h * HEAD_DIM:(h + 1) * HEAD_DIM] = (
                acc[:, :HEAD_DIM] / acc[:, HEAD_DIM:HEAD_DIM + 1]).astype(o_ref.dtype)


def _alibi_pieces():
    out = []
    for h in range(ATTN_HEADS):
        rest = np.float32(LOG2E * 2.0 ** (-8.0 * (h + 1) / ATTN_HEADS))
        pieces = []
        for _ in range(N_PIECES):
            piece = np.float32(rest.astype(BF16))
            pieces.append(piece)
            rest = np.float32(rest - piece)
        out.append(pieces)
    return np.asarray(out, np.float32)


def _alibi_tables():
    pieces = _alibi_pieces()
    n = N_PIECES
    assert 4 * n <= AUG
    cpos = np.zeros((ATTN_HEADS, AUG), np.float32)
    cneg = np.zeros((ATTN_HEADS, AUG, 1), np.float32)
    cpos[:, 0 * n:1 * n] = pieces
    cpos[:, 1 * n:2 * n] = pieces
    cneg[:, 2 * n:3 * n, 0] = -pieces
    cneg[:, 3 * n:4 * n, 0] = -pieces
    alibi_c = tuple(float(p.sum(dtype=np.float32)) for p in pieces)
    return jnp.asarray(cpos), jnp.asarray(cneg), alibi_c


def _attention(qit, wit, ki, q_aug, kt_aug, v, alibi_c):
    s = q_aug.shape[0]
    tq, tk = ATT_TQ, ATT_TK
    hw = HEAD_DIM + AUG
    assert s % CNT_BLK == 0 and CNT_BLK % tk == 0 and CNT_BLK % tq == 0 and tq % LANES == 0
    assert tq & (tq - 1) == 0 and tk & (tk - 1) == 0 and PROJ_TM % tq == 0
    topk = min(TOPK_MAX, s // 4)
    idx_bits = max(1, (s - 1).bit_length())
    assert IDX_TK >= topk and tq % IDX_TK == 0

    steps = [(i, j) for i in range(s // tq) for j in range(((i + 1) * tq + tk - 1) // tk)]
    itab = jnp.asarray(np.array([p[0] for p in steps], np.int32))
    jtab = jnp.asarray(np.array([p[1] for p in steps], np.int32))

    grid_spec = pltpu.PrefetchScalarGridSpec(
        num_scalar_prefetch=2,
        grid=(len(steps),),
        in_specs=[pl.BlockSpec((1, IDX_DIM, IDX_HEADS * tq), lambda t, it, jt: (it[t], 0, 0)),
                  pl.BlockSpec((IDX_HEADS, tq), lambda t, it, jt: (0, it[t])),
                  pl.BlockSpec((s, IDX_DIM), lambda t, it, jt: (0, 0)),
                  pl.BlockSpec((tq, ATTN_HEADS * hw), lambda t, it, jt: (it[t], 0)),
                  pl.BlockSpec((ATTN_HEADS * hw, tk), lambda t, it, jt: (0, jt[t])),
                  pl.BlockSpec((tk, ATTN_WIDTH), lambda t, it, jt: (jt[t], 0))],
        out_specs=pl.BlockSpec((tq, ATTN_WIDTH), lambda t, it, jt: (it[t], 0)),
        scratch_shapes=[pltpu.VMEM((s, tq), F32),
                        pltpu.VMEM((IDX_SUB * IDX_TK, IDX_HEADS * tq), F32),
                        pltpu.VMEM((IDX_TK, tq), F32),
                        pltpu.VMEM((SUBLANES, tq), F32),
                        pltpu.VMEM((ATTN_HEADS, tq, LANES), F32),
                        pltpu.VMEM((ATTN_HEADS, tq, 2 * HEAD_DIM), F32)])
    return pl.pallas_call(
        functools.partial(_attn_kernel, alibi_c=alibi_c, topk=topk, idx_bits=idx_bits),
        grid_spec=grid_spec,
        out_shape=jax.ShapeDtypeStruct((s, ATTN_WIDTH), BF16),
        compiler_params=_cparams(("arbitrary",)),
        name="dsa_attention",
    )(itab, jtab, qit, wit, ki, q_aug, kt_aug, v)


def _merge_kernel(ya_ref, yb_ref, sg_a_ref, sg_b_ref, x_ref, gt_ref, wa_ref, wb_ref, wo_ref, o_ref):
    a = jnp.dot(ya_ref[...], wa_ref[...], preferred_element_type=F32)
    b = jnp.dot(yb_ref[...], wb_ref[...], preferred_element_type=F32)
    merged = sg_a_ref[...].astype(F32) * a + sg_b_ref[...].astype(F32) * b
    o_ref[...] = x_ref[...] + gt_ref[...] * jnp.dot(
        merged.astype(BF16), wo_ref[...], preferred_element_type=F32)


def _merge(ya, yb, sg, x, gt, wa, wb, wo):
    s, d = x.shape
    tm = MERGE_TM
    wdt = ya.shape[1]
    const = lambda i: (0, 0)
    return pl.pallas_call(
        _merge_kernel,
        grid=(s // tm,),
        in_specs=[pl.BlockSpec((tm, wdt), lambda i: (i, 0)),
                  pl.BlockSpec((tm, wdt), lambda i: (i, 0)),
                  pl.BlockSpec((tm, d), lambda i: (i, 0)),
                  pl.BlockSpec((tm, d), lambda i: (i, 1)),
                  pl.BlockSpec((tm, d), lambda i: (i, 0)),
                  pl.BlockSpec((1, d), const),
                  pl.BlockSpec((wdt, d), const, pipeline_mode=pl.Buffered(1)),
                  pl.BlockSpec((wdt, d), const, pipeline_mode=pl.Buffered(1)),
                  pl.BlockSpec((d, d), const, pipeline_mode=pl.Buffered(1))],
        out_specs=pl.BlockSpec((tm, d), lambda i: (i, 0)),
        out_shape=jax.ShapeDtypeStruct((s, d), F32),
        compiler_params=_cparams(("parallel",)),
        name="merge",
    )(ya, yb, sg, sg, x, gt, wa, wb, wo)


def _pad_cols(w, n):
    return jnp.pad(w, ((0, 0), (0, n - w.shape[1])))


def _layer(x, c, w_ada, b_ada, g_norm1, w1_gate, w1_up, w1_down, g_norm2, w_in, g_sgu, w_spatial,
           b_spatial, g_q, g_k, g_kidx, b_kidx, w_branch_a, w_branch_b, w_out, g_norm3,
           w2_gate, w2_up, w2_down):
    s, d = x.shape

    ada = _ada(c.reshape(d, 1), w_ada, b_ada.reshape(1, -1))
    sh1, sc1, gt1, sh2, sc2, gt2, sh3, sc3, gt3 = [ada[:, k * d:(k + 1) * d] for k in range(N_ADA)]

    def ffn_weights(wg, wu, wd):
        return wg.astype(BF16), wu.astype(BF16), wd.astype(BF16)

    x1 = _ffn(x, g_norm1.reshape(1, d), sh1, sc1, gt1, *ffn_weights(w1_gate, w1_up, w1_down))

    n2 = _normmod(x1, g_norm2.reshape(1, d), sh2, sc2)
    wi_b = w_in.astype(BF16)
    o_u, o_v, o_q = 0, SGU_WIDTH, 2 * SGU_WIDTH
    o_k, o_vv = o_q + ATTN_WIDTH, o_q + 2 * ATTN_WIDTH
    o_qi = o_vv + ATTN_WIDTH
    o_ki = o_qi + IDX_HEADS * IDX_DIM
    o_g = o_ki + IDX_DIM + IDX_HEADS

    ug = _proj(n2, wi_b[:, o_u:o_v], "gelu")
    vn = _proj(n2, wi_b[:, o_v:o_q], "gelu_gnorm", gain=g_sgu.reshape(1, SGU_WIDTH))
    vv = _proj(n2, wi_b[:, o_vv:o_qi], "none")
    sg = _proj(n2, wi_b[:, o_g:], "sigmoid")
    ki, wit = _kidx(n2, _pad_cols(wi_b[:, o_ki:o_g], LANES),
                    g_kidx.reshape(1, IDX_DIM), b_kidx.reshape(1, IDX_DIM))

    cpos, cneg, alibi_c = _alibi_tables()
    hw = HEAD_DIM + AUG
    row = lambda i: (0, 0)
    q_aug = _attn_proj(
        functools.partial(_attn_q_kernel, post_scale=HEAD_DIM ** -0.5 * LOG2E),
        n2, wi_b[:, o_q:o_k], [jnp.tile(g_q, ATTN_HEADS).reshape(1, -1), cpos],
        [pl.BlockSpec((1, ATTN_WIDTH), row), pl.BlockSpec((ATTN_HEADS, AUG), row)],
        jax.ShapeDtypeStruct((s, ATTN_HEADS * hw), BF16),
        pl.BlockSpec((PROJ_TM, ATTN_HEADS * hw), lambda i: (i, 0)), "proj_attn_q")
    kt_aug = _attn_proj(
        _attn_kt_kernel, n2, wi_b[:, o_k:o_vv], [jnp.tile(g_k, ATTN_HEADS).reshape(1, -1), cneg],
        [pl.BlockSpec((1, ATTN_WIDTH), row), pl.BlockSpec((ATTN_HEADS, AUG, 1), lambda i: (0, 0, 0))],
        jax.ShapeDtypeStruct((ATTN_HEADS * hw, s), BF16),
        pl.BlockSpec((ATTN_HEADS * hw, PROJ_TM), lambda i: (0, i)), "proj_attn_kt")
    qit = _attn_proj(
        _idx_q_kernel, n2, wi_b[:, o_qi:o_ki], [], [],
        jax.ShapeDtypeStruct((s // ATT_TQ, IDX_DIM, IDX_HEADS * ATT_TQ), BF16),
        pl.BlockSpec((PROJ_TM // ATT_TQ, IDX_DIM, IDX_HEADS * ATT_TQ), lambda i: (i, 0, 0)), "proj_idx_q")

    y_a = _sgu(ug, vn, w_spatial, jnp.transpose(b_spatial))
    y_b = _attention(qit, wit, ki, q_aug, kt_aug, vv, alibi_c)

    x2 = _merge(y_a, y_b, sg, x1, gt2, w_branch_a.astype(BF16), w_branch_b.astype(BF16),
                w_out.astype(BF16))

    return _ffn(x2, g_norm3.reshape(1, d), sh3, sc3, gt3, *ffn_weights(w2_gate, w2_up, w2_down))


def kernel(x, c, w_ada, b_ada, g_norm1, w1_gate, w1_up, w1_down, g_norm2, w_in, g_sgu, w_spatial,
           b_spatial, g_q, g_k, g_kidx, b_kidx, w_branch_a, w_branch_b, w_out, g_norm3,
           w2_gate, w2_up, w2_down):
    batch, depth = x.shape[0], w_ada.shape[0]
    outs = []
    for b in range(batch):
        xb = x[b]
        for l in range(depth):
            xb = _layer(xb, c[b], w_ada[l], b_ada[l], g_norm1[l], w1_gate[l], w1_up[l], w1_down[l],
                        g_norm2[l], w_in[l], g_sgu[l], w_spatial[l], b_spatial[l], g_q[l], g_k[l],
                        g_kidx[l], b_kidx[l], w_branch_a[l], w_branch_b[l], w_out[l], g_norm3[l],
                        w2_gate[l], w2_up[l], w2_down[l])
        outs.append(xb[None])
    return outs[0] if batch == 1 else jnp.concatenate(outs)
```

```python
import functools

import numpy as np
import jax
import jax.numpy as jnp
from jax import lax
from jax.experimental import pallas as pl
from jax.experimental.pallas import tpu as pltpu

F32 = jnp.float32
BF16 = jnp.bfloat16
I32 = jnp.int32

CHUNK = 128
SGU_GROUPS = 8
SGU_WIDTH = 1024
ATTN_HEADS = 8
HEAD_DIM = 128
ATTN_WIDTH = ATTN_HEADS * HEAD_DIM
IDX_HEADS = 16
IDX_DIM = 64
TOPK_MAX = 256
N_ADA = 9
EPS = 1e-6
NEG = -1e30

LANES = 128
SUBLANES = 8
VMEM_LIMIT = 56 * 1024 * 1024
FFN_VMEM_LIMIT = 60 * 1024 * 1024

FFN_TM = 512
FFN_TF = 1024
PROJ_TM = 1024
PROJ_TN = 1024
SGU_TM = 512
MERGE_TM = 512
ATT_TQ = 256
ATT_TK = 1024
IDX_TK = 256
IDX_SUB = 4
CNT_BLK = 1024
CNT_ACC = 8
BISECT_CAP = 512
AUG = 128
N_PIECES = 3
LOG2E = 1.4426950408889634


def _cparams(sem):
    return pltpu.CompilerParams(dimension_semantics=sem, vmem_limit_bytes=VMEM_LIMIT)


def _gelu_tanh(x):
    return 0.5 * x * (1.0 + jnp.tanh(0.7978845608028654 * (x + 0.044715 * (x * x * x))))


def _rms_mod(x, g, sh, sc):
    ms = jnp.mean(x * x, axis=-1, keepdims=True)
    y = (x * lax.rsqrt(ms + EPS)) * g
    return y * (1.0 + sc) + sh


def _ada_kernel(c_ref, w_ref, b_ref, o_ref, sb_ref):
    d = w_ref.shape[0]
    tn = w_ref.shape[1]

    @pl.when(pl.program_id(0) == 0)
    def _():
        cc = c_ref[...]
        sb_ref[...] = jnp.broadcast_to(cc * jax.nn.sigmoid(cc), (d, LANES))

    sb = sb_ref[...]
    for cb in range(tn // LANES):
        cols = slice(cb * LANES, (cb + 1) * LANES)
        prod = (w_ref[:, cols] * sb).reshape(d // SUBLANES, SUBLANES, LANES).sum(axis=0)
        o_ref[:, cols] = prod.sum(axis=0, keepdims=True) + b_ref[:, cols]


def _ada(c_col, w, b):
    d, n = w.shape
    tn = 1024
    return pl.pallas_call(
        _ada_kernel,
        grid=(n // tn,),
        in_specs=[pl.BlockSpec((d, 1), lambda j: (0, 0)),
                  pl.BlockSpec((d, tn), lambda j: (0, j)),
                  pl.BlockSpec((1, tn), lambda j: (0, j))],
        out_specs=pl.BlockSpec((1, tn), lambda j: (0, j)),
        out_shape=jax.ShapeDtypeStruct((1, n), F32),
        scratch_shapes=[pltpu.VMEM((d, LANES), F32)],
        compiler_params=_cparams(("arbitrary",)),
        name="ada",
    )(c_col, w, b)


def _gate_up(n, wg_ref, wu_ref):
    g = jnp.dot(n, wg_ref[...], preferred_element_type=F32)
    u = jnp.dot(n, wu_ref[...], preferred_element_type=F32)
    return ((g * jax.nn.sigmoid(g)) * u).astype(BF16)


def _ffn_kernel(x_ref, g_ref, sh_ref, sc_ref, gt_ref, wg_ref, wu_ref, wd_ref, *rest):
    *tail, o_ref, n_ref, h_ref = rest
    j = pl.program_id(1)
    last = pl.num_programs(1) - 1

    @pl.when(j == 0)
    def _():
        n_ref[...] = _rms_mod(x_ref[...], g_ref[...], sh_ref[...], sc_ref[...]).astype(BF16)
        o_ref[...] = jnp.zeros(o_ref.shape, F32)
        h_ref[0] = _gate_up(n_ref[...], wg_ref, wu_ref)

    @pl.when((j > 0) & (j < last))
    def _():
        o_ref[...] += jnp.dot(h_ref[(j - 1) % 2], wd_ref[...], preferred_element_type=F32)
        h_ref[j % 2] = _gate_up(n_ref[...], wg_ref, wu_ref)

    @pl.when(j == last)
    def _():
        y = o_ref[...] + jnp.dot(h_ref[(j - 1) % 2], wd_ref[...], preferred_element_type=F32)
        if tail:
            wg_t, wu_t, wd_t = tail
            y = y + jnp.dot(_gate_up(n_ref[...], wg_t, wu_t), wd_t[...], preferred_element_type=F32)
        o_ref[...] = x_ref[...] + (0.5 * gt_ref[...]) * y


def _ffn(x, g, sh, sc, gt, wg, wu, wd):
    s, d = x.shape
    f = wg.shape[1]
    tm = min(FFN_TM, s)
    n_main = f // FFN_TF
    f_main = n_main * FFN_TF
    row = pl.BlockSpec((1, d), lambda i, j: (0, 0))
    assert n_main >= 1
    in_specs = [pl.BlockSpec((tm, d), lambda i, j: (i, 0)), row, row, row, row,
                pl.BlockSpec((d, FFN_TF), lambda i, j: (0, jnp.minimum(j, n_main - 1))),
                pl.BlockSpec((d, FFN_TF), lambda i, j: (0, jnp.minimum(j, n_main - 1))),
                pl.BlockSpec((FFN_TF, d), lambda i, j: (jnp.where(j == 0, n_main - 1, j - 1), 0))]
    args = [x, g, sh, sc, gt, wg, wu, wd]
    if f_main < f:
        assert (f - f_main) % LANES == 0
        args += [wg[:, f_main:], wu[:, f_main:], wd[f_main:]]
        once = dict(pipeline_mode=pl.Buffered(1))
        in_specs += [pl.BlockSpec((d, f - f_main), lambda i, j: (0, 0), **once),
                     pl.BlockSpec((d, f - f_main), lambda i, j: (0, 0), **once),
                     pl.BlockSpec((f - f_main, d), lambda i, j: (0, 0), **once)]
    return pl.pallas_call(
        _ffn_kernel,
        grid=(s // tm, n_main + 1),
        in_specs=in_specs,
        out_specs=pl.BlockSpec((tm, d), lambda i, j: (i, 0)),
        out_shape=jax.ShapeDtypeStruct((s, d), F32),
        scratch_shapes=[pltpu.VMEM((tm, d), BF16), pltpu.VMEM((2, tm, FFN_TF), BF16)],
        compiler_params=pltpu.CompilerParams(dimension_semantics=("parallel", "arbitrary"),
                                             vmem_limit_bytes=FFN_VMEM_LIMIT),
        name="ffn",
    )(*args)


def _normmod_kernel(x_ref, g_ref, sh_ref, sc_ref, o_ref):
    o_ref[...] = _rms_mod(x_ref[...], g_ref[...], sh_ref[...], sc_ref[...]).astype(o_ref.dtype)


def _normmod(x, g, sh, sc):
    s, d = x.shape
    tm = 512
    row = pl.BlockSpec((1, d), lambda i: (0, 0))
    return pl.pallas_call(
        _normmod_kernel,
        grid=(s // tm,),
        in_specs=[pl.BlockSpec((tm, d), lambda i: (i, 0)), row, row, row],
        out_specs=pl.BlockSpec((tm, d), lambda i: (i, 0)),
        out_shape=jax.ShapeDtypeStruct((s, d), BF16),
        compiler_params=_cparams(("parallel",)),
        name="normmod",
    )(x, g, sh, sc)


def _group_rms(z, gain, post_scale):
    outs = []
    for gidx in range(z.shape[1] // LANES):
        cols = slice(gidx * LANES, (gidx + 1) * LANES)
        zg = z[:, cols]
        ms = jnp.mean(zg * zg, axis=-1, keepdims=True)
        y = (zg * lax.rsqrt(ms + EPS)) * gain[:, cols]
        if post_scale != 1.0:
            y = y * post_scale
        outs.append(y)
    return jnp.concatenate(outs, axis=-1)


def _proj_kernel(n_ref, w_ref, *rest, mode):
    o_ref = rest[-1]
    z = jnp.dot(n_ref[...], w_ref[...], preferred_element_type=F32)
    if mode == "gelu":
        out = _gelu_tanh(z)
    elif mode == "gelu_gnorm":
        out = _group_rms(_gelu_tanh(z), rest[0][...], 1.0)
    elif mode == "sigmoid":
        out = jax.nn.sigmoid(z)
    else:
        assert mode == "none"
        out = z
    o_ref[...] = out.astype(o_ref.dtype)


def _proj(n, w, mode, gain=None):
    s, d = n.shape
    nout = w.shape[1]
    in_specs = [pl.BlockSpec((PROJ_TM, d), lambda i, j: (i, 0)),
                pl.BlockSpec((d, PROJ_TN), lambda i, j: (0, j))]
    args = [n, w]
    if gain is not None:
        in_specs.append(pl.BlockSpec((1, PROJ_TN), lambda i, j: (0, j)))
        args.append(gain)
    return pl.pallas_call(
        functools.partial(_proj_kernel, mode=mode),
        grid=(s // PROJ_TM, nout // PROJ_TN),
        in_specs=in_specs,
        out_specs=pl.BlockSpec((PROJ_TM, PROJ_TN), lambda i, j: (i, j)),
        out_shape=jax.ShapeDtypeStruct((s, nout), BF16),
        compiler_params=_cparams(("parallel", "arbitrary")),
        name="proj_" + mode,
    )(*args)


def _split_hi_lo(pos):
    lo = pos & (LANES - 1)
    return pos - lo, lo


def _piece_slot(idx, k):
    return (idx >= k * N_PIECES) & (idx < (k + 1) * N_PIECES)


def _attn_q_kernel(n_ref, w_ref, gain_ref, cpos_ref, o_ref, *, post_scale):
    tm = n_ref.shape[0]
    y = _group_rms(jnp.dot(n_ref[...], w_ref[...], preferred_element_type=F32),
                   gain_ref[...], post_scale)
    col = lax.broadcasted_iota(I32, (tm, AUG), 1)
    t_hi, t_lo = _split_hi_lo(lax.broadcasted_iota(I32, (tm, AUG), 0) & (ATT_TQ - 1))
    tab = jnp.where(_piece_slot(col, 2), t_hi, jnp.where(_piece_slot(col, 3), t_lo, 0)).astype(F32)
    hw = HEAD_DIM + AUG
    for h in range(ATTN_HEADS):
        o_ref[:, h * hw:h * hw + HEAD_DIM] = y[:, h * HEAD_DIM:(h + 1) * HEAD_DIM].astype(o_ref.dtype)
        o_ref[:, h * hw + HEAD_DIM:(h + 1) * hw] = (tab + cpos_ref[h:h + 1, :]).astype(o_ref.dtype)


def _attn_kt_kernel(n_ref, w_ref, gain_ref, cneg_ref, o_ref):
    tm = n_ref.shape[0]
    y = _group_rms(jnp.dot(n_ref[...], w_ref[...], preferred_element_type=F32), gain_ref[...], 1.0)
    yt = jnp.transpose(y)
    row = lax.broadcasted_iota(I32, (AUG, tm), 0)
    pos = pl.program_id(0) * tm + lax.broadcasted_iota(I32, (AUG, tm), 1)
    u_hi, u_lo = _split_hi_lo(pos & (ATT_TK - 1))
    tab = jnp.where(_piece_slot(row, 0), u_hi, jnp.where(_piece_slot(row, 1), u_lo, 0)).astype(F32)
    hw = HEAD_DIM + AUG
    for h in range(ATTN_HEADS):
        o_ref[h * hw:h * hw + HEAD_DIM, :] = yt[h * HEAD_DIM:(h + 1) * HEAD_DIM, :].astype(o_ref.dtype)
        o_ref[h * hw + HEAD_DIM:(h + 1) * hw, :] = (tab + cneg_ref[h]).astype(o_ref.dtype)


def _idx_q_kernel(n_ref, w_ref, o_ref):
    zt = jnp.transpose(jnp.dot(n_ref[...], w_ref[...], preferred_element_type=F32))
    for qt in range(n_ref.shape[0] // ATT_TQ):
        for h in range(IDX_HEADS):
            o_ref[qt, :, h * ATT_TQ:(h + 1) * ATT_TQ] = zt[
                h * IDX_DIM:(h + 1) * IDX_DIM, qt * ATT_TQ:(qt + 1) * ATT_TQ].astype(o_ref.dtype)


def _attn_proj(kernel, n, w, extra, extra_specs, out_shape, out_spec, name):
    s, d = n.shape
    return pl.pallas_call(
        kernel,
        grid=(s // PROJ_TM,),
        in_specs=[pl.BlockSpec((PROJ_TM, d), lambda i: (i, 0)),
                  pl.BlockSpec(w.shape, lambda i: (0, 0))] + extra_specs,
        out_specs=out_spec,
        out_shape=out_shape,
        compiler_params=_cparams(("parallel",)),
        name=name,
    )(n, w, *extra)


def _kidx_kernel(n_ref, w_ref, g_ref, b_ref, ki_ref, wit_ref):
    z = jnp.dot(n_ref[...], w_ref[...], preferred_element_type=F32)
    ki = z[:, :IDX_DIM]
    mu = jnp.mean(ki, axis=-1, keepdims=True)
    var = jnp.mean(jnp.square(ki - mu), axis=-1, keepdims=True)
    y = (ki - mu) * lax.rsqrt(var + EPS)
    ki_ref[...] = (y * g_ref[...] + b_ref[...]).astype(ki_ref.dtype)
    wit_ref[...] = jnp.transpose(z)[IDX_DIM:IDX_DIM + IDX_HEADS, :] * (IDX_HEADS ** -0.5 * IDX_DIM ** -0.5)


def _kidx(n, w, g, b):
    s, d = n.shape
    tm = 512
    return pl.pallas_call(
        _kidx_kernel,
        grid=(s // tm,),
        in_specs=[pl.BlockSpec((tm, d), lambda i: (i, 0)),
                  pl.BlockSpec((d, LANES), lambda i: (0, 0)),
                  pl.BlockSpec((1, IDX_DIM), lambda i: (0, 0)),
                  pl.BlockSpec((1, IDX_DIM), lambda i: (0, 0))],
        out_specs=[pl.BlockSpec((tm, IDX_DIM), lambda i: (i, 0)),
                   pl.BlockSpec((IDX_HEADS, tm), lambda i: (0, i))],
        out_shape=[jax.ShapeDtypeStruct((s, IDX_DIM), BF16),
                   jax.ShapeDtypeStruct((IDX_HEADS, s), F32)],
        compiler_params=_cparams(("parallel",)),
        name="proj_kidx",
    )(n, w, g, b)


def _sgu_kernel(u_ref, v_ref, ws_ref, bt_ref, o_ref):
    tm = u_ref.shape[0]
    r = lax.broadcasted_iota(I32, (CHUNK, CHUNK), 0)
    c = lax.broadcasted_iota(I32, (CHUNK, CHUNK), 1)
    causal = c <= r
    for g in range(SGU_GROUPS):
        cols = slice(g * LANES, (g + 1) * LANES)
        w = jnp.where(causal, ws_ref[g], 0.0).astype(BF16)
        bcol = bt_ref[:, g:g + 1]
        for ch in range(tm // CHUNK):
            rows = slice(ch * CHUNK, (ch + 1) * CHUNK)
            sv = jnp.dot(w, v_ref[rows, cols], preferred_element_type=F32) + bcol
            o_ref[rows, cols] = (u_ref[rows, cols].astype(F32) * sv).astype(o_ref.dtype)


def _sgu(u, v, ws, bt):
    s, wdt = u.shape
    return pl.pallas_call(
        _sgu_kernel,
        grid=(s // SGU_TM,),
        in_specs=[pl.BlockSpec((SGU_TM, wdt), lambda i: (i, 0)),
                  pl.BlockSpec((SGU_TM, wdt), lambda i: (i, 0)),
                  pl.BlockSpec((SGU_GROUPS, CHUNK, CHUNK), lambda i: (0, 0, 0)),
                  pl.BlockSpec((CHUNK, SGU_GROUPS), lambda i: (0, 0))],
        out_specs=pl.BlockSpec((SGU_TM, wdt), lambda i: (i, 0)),
        out_shape=jax.ShapeDtypeStruct((s, wdt), BF16),
        compiler_params=_cparams(("parallel",)),
        name="sgu",
    )(u, v, ws, bt)


def _tile_lanes(x, n):
    return x if n == 1 else jnp.concatenate([x] * n, axis=1)


def _sublane_all(x, op):
    shift = SUBLANES // 2
    while shift:
        x = op(x, pltpu.roll(x, shift, 0))
        shift //= 2
    return x


def _count_keys(score_ref, n_blocks, pred):
    tq = score_ref.shape[1]

    def body(it, acc):
        off = pl.multiple_of(it * CNT_BLK, CNT_BLK)
        blk = score_ref[pl.ds(off, CNT_BLK), :].reshape(CNT_BLK // SUBLANES, SUBLANES, tq)
        ind = jnp.where(pred(blk, off), 1, 0)
        return acc + ind.reshape(-1, CNT_ACC, SUBLANES, tq).sum(axis=0)

    acc = lax.fori_loop(0, n_blocks, body, jnp.zeros((CNT_ACC, SUBLANES, tq), I32))
    tot = jnp.sum(acc.sum(axis=0).astype(F32), axis=0, keepdims=True)
    return jnp.broadcast_to(tot, (SUBLANES, tq)).astype(I32)


def _attn_kernel(itab_ref, jtab_ref, qit_ref, wit_ref, ki_ref, q_ref, kt_ref, v_ref, o_ref,
                 score_ref, r_ref, gmax_ref, thr_ref, m_ref, acc_ref, *, alibi_c, topk, idx_bits):
    i = itab_ref[pl.program_id(0)]
    j = jtab_ref[pl.program_id(0)]
    tq, tk = ATT_TQ, ATT_TK
    hw = HEAD_DIM + AUG
    q0 = i * tq
    n_kv = (q0 + tq + tk - 1) // tk
    n_blk = (q0 + tq + CNT_BLK - 1) // CNT_BLK

    @pl.when(j == 0)
    def _index_and_threshold():
        m_ref[...] = jnp.full(m_ref.shape, NEG, F32)
        acc_ref[...] = jnp.zeros(acc_ref.shape, F32)

        def index_chunk(off, n_sub):
            rows = n_sub * IDX_TK
            r_ref[0:rows, :] = jnp.dot(ki_ref[pl.ds(off, rows), :], qit_ref[0],
                                       preferred_element_type=F32)
            for sub in range(n_sub):
                r0 = sub * IDX_TK
                s_idx = off + r0 + lax.broadcasted_iota(I32, (IDX_TK, LANES), 0)
                for g in range(tq // LANES):
                    lanes = slice(g * LANES, (g + 1) * LANES)
                    t_idx = q0 + g * LANES + lax.broadcasted_iota(I32, (IDX_TK, LANES), 1)
                    acc = jnp.zeros((IDX_TK, LANES), F32)
                    for h in range(IDX_HEADS):
                        rr = r_ref[r0:r0 + IDX_TK, h * tq + g * LANES:h * tq + (g + 1) * LANES]
                        acc = acc + jnp.maximum(rr, 0.0) * wit_ref[h:h + 1, lanes]
                    score = jnp.where(s_idx <= t_idx, acc, -jnp.inf)
                    score_ref[pl.ds(pl.multiple_of(off + r0, IDX_TK), IDX_TK), lanes] = score
                    gmax_ref[:, lanes] = jnp.maximum(gmax_ref[:, lanes], score)

        def chunk_body(c, carry):
            index_chunk(pl.multiple_of(c * (IDX_SUB * IDX_TK), IDX_SUB * IDX_TK), IDX_SUB)
            return carry

        gmax_ref[...] = jnp.full(gmax_ref.shape, -jnp.inf, F32)
        n_idx = (q0 + tq) // IDX_TK
        lax.fori_loop(0, n_idx // IDX_SUB, chunk_body, 0)
        for rest in range(1, IDX_SUB):
            @pl.when(n_idx % IDX_SUB == rest)
            def _(rest=rest):
                index_chunk(pl.multiple_of((n_idx - rest) * IDX_TK, IDX_TK), rest)

        def fill_body(f, carry):
            off = pl.multiple_of(q0 + tq + f * tq, tq)
            score_ref[pl.ds(off, tq), :] = jnp.full((tq, tq), -jnp.inf, F32)
            return carry

        lax.fori_loop(0, (n_blk * CNT_BLK - (q0 + tq)) // tq, fill_body, 0)

        t_row = q0 + lax.broadcasted_iota(I32, (SUBLANES, tq), 1)
        kq = jnp.minimum(topk, t_row + 1)

        g = gmax_ref[...].reshape(IDX_TK // SUBLANES, SUBLANES, tq)
        hi0 = _sublane_all(g.max(axis=0), jnp.maximum)
        lo0 = _sublane_all(jnp.where(g == -jnp.inf, jnp.inf, g).min(axis=0), jnp.minimum)

        def bis_cond(st):
            it, _, _, _, _, done = st
            return jnp.logical_and(it < BISECT_CAP, done == 0)

        def bis_body(st):
            it, lo, hi, cnt, fin, _ = st
            mid = 0.5 * lo + 0.5 * hi
            stuck = (mid <= lo) | (mid >= hi)
            cand = jnp.where(stuck, hi, mid)
            tot = _count_keys(score_ref, n_blk, lambda blk, off: blk >= cand[None])
            take = (tot >= kq) & (fin == 0)
            lo = jnp.where(take, cand, lo)
            cnt = jnp.where(take, tot, cnt)
            hi = jnp.where(take | (fin != 0), hi, cand)
            fin = jnp.where(stuck, 1, fin)
            settled = (cnt == kq) | (fin != 0)
            done = (jnp.min(jnp.where(settled, 1.0, 0.0)) > 0.5).astype(I32)
            return it + 1, lo, hi, cnt, fin, done

        empty = lo0 > hi0
        _, thr, _, cnt, _, _ = lax.while_loop(
            bis_cond, bis_body,
            (jnp.int32(0), lo0, hi0, jnp.where(empty, 0, -1), jnp.where(empty, 1, 0), jnp.int32(0)))
        thr_ref[...] = thr
        cnt = lax.cond(jnp.min(cnt.astype(F32)) < 0.0,
                       lambda: jnp.where(cnt < 0, _count_keys(
                           score_ref, n_blk, lambda blk, off: blk >= thr[None]), cnt),
                       lambda: cnt)

        @pl.when(jnp.max(jnp.where(cnt > kq, 1.0, 0.0)) > 0.5)
        def _break_ties():
            def row_idx(off):
                return off + lax.broadcasted_iota(I32, (CNT_BLK, tq), 0).reshape(
                    CNT_BLK // SUBLANES, SUBLANES, tq)

            above = _count_keys(score_ref, n_blk, lambda blk, off: blk > thr[None])
            need = kq - above

            def cut_body(b, p):
                c = p + lax.shift_left(jnp.int32(1), idx_bits - 1 - b)
                below = _count_keys(
                    score_ref, n_blk, lambda blk, off: (blk == thr[None]) & (row_idx(off) < c[None]))
                return jnp.where(below < need, c, p)

            p = lax.fori_loop(0, idx_bits, cut_body, jnp.zeros((SUBLANES, tq), I32))

            def drop_body(it, carry):
                off = pl.multiple_of(it * CNT_BLK, CNT_BLK)
                blk = score_ref[pl.ds(off, CNT_BLK), :].reshape(CNT_BLK // SUBLANES, SUBLANES, tq)
                new = jnp.where((blk == thr[None]) & (row_idx(off) > p[None]), -jnp.inf, blk)
                score_ref[pl.ds(off, CNT_BLK), :] = new.reshape(CNT_BLK, tq)
                return carry

            lax.fori_loop(0, n_blk, drop_body, 0)

    k0 = pl.multiple_of(j * tk, tk)
    nl = tk // LANES
    sel_t = score_ref[pl.ds(k0, tk), :] >= thr_ref[0:1, :]
    bias = jnp.transpose(jnp.where(sel_t, 0.0, NEG))
    tile_dist = jnp.full((tq, LANES), k0 - q0, I32).astype(F32)
    ones_col = (lax.broadcasted_iota(I32, (tk, LANES), 1) == 0).astype(BF16)

    def logits(h):
        s = bias + jnp.dot(q_ref[:, h * hw:(h + 1) * hw], kt_ref[h * hw:(h + 1) * hw, :],
                           preferred_element_type=F32)
        return s, jnp.max(s, axis=-1, keepdims=True)

    def probs(h, s, s_max):
        shift = tile_dist * alibi_c[h]
        m_prev = m_ref[h]
        m_new = jnp.maximum(m_prev, s_max + shift)
        m_ref[h] = m_new
        p = jnp.exp2(s - _tile_lanes(m_new - shift, nl)).astype(BF16)
        return p, jnp.exp2(m_prev - m_new)

    def accumulate(h, p, alpha):
        v_aug = jnp.concatenate([v_ref[:, h * HEAD_DIM:(h + 1) * HEAD_DIM], ones_col], axis=1)
        acc_ref[h] = _tile_lanes(alpha, 2) * acc_ref[h] + jnp.dot(
            p, v_aug, preferred_element_type=F32)

    st_a, st_b = {}, {}
    for step in range(ATTN_HEADS + 2):
        if 0 <= step - 2 < ATTN_HEADS:
            accumulate(step - 2, *st_b.pop(step - 2))
        if step < ATTN_HEADS:
            st_a[step] = logits(step)
        if 0 <= step - 1 < ATTN_HEADS:
            st_b[step - 1] = probs(step - 1, *st_a.pop(step - 1))

    @pl.when(j == n_kv - 1)
    def _finish():
        for h in range(ATTN_HEADS):
            acc = acc_ref[h]
            o_ref[:, h * HEAD_DIM:(h + 1) * HEAD_DIM] = (
                acc[:, :HEAD_DIM] / acc[:, HEAD_DIM:HEAD_DIM + 1]).astype(o_ref.dtype)


def _alibi_pieces():
    out = []
    for h in range(ATTN_HEADS):
        rest = np.float32(LOG2E * 2.0 ** (-8.0 * (h + 1) / ATTN_HEADS))
        pieces = []
        for _ in range(N_PIECES):
            piece = np.float32(rest.astype(BF16))
            pieces.append(piece)
            rest = np.float32(rest - piece)
        out.append(pieces)
    return np.asarray(out, np.float32)


def _alibi_tables():
    pieces = _alibi_pieces()
    n = N_PIECES
    assert 4 * n <= AUG
    cpos = np.zeros((ATTN_HEADS, AUG), np.float32)
    cneg = np.zeros((ATTN_HEADS, AUG, 1), np.float32)
    cpos[:, 0 * n:1 * n] = pieces
    cpos[:, 1 * n:2 * n] = pieces
    cneg[:, 2 * n:3 * n, 0] = -pieces
    cneg[:, 3 * n:4 * n, 0] = -pieces
    alibi_c = tuple(float(p.sum(dtype=np.float32)) for p in pieces)
    return jnp.asarray(cpos), jnp.asarray(cneg), alibi_c


def _attention(qit, wit, ki, q_aug, kt_aug, v, alibi_c):
    s = q_aug.shape[0]
    tq, tk = ATT_TQ, ATT_TK
    hw = HEAD_DIM + AUG
    assert s % CNT_BLK == 0 and CNT_BLK % tk == 0 and CNT_BLK % tq == 0 and tq % LANES == 0
    assert tq & (tq - 1) == 0 and tk & (tk - 1) == 0 and PROJ_TM % tq == 0
    topk = min(TOPK_MAX, s // 4)
    idx_bits = max(1, (s - 1).bit_length())
    assert IDX_TK >= topk and tq % IDX_TK == 0

    steps = [(i, j) for i in range(s // tq) for j in range(((i + 1) * tq + tk - 1) // tk)]
    itab = jnp.asarray(np.array([p[0] for p in steps], np.int32))
    jtab = jnp.asarray(np.array([p[1] for p in steps], np.int32))

    grid_spec = pltpu.PrefetchScalarGridSpec(
        num_scalar_prefetch=2,
        grid=(len(steps),),
        in_specs=[pl.BlockSpec((1, IDX_DIM, IDX_HEADS * tq), lambda t, it, jt: (it[t], 0, 0)),
                  pl.BlockSpec((IDX_HEADS, tq), lambda t, it, jt: (0, it[t])),
                  pl.BlockSpec((s, IDX_DIM), lambda t, it, jt: (0, 0)),
                  pl.BlockSpec((tq, ATTN_HEADS * hw), lambda t, it, jt: (it[t], 0)),
                  pl.BlockSpec((ATTN_HEADS * hw, tk), lambda t, it, jt: (0, jt[t])),
                  pl.BlockSpec((tk, ATTN_WIDTH), lambda t, it, jt: (jt[t], 0))],
        out_specs=pl.BlockSpec((tq, ATTN_WIDTH), lambda t, it, jt: (it[t], 0)),
        scratch_shapes=[pltpu.VMEM((s, tq), F32),
                        pltpu.VMEM((IDX_SUB * IDX_TK, IDX_HEADS * tq), F32),
                        pltpu.VMEM((IDX_TK, tq), F32),
                        pltpu.VMEM((SUBLANES, tq), F32),
                        pltpu.VMEM((ATTN_HEADS, tq, LANES), F32),
                        pltpu.VMEM((ATTN_HEADS, tq, 2 * HEAD_DIM), F32)])
    return pl.pallas_call(
        functools.partial(_attn_kernel, alibi_c=alibi_c, topk=topk, idx_bits=idx_bits),
        grid_spec=grid_spec,
        out_shape=jax.ShapeDtypeStruct((s, ATTN_WIDTH), BF16),
        compiler_params=_cparams(("arbitrary",)),
        name="dsa_attention",
    )(itab, jtab, qit, wit, ki, q_aug, kt_aug, v)


def _merge_kernel(ya_ref, yb_ref, sg_a_ref, sg_b_ref, x_ref, gt_ref, wa_ref, wb_ref, wo_ref, o_ref):
    a = jnp.dot(ya_ref[...], wa_ref[...], preferred_element_type=F32)
    b = jnp.dot(yb_ref[...], wb_ref[...], preferred_element_type=F32)
    merged = sg_a_ref[...].astype(F32) * a + sg_b_ref[...].astype(F32) * b
    o_ref[...] = x_ref[...] + gt_ref[...] * jnp.dot(
        merged.astype(BF16), wo_ref[...], preferred_element_type=F32)


def _merge(ya, yb, sg, x, gt, wa, wb, wo):
    s, d = x.shape
    tm = MERGE_TM
    wdt = ya.shape[1]
    const = lambda i: (0, 0)
    return pl.pallas_call(
        _merge_kernel,
        grid=(s // tm,),
        in_specs=[pl.BlockSpec((tm, wdt), lambda i: (i, 0)),
                  pl.BlockSpec((tm, wdt), lambda i: (i, 0)),
                  pl.BlockSpec((tm, d), lambda i: (i, 0)),
                  pl.BlockSpec((tm, d), lambda i: (i, 1)),
                  pl.BlockSpec((tm, d), lambda i: (i, 0)),
                  pl.BlockSpec((1, d), const),
                  pl.BlockSpec((wdt, d), const, pipeline_mode=pl.Buffered(1)),
                  pl.BlockSpec((wdt, d), const, pipeline_mode=pl.Buffered(1)),
                  pl.BlockSpec((d, d), const, pipeline_mode=pl.Buffered(1))],
        out_specs=pl.BlockSpec((tm, d), lambda i: (i, 0)),
        out_shape=jax.ShapeDtypeStruct((s, d), F32),
        compiler_params=_cparams(("parallel",)),
        name="merge",
    )(ya, yb, sg, sg, x, gt, wa, wb, wo)


def _pad_cols(w, n):
    return jnp.pad(w, ((0, 0), (0, n - w.shape[1])))


def _layer(x, c, w_ada, b_ada, g_norm1, w1_gate, w1_up, w1_down, g_norm2, w_in, g_sgu, w_spatial,
           b_spatial, g_q, g_k, g_kidx, b_kidx, w_branch_a, w_branch_b, w_out, g_norm3,
           w2_gate, w2_up, w2_down):
    s, d = x.shape

    ada = _ada(c.reshape(d, 1), w_ada, b_ada.reshape(1, -1))
    sh1, sc1, gt1, sh2, sc2, gt2, sh3, sc3, gt3 = [ada[:, k * d:(k + 1) * d] for k in range(N_ADA)]

    def ffn_weights(wg, wu, wd):
        return wg.astype(BF16), wu.astype(BF16), wd.astype(BF16)

    x1 = _ffn(x, g_norm1.reshape(1, d), sh1, sc1, gt1, *ffn_weights(w1_gate, w1_up, w1_down))

    n2 = _normmod(x1, g_norm2.reshape(1, d), sh2, sc2)
    wi_b = w_in.astype(BF16)
    o_u, o_v, o_q = 0, SGU_WIDTH, 2 * SGU_WIDTH
    o_k, o_vv = o_q + ATTN_WIDTH, o_q + 2 * ATTN_WIDTH
    o_qi = o_vv + ATTN_WIDTH
    o_ki = o_qi + IDX_HEADS * IDX_DIM
    o_g = o_ki + IDX_DIM + IDX_HEADS

    ug = _proj(n2, wi_b[:, o_u:o_v], "gelu")
    vn = _proj(n2, wi_b[:, o_v:o_q], "gelu_gnorm", gain=g_sgu.reshape(1, SGU_WIDTH))
    vv = _proj(n2, wi_b[:, o_vv:o_qi], "none")
    sg = _proj(n2, wi_b[:, o_g:], "sigmoid")
    ki, wit = _kidx(n2, _pad_cols(wi_b[:, o_ki:o_g], LANES),
                    g_kidx.reshape(1, IDX_DIM), b_kidx.reshape(1, IDX_DIM))

    cpos, cneg, alibi_c = _alibi_tables()
    hw = HEAD_DIM + AUG
    row = lambda i: (0, 0)
    q_aug = _attn_proj(
        functools.partial(_attn_q_kernel, post_scale=HEAD_DIM ** -0.5 * LOG2E),
        n2, wi_b[:, o_q:o_k], [jnp.tile(g_q, ATTN_HEADS).reshape(1, -1), cpos],
        [pl.BlockSpec((1, ATTN_WIDTH), row), pl.BlockSpec((ATTN_HEADS, AUG), row)],
        jax.ShapeDtypeStruct((s, ATTN_HEADS * hw), BF16),
        pl.BlockSpec((PROJ_TM, ATTN_HEADS * hw), lambda i: (i, 0)), "proj_attn_q")
    kt_aug = _attn_proj(
        _attn_kt_kernel, n2, wi_b[:, o_k:o_vv], [jnp.tile(g_k, ATTN_HEADS).reshape(1, -1), cneg],
        [pl.BlockSpec((1, ATTN_WIDTH), row), pl.BlockSpec((ATTN_HEADS, AUG, 1), lambda i: (0, 0, 0))],
        jax.ShapeDtypeStruct((ATTN_HEADS * hw, s), BF16),
        pl.BlockSpec((ATTN_HEADS * hw, PROJ_TM), lambda i: (0, i)), "proj_attn_kt")
    qit = _attn_proj(
        _idx_q_kernel, n2, wi_b[:, o_qi:o_ki], [], [],
        jax.ShapeDtypeStruct((s // ATT_TQ, IDX_DIM, IDX_HEADS * ATT_TQ), BF16),
        pl.BlockSpec((PROJ_TM // ATT_TQ, IDX_DIM, IDX_HEADS * ATT_TQ), lambda i: (i, 0, 0)), "proj_idx_q")

    y_a = _sgu(ug, vn, w_spatial, jnp.transpose(b_spatial))
    y_b = _attention(qit, wit, ki, q_aug, kt_aug, vv, alibi_c)

    x2 = _merge(y_a, y_b, sg, x1, gt2, w_branch_a.astype(BF16), w_branch_b.astype(BF16),
                w_out.astype(BF16))

    return _ffn(x2, g_norm3.reshape(1, d), sh3, sc3, gt3, *ffn_weights(w2_gate, w2_up, w2_down))


def kernel(x, c, w_ada, b_ada, g_norm1, w1_gate, w1_up, w1_down, g_norm2, w_in, g_sgu, w_spatial,
           b_spatial, g_q, g_k, g_kidx, b_kidx, w_branch_a, w_branch_b, w_out, g_norm3,
           w2_gate, w2_up, w2_down):
    batch, depth = x.shape[0], w_ada.shape[0]
    outs = []
    for b in range(batch):
        xb = x[b]
        for l in range(depth):
            xb = _layer(xb, c[b], w_ada[l], b_ada[l], g_norm1[l], w1_gate[l], w1_up[l], w1_down[l],
                        g_norm2[l], w_in[l], g_sgu[l], w_spatial[l], b_spatial[l], g_q[l], g_k[l],
                        g_kidx[l], b_kidx[l], w_branch_a[l], w_branch_b[l], w_out[l], g_norm3[l],
                        w2_gate[l], w2_up[l], w2_down[l])
        outs.append(xb[None])
    return outs[0] if batch == 1 else jnp.concatenate(outs)
```

```python
import functools

import numpy as np
import jax
import jax.numpy as jnp
from jax import lax
from jax.experimental import pallas as pl
from jax.experimental.pallas import tpu as pltpu

F32 = jnp.float32
BF16 = jnp.bfloat16
I32 = jnp.int32

CHUNK = 128
SGU_GROUPS = 8
SGU_WIDTH = 1024
ATTN_HEADS = 8
HEAD_DIM = 128
ATTN_WIDTH = ATTN_HEADS * HEAD_DIM
IDX_HEADS = 16
IDX_DIM = 64
TOPK_MAX = 256
N_ADA = 9
EPS = 1e-6
NEG = -1e30

LANES = 128
SUBLANES = 8
VMEM_LIMIT = 56 * 1024 * 1024
FFN_VMEM_LIMIT = 60 * 1024 * 1024

FFN_TM = 512
FFN_TF = 1024
PROJ_TM = 1024
PROJ_TN = 1024
SGU_TM = 512
MERGE_TM = 512
ATT_TQ = 256
ATT_TK = 1024
IDX_TK = 256
IDX_SUB = 4
CNT_BLK = 1024
CNT_ACC = 8
BISECT_CAP = 512
AUG = 128
N_PIECES = 3
LOG2E = 1.4426950408889634


def _cparams(sem):
    return pltpu.CompilerParams(dimension_semantics=sem, vmem_limit_bytes=VMEM_LIMIT)


def _gelu_tanh(x):
    return 0.5 * x * (1.0 + jnp.tanh(0.7978845608028654 * (x + 0.044715 * (x * x * x))))


def _rms_mod(x, g, sh, sc):
    ms = jnp.mean(x * x, axis=-1, keepdims=True)
    y = (x * lax.rsqrt(ms + EPS)) * g
    return y * (1.0 + sc) + sh


def _ada_kernel(c_ref, w_ref, b_ref, o_ref, sb_ref):
    d = w_ref.shape[0]
    tn = w_ref.shape[1]

    @pl.when(pl.program_id(0) == 0)
    def _():
        cc = c_ref[...]
        sb_ref[...] = jnp.broadcast_to(cc * jax.nn.sigmoid(cc), (d, LANES))

    sb = sb_ref[...]
    for cb in range(tn // LANES):
        cols = slice(cb * LANES, (cb + 1) * LANES)
        prod = (w_ref[:, cols] * sb).reshape(d // SUBLANES, SUBLANES, LANES).sum(axis=0)
        o_ref[:, cols] = prod.sum(axis=0, keepdims=True) + b_ref[:, cols]


def _ada(c_col, w, b):
    d, n = w.shape
    tn = 1024
    return pl.pallas_call(
        _ada_kernel,
        grid=(n // tn,),
        in_specs=[pl.BlockSpec((d, 1), lambda j: (0, 0)),
                  pl.BlockSpec((d, tn), lambda j: (0, j)),
                  pl.BlockSpec((1, tn), lambda j: (0, j))],
        out_specs=pl.BlockSpec((1, tn), lambda j: (0, j)),
        out_shape=jax.ShapeDtypeStruct((1, n), F32),
        scratch_shapes=[pltpu.VMEM((d, LANES), F32)],
        compiler_params=_cparams(("arbitrary",)),
        name="ada",
    )(c_col, w, b)


def _gate_up(n, wg_ref, wu_ref):
    g = jnp.dot(n, wg_ref[...], preferred_element_type=F32)
    u = jnp.dot(n, wu_ref[...], preferred_element_type=F32)
    return ((g * jax.nn.sigmoid(g)) * u).astype(BF16)


def _ffn_kernel(x_ref, g_ref, sh_ref, sc_ref, gt_ref, wg_ref, wu_ref, wd_ref, *rest, has_tail, has_next):
    rest = list(rest)
    tail = [rest.pop(0) for _ in range(3)] if has_tail else []
    nxt = [rest.pop(0) for _ in range(3)] if has_next else []
    o_ref = rest.pop(0)
    nxt_ref = rest.pop(0) if has_next else None
    n_ref, h_ref = rest
    j = pl.program_id(1)
    last = pl.num_programs(1) - 1

    @pl.when(j == 0)
    def _():
        n_ref[...] = _rms_mod(x_ref[...], g_ref[...], sh_ref[...], sc_ref[...]).astype(BF16)
        o_ref[...] = jnp.zeros(o_ref.shape, F32)
        h_ref[0] = _gate_up(n_ref[...], wg_ref, wu_ref)

    @pl.when((j > 0) & (j < last))
    def _():
        o_ref[...] += jnp.dot(h_ref[(j - 1) % 2], wd_ref[...], preferred_element_type=F32)
        h_ref[j % 2] = _gate_up(n_ref[...], wg_ref, wu_ref)

    @pl.when(j == last)
    def _():
        y = o_ref[...] + jnp.dot(h_ref[(j - 1) % 2], wd_ref[...], preferred_element_type=F32)
        if tail:
            wg_t, wu_t, wd_t = tail
            y = y + jnp.dot(_gate_up(n_ref[...], wg_t, wu_t), wd_t[...], preferred_element_type=F32)
        out = x_ref[...] + (0.5 * gt_ref[...]) * y
        o_ref[...] = out
        if has_next:
            g2, sh2, sc2 = nxt
            nxt_ref[...] = _rms_mod(out, g2[...], sh2[...], sc2[...]).astype(nxt_ref.dtype)


def _ffn(x, g, sh, sc, gt, wg, wu, wd, next_norm=None):
    s, d = x.shape
    f = wg.shape[1]
    tm = min(FFN_TM, s)
    n_main = f // FFN_TF
    f_main = n_main * FFN_TF
    row = pl.BlockSpec((1, d), lambda i, j: (0, 0))
    assert n_main >= 1
    in_specs = [pl.BlockSpec((tm, d), lambda i, j: (i, 0)), row, row, row, row,
                pl.BlockSpec((d, FFN_TF), lambda i, j: (0, jnp.minimum(j, n_main - 1))),
                pl.BlockSpec((d, FFN_TF), lambda i, j: (0, jnp.minimum(j, n_main - 1))),
                pl.BlockSpec((FFN_TF, d), lambda i, j: (jnp.maximum(j - 1, 0), 0))]
    args = [x, g, sh, sc, gt, wg, wu, wd]
    if f_main < f:
        assert (f - f_main) % LANES == 0
        args += [wg[:, f_main:], wu[:, f_main:], wd[f_main:]]
        once = dict(pipeline_mode=pl.Buffered(1))
        in_specs += [pl.BlockSpec((d, f - f_main), lambda i, j: (0, 0), **once),
                     pl.BlockSpec((d, f - f_main), lambda i, j: (0, 0), **once),
                     pl.BlockSpec((f - f_main, d), lambda i, j: (0, 0), **once)]
    out_specs = pl.BlockSpec((tm, d), lambda i, j: (i, 0))
    out_shape = jax.ShapeDtypeStruct((s, d), F32)
    if next_norm is not None:
        args += list(next_norm)
        in_specs += [row, row, row]
        out_specs = [out_specs, pl.BlockSpec((tm, d), lambda i, j: (i, 0))]
        out_shape = [out_shape, jax.ShapeDtypeStruct((s, d), BF16)]
    return pl.pallas_call(
        functools.partial(_ffn_kernel, has_tail=f_main < f, has_next=next_norm is not None),
        grid=(s // tm, n_main + 1),
        in_specs=in_specs,
        out_specs=out_specs,
        out_shape=out_shape,
        scratch_shapes=[pltpu.VMEM((tm, d), BF16), pltpu.VMEM((2, tm, FFN_TF), BF16)],
        compiler_params=pltpu.CompilerParams(dimension_semantics=("parallel", "arbitrary"),
                                             vmem_limit_bytes=FFN_VMEM_LIMIT),
        name="ffn",
    )(*args)


def _group_rms(z, gain, post_scale):
    outs = []
    for gidx in range(z.shape[1] // LANES):
        cols = slice(gidx * LANES, (gidx + 1) * LANES)
        zg = z[:, cols]
        ms = jnp.mean(zg * zg, axis=-1, keepdims=True)
        y = (zg * lax.rsqrt(ms + EPS)) * gain[:, cols]
        if post_scale != 1.0:
            y = y * post_scale
        outs.append(y)
    return jnp.concatenate(outs, axis=-1)


def _proj_kernel(n_ref, w_ref, *rest, mode):
    o_ref = rest[-1]
    z = jnp.dot(n_ref[...], w_ref[...], preferred_element_type=F32)
    if mode == "gelu":
        out = _gelu_tanh(z)
    elif mode == "gelu_gnorm":
        out = _group_rms(_gelu_tanh(z), rest[0][...], 1.0)
    elif mode == "sigmoid":
        out = jax.nn.sigmoid(z)
    else:
        assert mode == "none"
        out = z
    o_ref[...] = out.astype(o_ref.dtype)


def _proj(n, w, mode, gain=None):
    s, d = n.shape
    nout = w.shape[1]
    in_specs = [pl.BlockSpec((PROJ_TM, d), lambda i, j: (i, 0)),
                pl.BlockSpec((d, PROJ_TN), lambda i, j: (0, j))]
    args = [n, w]
    if gain is not None:
        in_specs.append(pl.BlockSpec((1, PROJ_TN), lambda i, j: (0, j)))
        args.append(gain)
    return pl.pallas_call(
        functools.partial(_proj_kernel, mode=mode),
        grid=(s // PROJ_TM, nout // PROJ_TN),
        in_specs=in_specs,
        out_specs=pl.BlockSpec((PROJ_TM, PROJ_TN), lambda i, j: (i, j)),
        out_shape=jax.ShapeDtypeStruct((s, nout), BF16),
        compiler_params=_cparams(("parallel", "arbitrary")),
        name="proj_" + mode,
    )(*args)


def _split_hi_lo(pos):
    lo = pos & (LANES - 1)
    return pos - lo, lo


def _piece_slot(idx, k):
    return (idx >= k * N_PIECES) & (idx < (k + 1) * N_PIECES)


def _attn_q_kernel(n_ref, w_ref, gain_ref, cpos_ref, o_ref, *, post_scale):
    tm = n_ref.shape[0]
    y = _group_rms(jnp.dot(n_ref[...], w_ref[...], preferred_element_type=F32),
                   gain_ref[...], post_scale)
    col = lax.broadcasted_iota(I32, (tm, AUG), 1)
    t_hi, t_lo = _split_hi_lo(lax.broadcasted_iota(I32, (tm, AUG), 0) & (ATT_TQ - 1))
    tab = jnp.where(_piece_slot(col, 2), t_hi, jnp.where(_piece_slot(col, 3), t_lo, 0)).astype(F32)
    hw = HEAD_DIM + AUG
    for h in range(ATTN_HEADS):
        o_ref[:, h * hw:h * hw + HEAD_DIM] = y[:, h * HEAD_DIM:(h + 1) * HEAD_DIM].astype(o_ref.dtype)
        o_ref[:, h * hw + HEAD_DIM:(h + 1) * hw] = (tab + cpos_ref[h:h + 1, :]).astype(o_ref.dtype)


def _attn_kt_kernel(n_ref, w_ref, gain_ref, cneg_ref, o_ref):
    tm = n_ref.shape[0]
    y = _group_rms(jnp.dot(n_ref[...], w_ref[...], preferred_element_type=F32), gain_ref[...], 1.0)
    yt = jnp.transpose(y)
    row = lax.broadcasted_iota(I32, (AUG, tm), 0)
    pos = pl.program_id(0) * tm + lax.broadcasted_iota(I32, (AUG, tm), 1)
    u_hi, u_lo = _split_hi_lo(pos & (ATT_TK - 1))
    tab = jnp.where(_piece_slot(row, 0), u_hi, jnp.where(_piece_slot(row, 1), u_lo, 0)).astype(F32)
    hw = HEAD_DIM + AUG
    for h in range(ATTN_HEADS):
        o_ref[h * hw:h * hw + HEAD_DIM, :] = yt[h * HEAD_DIM:(h + 1) * HEAD_DIM, :].astype(o_ref.dtype)
        o_ref[h * hw + HEAD_DIM:(h + 1) * hw, :] = (tab + cneg_ref[h]).astype(o_ref.dtype)


def _idx_q_kernel(n_ref, w_ref, o_ref):
    zt = jnp.transpose(jnp.dot(n_ref[...], w_ref[...], preferred_element_type=F32))
    for qt in range(n_ref.shape[0] // ATT_TQ):
        for h in range(IDX_HEADS):
            o_ref[qt, :, h * ATT_TQ:(h + 1) * ATT_TQ] = zt[
                h * IDX_DIM:(h + 1) * IDX_DIM, qt * ATT_TQ:(qt + 1) * ATT_TQ].astype(o_ref.dtype)


def _attn_proj(kernel, n, w, extra, extra_specs, out_shape, out_spec, name):
    s, d = n.shape
    return pl.pallas_call(
        kernel,
        grid=(s // PROJ_TM,),
        in_specs=[pl.BlockSpec((PROJ_TM, d), lambda i: (i, 0)),
                  pl.BlockSpec(w.shape, lambda i: (0, 0))] + extra_specs,
        out_specs=out_spec,
        out_shape=out_shape,
        compiler_params=_cparams(("parallel",)),
        name=name,
    )(n, w, *extra)


def _kidx_kernel(n_ref, w_ref, g_ref, b_ref, ki_ref, wit_ref):
    z = jnp.dot(n_ref[...], w_ref[...], preferred_element_type=F32)
    ki = z[:, :IDX_DIM]
    mu = jnp.mean(ki, axis=-1, keepdims=True)
    var = jnp.mean(jnp.square(ki - mu), axis=-1, keepdims=True)
    y = (ki - mu) * lax.rsqrt(var + EPS)
    ki_ref[...] = (y * g_ref[...] + b_ref[...]).astype(ki_ref.dtype)
    wit_ref[...] = jnp.transpose(z)[IDX_DIM:IDX_DIM + IDX_HEADS, :] * (IDX_HEADS ** -0.5 * IDX_DIM ** -0.5)


def _kidx(n, w, g, b):
    s, d = n.shape
    tm = 512
    return pl.pallas_call(
        _kidx_kernel,
        grid=(s // tm,),
        in_specs=[pl.BlockSpec((tm, d), lambda i: (i, 0)),
                  pl.BlockSpec((d, LANES), lambda i: (0, 0)),
                  pl.BlockSpec((1, IDX_DIM), lambda i: (0, 0)),
                  pl.BlockSpec((1, IDX_DIM), lambda i: (0, 0))],
        out_specs=[pl.BlockSpec((tm, IDX_DIM), lambda i: (i, 0)),
                   pl.BlockSpec((IDX_HEADS, tm), lambda i: (0, i))],
        out_shape=[jax.ShapeDtypeStruct((s, IDX_DIM), BF16),
                   jax.ShapeDtypeStruct((IDX_HEADS, s), F32)],
        compiler_params=_cparams(("parallel",)),
        name="proj_kidx",
    )(n, w, g, b)


def _sgu_kernel(u_ref, v_ref, ws_ref, bt_ref, o_ref):
    tm = u_ref.shape[0]
    r = lax.broadcasted_iota(I32, (CHUNK, CHUNK), 0)
    c = lax.broadcasted_iota(I32, (CHUNK, CHUNK), 1)
    causal = c <= r
    for g in range(SGU_GROUPS):
        cols = slice(g * LANES, (g + 1) * LANES)
        w = jnp.where(causal, ws_ref[g], 0.0).astype(BF16)
        bcol = bt_ref[:, g:g + 1]
        for ch in range(tm // CHUNK):
            rows = slice(ch * CHUNK, (ch + 1) * CHUNK)
            sv = jnp.dot(w, v_ref[rows, cols], preferred_element_type=F32) + bcol
            o_ref[rows, cols] = (u_ref[rows, cols].astype(F32) * sv).astype(o_ref.dtype)


def _sgu(u, v, ws, bt):
    s, wdt = u.shape
    return pl.pallas_call(
        _sgu_kernel,
        grid=(s // SGU_TM,),
        in_specs=[pl.BlockSpec((SGU_TM, wdt), lambda i: (i, 0)),
                  pl.BlockSpec((SGU_TM, wdt), lambda i: (i, 0)),
                  pl.BlockSpec((SGU_GROUPS, CHUNK, CHUNK), lambda i: (0, 0, 0)),
                  pl.BlockSpec((CHUNK, SGU_GROUPS), lambda i: (0, 0))],
        out_specs=pl.BlockSpec((SGU_TM, wdt), lambda i: (i, 0)),
        out_shape=jax.ShapeDtypeStruct((s, wdt), BF16),
        compiler_params=_cparams(("parallel",)),
        name="sgu",
    )(u, v, ws, bt)


def _tile_lanes(x, n):
    return x if n == 1 else jnp.concatenate([x] * n, axis=1)


def _sublane_all(x, op):
    shift = SUBLANES // 2
    while shift:
        x = op(x, pltpu.roll(x, shift, 0))
        shift //= 2
    return x


def _count_keys(score_ref, n_blocks, pred):
    tq = score_ref.shape[1]

    def body(it, acc):
        off = pl.multiple_of(it * CNT_BLK, CNT_BLK)
        blk = score_ref[pl.ds(off, CNT_BLK), :].reshape(CNT_BLK // SUBLANES, SUBLANES, tq)
        ind = jnp.where(pred(blk, off), 1, 0)
        return acc + ind.reshape(-1, CNT_ACC, SUBLANES, tq).sum(axis=0)

    acc = lax.fori_loop(0, n_blocks, body, jnp.zeros((CNT_ACC, SUBLANES, tq), I32))
    tot = jnp.sum(acc.sum(axis=0).astype(F32), axis=0, keepdims=True)
    return jnp.broadcast_to(tot, (SUBLANES, tq)).astype(I32)


def _attn_kernel(itab_ref, jtab_ref, qit_ref, wit_ref, ki_ref, q_ref, kt_ref, v_ref, o_ref,
                 score_ref, r_ref, gmax_ref, thr_ref, m_ref, acc_ref, *, alibi_c, topk, idx_bits):
    i = itab_ref[pl.program_id(0)]
    j = jtab_ref[pl.program_id(0)]
    tq, tk = ATT_TQ, ATT_TK
    hw = HEAD_DIM + AUG
    q0 = i * tq
    n_kv = (q0 + tq + tk - 1) // tk
    n_blk = (q0 + tq + CNT_BLK - 1) // CNT_BLK

    @pl.when(j == 0)
    def _index_and_threshold():
        m_ref[...] = jnp.full(m_ref.shape, NEG, F32)
        acc_ref[...] = jnp.zeros(acc_ref.shape, F32)

        def index_chunk(off, n_sub):
            rows = n_sub * IDX_TK
            r_ref[0:rows, :] = jnp.dot(ki_ref[pl.ds(off, rows), :], qit_ref[0],
                                       preferred_element_type=F32)
            for sub in range(n_sub):
                r0 = sub * IDX_TK
                s_idx = off + r0 + lax.broadcasted_iota(I32, (IDX_TK, LANES), 0)
                for g in range(tq // LANES):
                    lanes = slice(g * LANES, (g + 1) * LANES)
                    t_idx = q0 + g * LANES + lax.broadcasted_iota(I32, (IDX_TK, LANES), 1)
                    acc = jnp.zeros((IDX_TK, LANES), F32)
                    for h in range(IDX_HEADS):
                        rr = r_ref[r0:r0 + IDX_TK, h * tq + g * LANES:h * tq + (g + 1) * LANES]
                        acc = acc + jnp.maximum(rr, 0.0) * wit_ref[h:h + 1, lanes]
                    score = jnp.where(s_idx <= t_idx, acc, -jnp.inf)
                    score_ref[pl.ds(pl.multiple_of(off + r0, IDX_TK), IDX_TK), lanes] = score
                    gmax_ref[:, lanes] = jnp.maximum(gmax_ref[:, lanes], score)

        def chunk_body(c, carry):
            index_chunk(pl.multiple_of(c * (IDX_SUB * IDX_TK), IDX_SUB * IDX_TK), IDX_SUB)
            return carry

        gmax_ref[...] = jnp.full(gmax_ref.shape, -jnp.inf, F32)
        n_idx = (q0 + tq) // IDX_TK
        lax.fori_loop(0, n_idx // IDX_SUB, chunk_body, 0)
        for rest in range(1, IDX_SUB):
            @pl.when(n_idx % IDX_SUB == rest)
            def _(rest=rest):
                index_chunk(pl.multiple_of((n_idx - rest) * IDX_TK, IDX_TK), rest)

        def fill_body(f, carry):
            off = pl.multiple_of(q0 + tq + f * tq, tq)
            score_ref[pl.ds(off, tq), :] = jnp.full((tq, tq), -jnp.inf, F32)
            return carry

        lax.fori_loop(0, (n_blk * CNT_BLK - (q0 + tq)) // tq, fill_body, 0)

        t_row = q0 + lax.broadcasted_iota(I32, (SUBLANES, tq), 1)
        kq = jnp.minimum(topk, t_row + 1)

        g = gmax_ref[...].reshape(IDX_TK // SUBLANES, SUBLANES, tq)
        hi0 = _sublane_all(g.max(axis=0), jnp.maximum)
        lo0 = _sublane_all(jnp.where(g == -jnp.inf, jnp.inf, g).min(axis=0), jnp.minimum)

        def bis_cond(st):
            it, _, _, _, _, done = st
            return jnp.logical_and(it < BISECT_CAP, done == 0)

        def bis_body(st):
            it, lo, hi, cnt, fin, _ = st
            mid = 0.5 * lo + 0.5 * hi
            stuck = (mid <= lo) | (mid >= hi)
            cand = jnp.where(stuck, hi, mid)
            tot = _count_keys(score_ref, n_blk, lambda blk, off: blk >= cand[None])
            take = (tot >= kq) & (fin == 0)
            lo = jnp.where(take, cand, lo)
            cnt = jnp.where(take, tot, cnt)
            hi = jnp.where(take | (fin != 0), hi, cand)
            fin = jnp.where(stuck, 1, fin)
            settled = (cnt == kq) | (fin != 0)
            done = (jnp.min(jnp.where(settled, 1.0, 0.0)) > 0.5).astype(I32)
            return it + 1, lo, hi, cnt, fin, done

        empty = lo0 > hi0
        _, thr, _, cnt, _, _ = lax.while_loop(
            bis_cond, bis_body,
            (jnp.int32(0), lo0, hi0, jnp.where(empty, 0, -1), jnp.where(empty, 1, 0), jnp.int32(0)))
        thr_ref[...] = thr
        cnt = lax.cond(jnp.min(cnt.astype(F32)) < 0.0,
                       lambda: jnp.where(cnt < 0, _count_keys(
                           score_ref, n_blk, lambda blk, off: blk >= thr[None]), cnt),
                       lambda: cnt)

        @pl.when(jnp.max(jnp.where(cnt > kq, 1.0, 0.0)) > 0.5)
        def _break_ties():
            def row_idx(off):
                return off + lax.broadcasted_iota(I32, (CNT_BLK, tq), 0).reshape(
                    CNT_BLK // SUBLANES, SUBLANES, tq)

            above = _count_keys(score_ref, n_blk, lambda blk, off: blk > thr[None])
            need = kq - above

            def cut_body(b, p):
                c = p + lax.shift_left(jnp.int32(1), idx_bits - 1 - b)
                below = _count_keys(
                    score_ref, n_blk, lambda blk, off: (blk == thr[None]) & (row_idx(off) < c[None]))
                return jnp.where(below < need, c, p)

            p = lax.fori_loop(0, idx_bits, cut_body, jnp.zeros((SUBLANES, tq), I32))

            def drop_body(it, carry):
                off = pl.multiple_of(it * CNT_BLK, CNT_BLK)
                blk = score_ref[pl.ds(off, CNT_BLK), :].reshape(CNT_BLK // SUBLANES, SUBLANES, tq)
                new = jnp.where((blk == thr[None]) & (row_idx(off) > p[None]), -jnp.inf, blk)
                score_ref[pl.ds(off, CNT_BLK), :] = new.reshape(CNT_BLK, tq)
                return carry

            lax.fori_loop(0, n_blk, drop_body, 0)

    k0 = pl.multiple_of(j * tk, tk)
    nl = tk // LANES
    sel_t = score_ref[pl.ds(k0, tk), :] >= thr_ref[0:1, :]
    bias = jnp.transpose(jnp.where(sel_t, 0.0, NEG))
    tile_dist = jnp.full((tq, LANES), k0 - q0, I32).astype(F32)
    ones_col = (lax.broadcasted_iota(I32, (tk, LANES), 1) == 0).astype(BF16)

    def logits(h):
        s = bias + jnp.dot(q_ref[:, h * hw:(h + 1) * hw], kt_ref[h * hw:(h + 1) * hw, :],
                           preferred_element_type=F32)
        return s, jnp.max(s, axis=-1, keepdims=True)

    def probs(h, s, s_max):
        shift = tile_dist * alibi_c[h]
        m_prev = m_ref[h]
        m_new = jnp.maximum(m_prev, s_max + shift)
        m_ref[h] = m_new
        p = jnp.exp2(s - _tile_lanes(m_new - shift, nl)).astype(BF16)
        return p, jnp.exp2(m_prev - m_new)

    def accumulate(h, p, alpha):
        v_aug = jnp.concatenate([v_ref[:, h * HEAD_DIM:(h + 1) * HEAD_DIM], ones_col], axis=1)
        acc_ref[h] = _tile_lanes(alpha, 2) * acc_ref[h] + jnp.dot(
            p, v_aug, preferred_element_type=F32)

    st_a, st_b = {}, {}
    for step in range(ATTN_HEADS + 2):
        if 0 <= step - 2 < ATTN_HEADS:
            accumulate(step - 2, *st_b.pop(step - 2))
        if step < ATTN_HEADS:
            st_a[step] = logits(step)
        if 0 <= step - 1 < ATTN_HEADS:
            st_b[step - 1] = probs(step - 1, *st_a.pop(step - 1))

    @pl.when(j == n_kv - 1)
    def _finish():
        for h in range(ATTN_HEADS):
            acc = acc_ref[h]
            o_ref[:, h * HEAD_DIM:(h + 1) * HEAD_DIM] = (
                acc[:, :HEAD_DIM] / acc[:, HEAD_DIM:HEAD_DIM + 1]).astype(o_ref.dtype)


def _alibi_pieces():
    out = []
    for h in range(ATTN_HEADS):
        rest = np.float32(LOG2E * 2.0 ** (-8.0 * (h + 1) / ATTN_HEADS))
        pieces = []
        for _ in range(N_PIECES):
            piece = np.float32(rest.astype(BF16))
            pieces.append(piece)
            rest = np.float32(rest - piece)
        out.append(pieces)
    return np.asarray(out, np.float32)


def _alibi_tables():
    pieces = _alibi_pieces()
    n = N_PIECES
    assert 4 * n <= AUG
    cpos = np.zeros((ATTN_HEADS, AUG), np.float32)
    cneg = np.zeros((ATTN_HEADS, AUG, 1), np.float32)
    cpos[:, 0 * n:1 * n] = pieces
    cpos[:, 1 * n:2 * n] = pieces
    cneg[:, 2 * n:3 * n, 0] = -pieces
    cneg[:, 3 * n:4 * n, 0] = -pieces
    alibi_c = tuple(float(p.sum(dtype=np.float32)) for p in pieces)
    return jnp.asarray(cpos), jnp.asarray(cneg), alibi_c


def _attention(qit, wit, ki, q_aug, kt_aug, v, alibi_c):
    s = q_aug.shape[0]
    tq, tk = ATT_TQ, ATT_TK
    hw = HEAD_DIM + AUG
    assert s % CNT_BLK == 0 and CNT_BLK % tk == 0 and CNT_BLK % tq == 0 and tq % LANES == 0
    assert tq & (tq - 1) == 0 and tk & (tk - 1) == 0 and PROJ_TM % tq == 0
    topk = min(TOPK_MAX, s // 4)
    idx_bits = max(1, (s - 1).bit_length())
    assert IDX_TK >= topk and tq % IDX_TK == 0

    steps = [(i, j) for i in range(s // tq) for j in range(((i + 1) * tq + tk - 1) // tk)]
    itab = jnp.asarray(np.array([p[0] for p in steps], np.int32))
    jtab = jnp.asarray(np.array([p[1] for p in steps], np.int32))

    grid_spec = pltpu.PrefetchScalarGridSpec(
        num_scalar_prefetch=2,
        grid=(len(steps),),
        in_specs=[pl.BlockSpec((1, IDX_DIM, IDX_HEADS * tq), lambda t, it, jt: (it[t], 0, 0)),
                  pl.BlockSpec((IDX_HEADS, tq), lambda t, it, jt: (0, it[t])),
                  pl.BlockSpec((s, IDX_DIM), lambda t, it, jt: (0, 0)),
                  pl.BlockSpec((tq, ATTN_HEADS * hw), lambda t, it, jt: (it[t], 0)),
                  pl.BlockSpec((ATTN_HEADS * hw, tk), lambda t, it, jt: (0, jt[t])),
                  pl.BlockSpec((tk, ATTN_WIDTH), lambda t, it, jt: (jt[t], 0))],
        out_specs=pl.BlockSpec((tq, ATTN_WIDTH), lambda t, it, jt: (it[t], 0)),
        scratch_shapes=[pltpu.VMEM((s, tq), F32),
                        pltpu.VMEM((IDX_SUB * IDX_TK, IDX_HEADS * tq), F32),
                        pltpu.VMEM((IDX_TK, tq), F32),
                        pltpu.VMEM((SUBLANES, tq), F32),
                        pltpu.VMEM((ATTN_HEADS, tq, LANES), F32),
                        pltpu.VMEM((ATTN_HEADS, tq, 2 * HEAD_DIM), F32)])
    return pl.pallas_call(
        functools.partial(_attn_kernel, alibi_c=alibi_c, topk=topk, idx_bits=idx_bits),
        grid_spec=grid_spec,
        out_shape=jax.ShapeDtypeStruct((s, ATTN_WIDTH), BF16),
        compiler_params=_cparams(("arbitrary",)),
        name="dsa_attention",
    )(itab, jtab, qit, wit, ki, q_aug, kt_aug, v)


def _merge_kernel(ya_ref, yb_ref, sg_a_ref, sg_b_ref, x_ref, gt_ref, wa_ref, wb_ref, wo_ref, o_ref):
    a = jnp.dot(ya_ref[...], wa_ref[...], preferred_element_type=F32)
    b = jnp.dot(yb_ref[...], wb_ref[...], preferred_element_type=F32)
    merged = sg_a_ref[...].astype(F32) * a + sg_b_ref[...].astype(F32) * b
    o_ref[...] = x_ref[...] + gt_ref[...] * jnp.dot(
        merged.astype(BF16), wo_ref[...], preferred_element_type=F32)


def _merge(ya, yb, sg, x, gt, wa, wb, wo):
    s, d = x.shape
    tm = MERGE_TM
    wdt = ya.shape[1]
    const = lambda i: (0, 0)
    return pl.pallas_call(
        _merge_kernel,
        grid=(s // tm,),
        in_specs=[pl.BlockSpec((tm, wdt), lambda i: (i, 0)),
                  pl.BlockSpec((tm, wdt), lambda i: (i, 0)),
                  pl.BlockSpec((tm, d), lambda i: (i, 0)),
                  pl.BlockSpec((tm, d), lambda i: (i, 1)),
                  pl.BlockSpec((tm, d), lambda i: (i, 0)),
                  pl.BlockSpec((1, d), const),
                  pl.BlockSpec((wdt, d), const, pipeline_mode=pl.Buffered(1)),
                  pl.BlockSpec((wdt, d), const, pipeline_mode=pl.Buffered(1)),
                  pl.BlockSpec((d, d), const, pipeline_mode=pl.Buffered(1))],
        out_specs=pl.BlockSpec((tm, d), lambda i: (i, 0)),
        out_shape=jax.ShapeDtypeStruct((s, d), F32),
        compiler_params=_cparams(("parallel",)),
        name="merge",
    )(ya, yb, sg, sg, x, gt, wa, wb, wo)


def _pad_cols(w, n):
    return jnp.pad(w, ((0, 0), (0, n - w.shape[1])))


def _layer(x, c, w_ada, b_ada, g_norm1, w1_gate, w1_up, w1_down, g_norm2, w_in, g_sgu, w_spatial,
           b_spatial, g_q, g_k, g_kidx, b_kidx, w_branch_a, w_branch_b, w_out, g_norm3,
           w2_gate, w2_up, w2_down):
    s, d = x.shape

    ada = _ada(c.reshape(d, 1), w_ada, b_ada.reshape(1, -1))
    sh1, sc1, gt1, sh2, sc2, gt2, sh3, sc3, gt3 = [ada[:, k * d:(k + 1) * d] for k in range(N_ADA)]

    def ffn_weights(wg, wu, wd):
        return wg.astype(BF16), wu.astype(BF16), wd.astype(BF16)

    x1, n2 = _ffn(x, g_norm1.reshape(1, d), sh1, sc1, gt1, *ffn_weights(w1_gate, w1_up, w1_down),
                  next_norm=(g_norm2.reshape(1, d), sh2, sc2))

    wi_b = w_in.astype(BF16)
    o_u, o_v, o_q = 0, SGU_WIDTH, 2 * SGU_WIDTH
    o_k, o_vv = o_q + ATTN_WIDTH, o_q + 2 * ATTN_WIDTH
    o_qi = o_vv + ATTN_WIDTH
    o_ki = o_qi + IDX_HEADS * IDX_DIM
    o_g = o_ki + IDX_DIM + IDX_HEADS

    ug = _proj(n2, wi_b[:, o_u:o_v], "gelu")
    vn = _proj(n2, wi_b[:, o_v:o_q], "gelu_gnorm", gain=g_sgu.reshape(1, SGU_WIDTH))
    vv = _proj(n2, wi_b[:, o_vv:o_qi], "none")
    sg = _proj(n2, wi_b[:, o_g:], "sigmoid")
    ki, wit = _kidx(n2, _pad_cols(wi_b[:, o_ki:o_g], LANES),
                    g_kidx.reshape(1, IDX_DIM), b_kidx.reshape(1, IDX_DIM))

    cpos, cneg, alibi_c = _alibi_tables()
    hw = HEAD_DIM + AUG
    row = lambda i: (0, 0)
    q_aug = _attn_proj(
        functools.partial(_attn_q_kernel, post_scale=HEAD_DIM ** -0.5 * LOG2E),
        n2, wi_b[:, o_q:o_k], [jnp.tile(g_q, ATTN_HEADS).reshape(1, -1), cpos],
        [pl.BlockSpec((1, ATTN_WIDTH), row), pl.BlockSpec((ATTN_HEADS, AUG), row)],
        jax.ShapeDtypeStruct((s, ATTN_HEADS * hw), BF16),
        pl.BlockSpec((PROJ_TM, ATTN_HEADS * hw), lambda i: (i, 0)), "proj_attn_q")
    kt_aug = _attn_proj(
        _attn_kt_kernel, n2, wi_b[:, o_k:o_vv], [jnp.tile(g_k, ATTN_HEADS).reshape(1, -1), cneg],
        [pl.BlockSpec((1, ATTN_WIDTH), row), pl.BlockSpec((ATTN_HEADS, AUG, 1), lambda i: (0, 0, 0))],
        jax.ShapeDtypeStruct((ATTN_HEADS * hw, s), BF16),
        pl.BlockSpec((ATTN_HEADS * hw, PROJ_TM), lambda i: (0, i)), "proj_attn_kt")
    qit = _attn_proj(
        _idx_q_kernel, n2, wi_b[:, o_qi:o_ki], [], [],
        jax.ShapeDtypeStruct((s // ATT_TQ, IDX_DIM, IDX_HEADS * ATT_TQ), BF16),
        pl.BlockSpec((PROJ_TM // ATT_TQ, IDX_DIM, IDX_HEADS * ATT_TQ), lambda i: (i, 0, 0)), "proj_idx_q")

    y_a = _sgu(ug, vn, w_spatial, jnp.transpose(b_spatial))
    y_b = _attention(qit, wit, ki, q_aug, kt_aug, vv, alibi_c)

    x2 = _merge(y_a, y_b, sg, x1, gt2, w_branch_a.astype(BF16), w_branch_b.astype(BF16),
                w_out.astype(BF16))

    return _ffn(x2, g_norm3.reshape(1, d), sh3, sc3, gt3, *ffn_weights(w2_gate, w2_up, w2_down))


def kernel(x, c, w_ada, b_ada, g_norm1, w1_gate, w1_up, w1_down, g_norm2, w_in, g_sgu, w_spatial,
           b_spatial, g_q, g_k, g_kidx, b_kidx, w_branch_a, w_branch_b, w_out, g_norm3,
           w2_gate, w2_up, w2_down):
    batch, depth = x.shape[0], w_ada.shape[0]
    outs = []
    for b in range(batch):
        xb = x[b]
        for l in range(depth):
            xb = _layer(xb, c[b], w_ada[l], b_ada[l], g_norm1[l], w1_gate[l], w1_up[l], w1_down[l],
                        g_norm2[l], w_in[l], g_sgu[l], w_spatial[l], b_spatial[l], g_q[l], g_k[l],
                        g_kidx[l], b_kidx[l], w_branch_a[l], w_branch_b[l], w_out[l], g_norm3[l],
                        w2_gate[l], w2_up[l], w2_down[l])
        outs.append(xb[None])
    return outs[0] if batch == 1 else jnp.concatenate(outs)
```

```python
import functools

import numpy as np
import jax
import jax.numpy as jnp
from jax import lax
from jax.experimental import pallas as pl
from jax.experimental.pallas import tpu as pltpu

F32 = jnp.float32
BF16 = jnp.bfloat16
I32 = jnp.int32

CHUNK = 128
SGU_GROUPS = 8
SGU_WIDTH = 1024
ATTN_HEADS = 8
HEAD_DIM = 128
ATTN_WIDTH = ATTN_HEADS * HEAD_DIM
IDX_HEADS = 16
IDX_DIM = 64
TOPK_MAX = 256
N_ADA = 9
EPS = 1e-6
NEG = -1e30

LANES = 128
SUBLANES = 8
VMEM_LIMIT = 56 * 1024 * 1024
FFN_VMEM_LIMIT = 60 * 1024 * 1024

FFN_TM = 512
FFN_TF = 1024
PROJ_TM = 1024
PROJ_TN = 1024
SGU_TM = 512
MERGE_TM = 512
ATT_TQ = 256
ATT_TK = 1024
IDX_TK = 256
IDX_SUB = 4
CNT_BLK = 1024
CNT_ACC = 8
BISECT_CAP = 512
AUG = 128
N_PIECES = 3
LOG2E = 1.4426950408889634


def _cparams(sem):
    return pltpu.CompilerParams(dimension_semantics=sem, vmem_limit_bytes=VMEM_LIMIT)


def _gelu_tanh(x):
    return 0.5 * x * (1.0 + jnp.tanh(0.7978845608028654 * (x + 0.044715 * (x * x * x))))


def _rms_mod(x, g, sh, sc):
    ms = jnp.mean(x * x, axis=-1, keepdims=True)
    y = (x * lax.rsqrt(ms + EPS)) * g
    return y * (1.0 + sc) + sh


def _ada_kernel(c_ref, w_ref, b_ref, o_ref, sb_ref):
    d = w_ref.shape[0]
    tn = w_ref.shape[1]

    @pl.when(pl.program_id(0) == 0)
    def _():
        cc = c_ref[...]
        sb_ref[...] = jnp.broadcast_to(cc * jax.nn.sigmoid(cc), (d, LANES))

    sb = sb_ref[...]
    for cb in range(tn // LANES):
        cols = slice(cb * LANES, (cb + 1) * LANES)
        prod = (w_ref[:, cols] * sb).reshape(d // SUBLANES, SUBLANES, LANES).sum(axis=0)
        o_ref[:, cols] = prod.sum(axis=0, keepdims=True) + b_ref[:, cols]


def _ada(c_col, w, b):
    d, n = w.shape
    tn = 1024
    return pl.pallas_call(
        _ada_kernel,
        grid=(n // tn,),
        in_specs=[pl.BlockSpec((d, 1), lambda j: (0, 0)),
                  pl.BlockSpec((d, tn), lambda j: (0, j)),
                  pl.BlockSpec((1, tn), lambda j: (0, j))],
        out_specs=pl.BlockSpec((1, tn), lambda j: (0, j)),
        out_shape=jax.ShapeDtypeStruct((1, n), F32),
        scratch_shapes=[pltpu.VMEM((d, LANES), F32)],
        compiler_params=_cparams(("arbitrary",)),
        name="ada",
    )(c_col, w, b)


def _gate_up(n, wg_ref, wu_ref):
    g = jnp.dot(n, wg_ref[...], preferred_element_type=F32)
    u = jnp.dot(n, wu_ref[...], preferred_element_type=F32)
    return ((g * jax.nn.sigmoid(g)) * u).astype(BF16)


def _ffn_kernel(x_ref, g_ref, sh_ref, sc_ref, gt_ref, wg_ref, wu_ref, wd_ref, *rest, has_tail, has_next):
    rest = list(rest)
    tail = [rest.pop(0) for _ in range(3)] if has_tail else []
    nxt = [rest.pop(0) for _ in range(3)] if has_next else []
    o_ref = rest.pop(0)
    nxt_ref = rest.pop(0) if has_next else None
    n_ref, h_ref = rest
    j = pl.program_id(1)
    last = pl.num_programs(1) - 1

    @pl.when(j == 0)
    def _():
        n_ref[...] = _rms_mod(x_ref[...], g_ref[...], sh_ref[...], sc_ref[...]).astype(BF16)
        o_ref[...] = jnp.zeros(o_ref.shape, F32)
        h_ref[0] = _gate_up(n_ref[...], wg_ref, wu_ref)

    @pl.when((j > 0) & (j < last))
    def _():
        o_ref[...] += jnp.dot(h_ref[(j - 1) % 2], wd_ref[...], preferred_element_type=F32)
        h_ref[j % 2] = _gate_up(n_ref[...], wg_ref, wu_ref)

    @pl.when(j == last)
    def _():
        y = o_ref[...] + jnp.dot(h_ref[(j - 1) % 2], wd_ref[...], preferred_element_type=F32)
        if tail:
            wg_t, wu_t, wd_t = tail
            y = y + jnp.dot(_gate_up(n_ref[...], wg_t, wu_t), wd_t[...], preferred_element_type=F32)
        out = x_ref[...] + (0.5 * gt_ref[...]) * y
        o_ref[...] = out
        if has_next:
            g2, sh2, sc2 = nxt
            nxt_ref[...] = _rms_mod(out, g2[...], sh2[...], sc2[...]).astype(nxt_ref.dtype)


def _ffn(x, g, sh, sc, gt, wg, wu, wd, next_norm=None):
    s, d = x.shape
    f = wg.shape[1]
    tm = min(FFN_TM, s)
    n_main = f // FFN_TF
    f_main = n_main * FFN_TF
    row = pl.BlockSpec((1, d), lambda i, j: (0, 0))
    assert n_main >= 1
    in_specs = [pl.BlockSpec((tm, d), lambda i, j: (i, 0)), row, row, row, row,
                pl.BlockSpec((d, FFN_TF), lambda i, j: (0, jnp.minimum(j, n_main - 1))),
                pl.BlockSpec((d, FFN_TF), lambda i, j: (0, jnp.minimum(j, n_main - 1))),
                pl.BlockSpec((FFN_TF, d), lambda i, j: (jnp.maximum(j - 1, 0), 0))]
    args = [x, g, sh, sc, gt, wg, wu, wd]
    if f_main < f:
        assert (f - f_main) % LANES == 0
        args += [wg[:, f_main:], wu[:, f_main:], wd[f_main:]]
        once = dict(pipeline_mode=pl.Buffered(1))
        in_specs += [pl.BlockSpec((d, f - f_main), lambda i, j: (0, 0), **once),
                     pl.BlockSpec((d, f - f_main), lambda i, j: (0, 0), **once),
                     pl.BlockSpec((f - f_main, d), lambda i, j: (0, 0), **once)]
    out_specs = pl.BlockSpec((tm, d), lambda i, j: (i, 0))
    out_shape = jax.ShapeDtypeStruct((s, d), F32)
    if next_norm is not None:
        args += list(next_norm)
        in_specs += [row, row, row]
        out_specs = [out_specs, pl.BlockSpec((tm, d), lambda i, j: (i, 0))]
        out_shape = [out_shape, jax.ShapeDtypeStruct((s, d), BF16)]
    return pl.pallas_call(
        functools.partial(_ffn_kernel, has_tail=f_main < f, has_next=next_norm is not None),
        grid=(s // tm, n_main + 1),
        in_specs=in_specs,
        out_specs=out_specs,
        out_shape=out_shape,
        scratch_shapes=[pltpu.VMEM((tm, d), BF16), pltpu.VMEM((2, tm, FFN_TF), BF16)],
        compiler_params=pltpu.CompilerParams(dimension_semantics=("parallel", "arbitrary"),
                                             vmem_limit_bytes=FFN_VMEM_LIMIT),
        name="ffn",
    )(*args)


def _group_rms(z, gain, post_scale):
    outs = []
    for gidx in range(z.shape[1] // LANES):
        cols = slice(gidx * LANES, (gidx + 1) * LANES)
        zg = z[:, cols]
        ms = jnp.mean(zg * zg, axis=-1, keepdims=True)
        y = (zg * lax.rsqrt(ms + EPS)) * gain[:, cols]
        if post_scale != 1.0:
            y = y * post_scale
        outs.append(y)
    return jnp.concatenate(outs, axis=-1)


def _cast_once(w_ref, wb_ref):
    @pl.when(pl.program_id(0) == 0)
    def _():
        wb_ref[...] = w_ref[...].astype(BF16)
    return wb_ref[...]


def _proj_kernel(n_ref, w_ref, *rest, mode, cast):
    if cast:
        o_ref, w = rest[-2], _cast_once(w_ref, rest[-1])
    else:
        o_ref, w = rest[-1], w_ref[...]
    z = jnp.dot(n_ref[...], w, preferred_element_type=F32)
    if mode == "gelu":
        out = _gelu_tanh(z)
    elif mode == "gelu_gnorm":
        out = _group_rms(_gelu_tanh(z), rest[0][...], 1.0)
    elif mode == "sigmoid":
        out = jax.nn.sigmoid(z)
    else:
        assert mode == "none"
        out = z
    o_ref[...] = out.astype(o_ref.dtype)


def _proj(n, w, mode, gain=None, col_block=None):
    s, d = n.shape
    cast = col_block is not None
    nout = PROJ_TN if cast else w.shape[1]
    if cast:
        w_spec = pl.BlockSpec((d, PROJ_TN), lambda i, j: (0, col_block), pipeline_mode=pl.Buffered(1))
    else:
        w_spec = pl.BlockSpec((d, PROJ_TN), lambda i, j: (0, j))
    in_specs = [pl.BlockSpec((PROJ_TM, d), lambda i, j: (i, 0)), w_spec]
    args = [n, w]
    if gain is not None:
        in_specs.append(pl.BlockSpec((1, PROJ_TN), lambda i, j: (0, j)))
        args.append(gain)
    return pl.pallas_call(
        functools.partial(_proj_kernel, mode=mode, cast=cast),
        grid=(s // PROJ_TM, nout // PROJ_TN),
        in_specs=in_specs,
        out_specs=pl.BlockSpec((PROJ_TM, PROJ_TN), lambda i, j: (i, j)),
        out_shape=jax.ShapeDtypeStruct((s, nout), BF16),
        scratch_shapes=[pltpu.VMEM((d, PROJ_TN), BF16)] if cast else [],
        compiler_params=_cparams(("arbitrary", "arbitrary")),
        name="proj_" + mode,
    )(*args)


def _split_hi_lo(pos):
    lo = pos & (LANES - 1)
    return pos - lo, lo


def _piece_slot(idx, k):
    return (idx >= k * N_PIECES) & (idx < (k + 1) * N_PIECES)


def _attn_q_kernel(n_ref, w_ref, gain_ref, cpos_ref, o_ref, wb_ref, *, post_scale):
    tm = n_ref.shape[0]
    y = _group_rms(jnp.dot(n_ref[...], _cast_once(w_ref, wb_ref), preferred_element_type=F32),
                   gain_ref[...], post_scale)
    col = lax.broadcasted_iota(I32, (tm, AUG), 1)
    t_hi, t_lo = _split_hi_lo(lax.broadcasted_iota(I32, (tm, AUG), 0) & (ATT_TQ - 1))
    tab = jnp.where(_piece_slot(col, 2), t_hi, jnp.where(_piece_slot(col, 3), t_lo, 0)).astype(F32)
    hw = HEAD_DIM + AUG
    for h in range(ATTN_HEADS):
        o_ref[:, h * hw:h * hw + HEAD_DIM] = y[:, h * HEAD_DIM:(h + 1) * HEAD_DIM].astype(o_ref.dtype)
        o_ref[:, h * hw + HEAD_DIM:(h + 1) * hw] = (tab + cpos_ref[h:h + 1, :]).astype(o_ref.dtype)


def _attn_kt_kernel(n_ref, w_ref, gain_ref, cneg_ref, o_ref, wb_ref):
    tm = n_ref.shape[0]
    y = _group_rms(jnp.dot(n_ref[...], _cast_once(w_ref, wb_ref), preferred_element_type=F32),
                   gain_ref[...], 1.0)
    yt = jnp.transpose(y)
    row = lax.broadcasted_iota(I32, (AUG, tm), 0)
    pos = pl.program_id(0) * tm + lax.broadcasted_iota(I32, (AUG, tm), 1)
    u_hi, u_lo = _split_hi_lo(pos & (ATT_TK - 1))
    tab = jnp.where(_piece_slot(row, 0), u_hi, jnp.where(_piece_slot(row, 1), u_lo, 0)).astype(F32)
    hw = HEAD_DIM + AUG
    for h in range(ATTN_HEADS):
        o_ref[h * hw:h * hw + HEAD_DIM, :] = yt[h * HEAD_DIM:(h + 1) * HEAD_DIM, :].astype(o_ref.dtype)
        o_ref[h * hw + HEAD_DIM:(h + 1) * hw, :] = (tab + cneg_ref[h]).astype(o_ref.dtype)


def _idx_q_kernel(n_ref, w_ref, o_ref, wb_ref):
    zt = jnp.transpose(jnp.dot(n_ref[...], _cast_once(w_ref, wb_ref),
                               preferred_element_type=F32))
    for qt in range(n_ref.shape[0] // ATT_TQ):
        for h in range(IDX_HEADS):
            o_ref[qt, :, h * ATT_TQ:(h + 1) * ATT_TQ] = zt[
                h * IDX_DIM:(h + 1) * IDX_DIM, qt * ATT_TQ:(qt + 1) * ATT_TQ].astype(o_ref.dtype)


def _attn_proj(kernel, n, w, col_block, extra, extra_specs, out_shape, out_spec, name):
    s, d = n.shape
    return pl.pallas_call(
        kernel,
        grid=(s // PROJ_TM,),
        in_specs=[pl.BlockSpec((PROJ_TM, d), lambda i: (i, 0)),
                  pl.BlockSpec((d, PROJ_TN), lambda i: (0, col_block),
                               pipeline_mode=pl.Buffered(1))] + extra_specs,
        out_specs=out_spec,
        out_shape=out_shape,
        scratch_shapes=[pltpu.VMEM((d, PROJ_TN), BF16)],
        compiler_params=_cparams(("arbitrary",)),
        name=name,
    )(n, w, *extra)


def _kidx_kernel(n_ref, w_ref, g_ref, b_ref, ki_ref, wit_ref):
    z = jnp.dot(n_ref[...], w_ref[...].astype(BF16), preferred_element_type=F32)
    ki = z[:, :IDX_DIM]
    mu = jnp.mean(ki, axis=-1, keepdims=True)
    var = jnp.mean(jnp.square(ki - mu), axis=-1, keepdims=True)
    y = (ki - mu) * lax.rsqrt(var + EPS)
    ki_ref[...] = (y * g_ref[...] + b_ref[...]).astype(ki_ref.dtype)
    wit_ref[...] = jnp.transpose(z)[IDX_DIM:IDX_DIM + IDX_HEADS, :] * (IDX_HEADS ** -0.5 * IDX_DIM ** -0.5)


def _kidx(n, w, col_block, g, b):
    s, d = n.shape
    tm = 512
    return pl.pallas_call(
        _kidx_kernel,
        grid=(s // tm,),
        in_specs=[pl.BlockSpec((tm, d), lambda i: (i, 0)),
                  pl.BlockSpec((d, LANES), lambda i: (0, col_block)),
                  pl.BlockSpec((1, IDX_DIM), lambda i: (0, 0)),
                  pl.BlockSpec((1, IDX_DIM), lambda i: (0, 0))],
        out_specs=[pl.BlockSpec((tm, IDX_DIM), lambda i: (i, 0)),
                   pl.BlockSpec((IDX_HEADS, tm), lambda i: (0, i))],
        out_shape=[jax.ShapeDtypeStruct((s, IDX_DIM), BF16),
                   jax.ShapeDtypeStruct((IDX_HEADS, s), F32)],
        compiler_params=_cparams(("parallel",)),
        name="proj_kidx",
    )(n, w, g, b)


def _sgu_kernel(u_ref, v_ref, ws_ref, bt_ref, o_ref):
    tm = u_ref.shape[0]
    r = lax.broadcasted_iota(I32, (CHUNK, CHUNK), 0)
    c = lax.broadcasted_iota(I32, (CHUNK, CHUNK), 1)
    causal = c <= r
    for g in range(SGU_GROUPS):
        cols = slice(g * LANES, (g + 1) * LANES)
        w = jnp.where(causal, ws_ref[g], 0.0).astype(BF16)
        bcol = bt_ref[:, g:g + 1]
        for ch in range(tm // CHUNK):
            rows = slice(ch * CHUNK, (ch + 1) * CHUNK)
            sv = jnp.dot(w, v_ref[rows, cols], preferred_element_type=F32) + bcol
            o_ref[rows, cols] = (u_ref[rows, cols].astype(F32) * sv).astype(o_ref.dtype)


def _sgu(u, v, ws, bt):
    s, wdt = u.shape
    return pl.pallas_call(
        _sgu_kernel,
        grid=(s // SGU_TM,),
        in_specs=[pl.BlockSpec((SGU_TM, wdt), lambda i: (i, 0)),
                  pl.BlockSpec((SGU_TM, wdt), lambda i: (i, 0)),
                  pl.BlockSpec((SGU_GROUPS, CHUNK, CHUNK), lambda i: (0, 0, 0)),
                  pl.BlockSpec((CHUNK, SGU_GROUPS), lambda i: (0, 0))],
        out_specs=pl.BlockSpec((SGU_TM, wdt), lambda i: (i, 0)),
        out_shape=jax.ShapeDtypeStruct((s, wdt), BF16),
        compiler_params=_cparams(("parallel",)),
        name="sgu",
    )(u, v, ws, bt)


def _tile_lanes(x, n):
    return x if n == 1 else jnp.concatenate([x] * n, axis=1)


def _sublane_all(x, op):
    shift = SUBLANES // 2
    while shift:
        x = op(x, pltpu.roll(x, shift, 0))
        shift //= 2
    return x


def _count_keys(score_ref, n_blocks, pred):
    tq = score_ref.shape[1]

    def body(it, acc):
        off = pl.multiple_of(it * CNT_BLK, CNT_BLK)
        blk = score_ref[pl.ds(off, CNT_BLK), :].reshape(CNT_BLK // SUBLANES, SUBLANES, tq)
        ind = jnp.where(pred(blk, off), 1, 0)
        return acc + ind.reshape(-1, CNT_ACC, SUBLANES, tq).sum(axis=0)

    acc = lax.fori_loop(0, n_blocks, body, jnp.zeros((CNT_ACC, SUBLANES, tq), I32))
    tot = jnp.sum(acc.sum(axis=0).astype(F32), axis=0, keepdims=True)
    return jnp.broadcast_to(tot, (SUBLANES, tq)).astype(I32)


def _attn_kernel(itab_ref, jtab_ref, qit_ref, wit_ref, ki_ref, q_ref, kt_ref, v_ref, o_ref,
                 score_ref, r_ref, gmax_ref, thr_ref, m_ref, acc_ref, *, alibi_c, topk, idx_bits):
    i = itab_ref[pl.program_id(0)]
    j = jtab_ref[pl.program_id(0)]
    tq, tk = ATT_TQ, ATT_TK
    hw = HEAD_DIM + AUG
    q0 = i * tq
    n_kv = (q0 + tq + tk - 1) // tk
    n_blk = (q0 + tq + CNT_BLK - 1) // CNT_BLK

    @pl.when(j == 0)
    def _index_and_threshold():
        m_ref[...] = jnp.full(m_ref.shape, NEG, F32)
        acc_ref[...] = jnp.zeros(acc_ref.shape, F32)

        def index_chunk(off, n_sub):
            rows = n_sub * IDX_TK
            r_ref[0:rows, :] = jnp.dot(ki_ref[pl.ds(off, rows), :], qit_ref[0],
                                       preferred_element_type=F32)
            for sub in range(n_sub):
                r0 = sub * IDX_TK
                s_idx = off + r0 + lax.broadcasted_iota(I32, (IDX_TK, LANES), 0)
                for g in range(tq // LANES):
                    lanes = slice(g * LANES, (g + 1) * LANES)
                    t_idx = q0 + g * LANES + lax.broadcasted_iota(I32, (IDX_TK, LANES), 1)
                    acc = jnp.zeros((IDX_TK, LANES), F32)
                    for h in range(IDX_HEADS):
                        rr = r_ref[r0:r0 + IDX_TK, h * tq + g * LANES:h * tq + (g + 1) * LANES]
                        acc = acc + jnp.maximum(rr, 0.0) * wit_ref[h:h + 1, lanes]
                    score = jnp.where(s_idx <= t_idx, acc, -jnp.inf)
                    score_ref[pl.ds(pl.multiple_of(off + r0, IDX_TK), IDX_TK), lanes] = score
                    gmax_ref[:, lanes] = jnp.maximum(gmax_ref[:, lanes], score)

        def chunk_body(c, carry):
            index_chunk(pl.multiple_of(c * (IDX_SUB * IDX_TK), IDX_SUB * IDX_TK), IDX_SUB)
            return carry

        gmax_ref[...] = jnp.full(gmax_ref.shape, -jnp.inf, F32)
        n_idx = (q0 + tq) // IDX_TK
        lax.fori_loop(0, n_idx // IDX_SUB, chunk_body, 0)
        for rest in range(1, IDX_SUB):
            @pl.when(n_idx % IDX_SUB == rest)
            def _(rest=rest):
                index_chunk(pl.multiple_of((n_idx - rest) * IDX_TK, IDX_TK), rest)

        def fill_body(f, carry):
            off = pl.multiple_of(q0 + tq + f * tq, tq)
            score_ref[pl.ds(off, tq), :] = jnp.full((tq, tq), -jnp.inf, F32)
            return carry

        lax.fori_loop(0, (n_blk * CNT_BLK - (q0 + tq)) // tq, fill_body, 0)

        t_row = q0 + lax.broadcasted_iota(I32, (SUBLANES, tq), 1)
        kq = jnp.minimum(topk, t_row + 1)

        g = gmax_ref[...].reshape(IDX_TK // SUBLANES, SUBLANES, tq)
        hi0 = _sublane_all(g.max(axis=0), jnp.maximum)
        lo0 = _sublane_all(jnp.where(g == -jnp.inf, jnp.inf, g).min(axis=0), jnp.minimum)

        def bis_cond(st):
            it, _, _, _, _, done = st
            return jnp.logical_and(it < BISECT_CAP, done == 0)

        def bis_body(st):
            it, lo, hi, cnt, fin, _ = st
            mid = 0.5 * lo + 0.5 * hi
            stuck = (mid <= lo) | (mid >= hi)
            cand = jnp.where(stuck, hi, mid)
            tot = _count_keys(score_ref, n_blk, lambda blk, off: blk >= cand[None])
            take = (tot >= kq) & (fin == 0)
            lo = jnp.where(take, cand, lo)
            cnt = jnp.where(take, tot, cnt)
            hi = jnp.where(take | (fin != 0), hi, cand)
            fin = jnp.where(stuck, 1, fin)
            settled = (cnt == kq) | (fin != 0)
            done = (jnp.min(jnp.where(settled, 1.0, 0.0)) > 0.5).astype(I32)
            return it + 1, lo, hi, cnt, fin, done

        empty = lo0 > hi0
        _, thr, _, cnt, _, _ = lax.while_loop(
            bis_cond, bis_body,
            (jnp.int32(0), lo0, hi0, jnp.where(empty, 0, -1), jnp.where(empty, 1, 0), jnp.int32(0)))
        thr_ref[...] = thr
        cnt = lax.cond(jnp.min(cnt.astype(F32)) < 0.0,
                       lambda: jnp.where(cnt < 0, _count_keys(
                           score_ref, n_blk, lambda blk, off: blk >= thr[None]), cnt),
                       lambda: cnt)

        @pl.when(jnp.max(jnp.where(cnt > kq, 1.0, 0.0)) > 0.5)
        def _break_ties():
            def row_idx(off):
                return off + lax.broadcasted_iota(I32, (CNT_BLK, tq), 0).reshape(
                    CNT_BLK // SUBLANES, SUBLANES, tq)

            above = _count_keys(score_ref, n_blk, lambda blk, off: blk > thr[None])
            need = kq - above

            def cut_body(b, p):
                c = p + lax.shift_left(jnp.int32(1), idx_bits - 1 - b)
                below = _count_keys(
                    score_ref, n_blk, lambda blk, off: (blk == thr[None]) & (row_idx(off) < c[None]))
                return jnp.where(below < need, c, p)

            p = lax.fori_loop(0, idx_bits, cut_body, jnp.zeros((SUBLANES, tq), I32))

            def drop_body(it, carry):
                off = pl.multiple_of(it * CNT_BLK, CNT_BLK)
                blk = score_ref[pl.ds(off, CNT_BLK), :].reshape(CNT_BLK // SUBLANES, SUBLANES, tq)
                new = jnp.where((blk == thr[None]) & (row_idx(off) > p[None]), -jnp.inf, blk)
                score_ref[pl.ds(off, CNT_BLK), :] = new.reshape(CNT_BLK, tq)
                return carry

            lax.fori_loop(0, n_blk, drop_body, 0)

    k0 = pl.multiple_of(j * tk, tk)
    nl = tk // LANES
    sel_t = score_ref[pl.ds(k0, tk), :] >= thr_ref[0:1, :]
    bias = jnp.transpose(jnp.where(sel_t, 0.0, NEG))
    tile_dist = jnp.full((tq, LANES), k0 - q0, I32).astype(F32)
    ones_col = (lax.broadcasted_iota(I32, (tk, LANES), 1) == 0).astype(BF16)

    def logits(h):
        s = bias + jnp.dot(q_ref[:, h * hw:(h + 1) * hw], kt_ref[h * hw:(h + 1) * hw, :],
                           preferred_element_type=F32)
        return s, jnp.max(s, axis=-1, keepdims=True)

    def probs(h, s, s_max):
        shift = tile_dist * alibi_c[h]
        m_prev = m_ref[h]
        m_new = jnp.maximum(m_prev, s_max + shift)
        m_ref[h] = m_new
        p = jnp.exp2(s - _tile_lanes(m_new - shift, nl)).astype(BF16)
        return p, jnp.exp2(m_prev - m_new)

    def accumulate(h, p, alpha):
        v_aug = jnp.concatenate([v_ref[:, h * HEAD_DIM:(h + 1) * HEAD_DIM], ones_col], axis=1)
        acc_ref[h] = _tile_lanes(alpha, 2) * acc_ref[h] + jnp.dot(
            p, v_aug, preferred_element_type=F32)

    st_a, st_b = {}, {}
    for step in range(ATTN_HEADS + 2):
        if 0 <= step - 2 < ATTN_HEADS:
            accumulate(step - 2, *st_b.pop(step - 2))
        if step < ATTN_HEADS:
            st_a[step] = logits(step)
        if 0 <= step - 1 < ATTN_HEADS:
            st_b[step - 1] = probs(step - 1, *st_a.pop(step - 1))

    @pl.when(j == n_kv - 1)
    def _finish():
        for h in range(ATTN_HEADS):
            acc = acc_ref[h]
            o_ref[:, h * HEAD_DIM:(h + 1) * HEAD_DIM] = (
                acc[:, :HEAD_DIM] / acc[:, HEAD_DIM:HEAD_DIM + 1]).astype(o_ref.dtype)


def _alibi_pieces():
    out = []
    for h in range(ATTN_HEADS):
        rest = np.float32(LOG2E * 2.0 ** (-8.0 * (h + 1) / ATTN_HEADS))
        pieces = []
        for _ in range(N_PIECES):
            piece = np.float32(rest.astype(BF16))
            pieces.append(piece)
            rest = np.float32(rest - piece)
        out.append(pieces)
    return np.asarray(out, np.float32)


def _alibi_tables():
    pieces = _alibi_pieces()
    n = N_PIECES
    assert 4 * n <= AUG
    cpos = np.zeros((ATTN_HEADS, AUG), np.float32)
    cneg = np.zeros((ATTN_HEADS, AUG, 1), np.float32)
    cpos[:, 0 * n:1 * n] = pieces
    cpos[:, 1 * n:2 * n] = pieces
    cneg[:, 2 * n:3 * n, 0] = -pieces
    cneg[:, 3 * n:4 * n, 0] = -pieces
    alibi_c = tuple(float(p.sum(dtype=np.float32)) for p in pieces)
    return jnp.asarray(cpos), jnp.asarray(cneg), alibi_c


def _attention(qit, wit, ki, q_aug, kt_aug, v, alibi_c):
    s = q_aug.shape[0]
    tq, tk = ATT_TQ, ATT_TK
    hw = HEAD_DIM + AUG
    assert s % CNT_BLK == 0 and CNT_BLK % tk == 0 and CNT_BLK % tq == 0 and tq % LANES == 0
    assert tq & (tq - 1) == 0 and tk & (tk - 1) == 0 and PROJ_TM % tq == 0
    topk = min(TOPK_MAX, s // 4)
    idx_bits = max(1, (s - 1).bit_length())
    assert IDX_TK >= topk and tq % IDX_TK == 0

    steps = [(i, j) for i in range(s // tq) for j in range(((i + 1) * tq + tk - 1) // tk)]
    itab = jnp.asarray(np.array([p[0] for p in steps], np.int32))
    jtab = jnp.asarray(np.array([p[1] for p in steps], np.int32))

    grid_spec = pltpu.PrefetchScalarGridSpec(
        num_scalar_prefetch=2,
        grid=(len(steps),),
        in_specs=[pl.BlockSpec((1, IDX_DIM, IDX_HEADS * tq), lambda t, it, jt: (it[t], 0, 0)),
                  pl.BlockSpec((IDX_HEADS, tq), lambda t, it, jt: (0, it[t])),
                  pl.BlockSpec((s, IDX_DIM), lambda t, it, jt: (0, 0)),
                  pl.BlockSpec((tq, ATTN_HEADS * hw), lambda t, it, jt: (it[t], 0)),
                  pl.BlockSpec((ATTN_HEADS * hw, tk), lambda t, it, jt: (0, jt[t])),
                  pl.BlockSpec((tk, ATTN_WIDTH), lambda t, it, jt: (jt[t], 0))],
        out_specs=pl.BlockSpec((tq, ATTN_WIDTH), lambda t, it, jt: (it[t], 0)),
        scratch_shapes=[pltpu.VMEM((s, tq), F32),
                        pltpu.VMEM((IDX_SUB * IDX_TK, IDX_HEADS * tq), F32),
                        pltpu.VMEM((IDX_TK, tq), F32),
                        pltpu.VMEM((SUBLANES, tq), F32),
                        pltpu.VMEM((ATTN_HEADS, tq, LANES), F32),
                        pltpu.VMEM((ATTN_HEADS, tq, 2 * HEAD_DIM), F32)])
    return pl.pallas_call(
        functools.partial(_attn_kernel, alibi_c=alibi_c, topk=topk, idx_bits=idx_bits),
        grid_spec=grid_spec,
        out_shape=jax.ShapeDtypeStruct((s, ATTN_WIDTH), BF16),
        compiler_params=_cparams(("arbitrary",)),
        name="dsa_attention",
    )(itab, jtab, qit, wit, ki, q_aug, kt_aug, v)


def _merge_kernel(ya_ref, yb_ref, sg_a_ref, sg_b_ref, x_ref, gt_ref, wa_ref, wb_ref, wo_ref, o_ref):
    a = jnp.dot(ya_ref[...], wa_ref[...], preferred_element_type=F32)
    b = jnp.dot(yb_ref[...], wb_ref[...], preferred_element_type=F32)
    merged = sg_a_ref[...].astype(F32) * a + sg_b_ref[...].astype(F32) * b
    o_ref[...] = x_ref[...] + gt_ref[...] * jnp.dot(
        merged.astype(BF16), wo_ref[...], preferred_element_type=F32)


def _merge(ya, yb, sg, x, gt, wa, wb, wo):
    s, d = x.shape
    tm = MERGE_TM
    wdt = ya.shape[1]
    const = lambda i: (0, 0)
    return pl.pallas_call(
        _merge_kernel,
        grid=(s // tm,),
        in_specs=[pl.BlockSpec((tm, wdt), lambda i: (i, 0)),
                  pl.BlockSpec((tm, wdt), lambda i: (i, 0)),
                  pl.BlockSpec((tm, d), lambda i: (i, 0)),
                  pl.BlockSpec((tm, d), lambda i: (i, 1)),
                  pl.BlockSpec((tm, d), lambda i: (i, 0)),
                  pl.BlockSpec((1, d), const),
                  pl.BlockSpec((wdt, d), const, pipeline_mode=pl.Buffered(1)),
                  pl.BlockSpec((wdt, d), const, pipeline_mode=pl.Buffered(1)),
                  pl.BlockSpec((d, d), const, pipeline_mode=pl.Buffered(1))],
        out_specs=pl.BlockSpec((tm, d), lambda i: (i, 0)),
        out_shape=jax.ShapeDtypeStruct((s, d), F32),
        compiler_params=_cparams(("parallel",)),
        name="merge",
    )(ya, yb, sg, sg, x, gt, wa, wb, wo)


def _layer(x, c, w_ada, b_ada, g_norm1, w1_gate, w1_up, w1_down, g_norm2, w_in, g_sgu, w_spatial,
           b_spatial, g_q, g_k, g_kidx, b_kidx, w_branch_a, w_branch_b, w_out, g_norm3,
           w2_gate, w2_up, w2_down):
    s, d = x.shape

    ada = _ada(c.reshape(d, 1), w_ada, b_ada.reshape(1, -1))
    sh1, sc1, gt1, sh2, sc2, gt2, sh3, sc3, gt3 = [ada[:, k * d:(k + 1) * d] for k in range(N_ADA)]

    def ffn_weights(wg, wu, wd):
        return wg.astype(BF16), wu.astype(BF16), wd.astype(BF16)

    x1, n2 = _ffn(x, g_norm1.reshape(1, d), sh1, sc1, gt1, *ffn_weights(w1_gate, w1_up, w1_down),
                  next_norm=(g_norm2.reshape(1, d), sh2, sc2))

    o_u, o_v, o_q = 0, SGU_WIDTH, 2 * SGU_WIDTH
    o_k, o_vv = o_q + ATTN_WIDTH, o_q + 2 * ATTN_WIDTH
    o_qi = o_vv + ATTN_WIDTH
    o_ki = o_qi + IDX_HEADS * IDX_DIM
    o_g = o_ki + IDX_DIM + IDX_HEADS

    assert all(o % PROJ_TN == 0 for o in (o_u, o_v, o_q, o_k, o_vv, o_qi)) and o_ki % LANES == 0
    ug = _proj(n2, w_in, "gelu", col_block=o_u // PROJ_TN)
    vn = _proj(n2, w_in, "gelu_gnorm", gain=g_sgu.reshape(1, SGU_WIDTH), col_block=o_v // PROJ_TN)
    vv = _proj(n2, w_in, "none", col_block=o_vv // PROJ_TN)
    sg = _proj(n2, w_in[:, o_g:].astype(BF16), "sigmoid")
    ki, wit = _kidx(n2, w_in, o_ki // LANES, g_kidx.reshape(1, IDX_DIM), b_kidx.reshape(1, IDX_DIM))

    cpos, cneg, alibi_c = _alibi_tables()
    hw = HEAD_DIM + AUG
    row = lambda i: (0, 0)
    q_aug = _attn_proj(
        functools.partial(_attn_q_kernel, post_scale=HEAD_DIM ** -0.5 * LOG2E),
        n2, w_in, o_q // PROJ_TN, [jnp.tile(g_q, ATTN_HEADS).reshape(1, -1), cpos],
        [pl.BlockSpec((1, ATTN_WIDTH), row), pl.BlockSpec((ATTN_HEADS, AUG), row)],
        jax.ShapeDtypeStruct((s, ATTN_HEADS * hw), BF16),
        pl.BlockSpec((PROJ_TM, ATTN_HEADS * hw), lambda i: (i, 0)), "proj_attn_q")
    kt_aug = _attn_proj(
        _attn_kt_kernel, n2, w_in, o_k // PROJ_TN, [jnp.tile(g_k, ATTN_HEADS).reshape(1, -1), cneg],
        [pl.BlockSpec((1, ATTN_WIDTH), row), pl.BlockSpec((ATTN_HEADS, AUG, 1), lambda i: (0, 0, 0))],
        jax.ShapeDtypeStruct((ATTN_HEADS * hw, s), BF16),
        pl.BlockSpec((ATTN_HEADS * hw, PROJ_TM), lambda i: (0, i)), "proj_attn_kt")
    qit = _attn_proj(
        _idx_q_kernel, n2, w_in, o_qi // PROJ_TN, [], [],
        jax.ShapeDtypeStruct((s // ATT_TQ, IDX_DIM, IDX_HEADS * ATT_TQ), BF16),
        pl.BlockSpec((PROJ_TM // ATT_TQ, IDX_DIM, IDX_HEADS * ATT_TQ), lambda i: (i, 0, 0)), "proj_idx_q")

    y_a = _sgu(ug, vn, w_spatial, jnp.transpose(b_spatial))
    y_b = _attention(qit, wit, ki, q_aug, kt_aug, vv, alibi_c)

    x2 = _merge(y_a, y_b, sg, x1, gt2, w_branch_a.astype(BF16), w_branch_b.astype(BF16),
                w_out.astype(BF16))

    return _ffn(x2, g_norm3.reshape(1, d), sh3, sc3, gt3, *ffn_weights(w2_gate, w2_up, w2_down))


def kernel(x, c, w_ada, b_ada, g_norm1, w1_gate, w1_up, w1_down, g_norm2, w_in, g_sgu, w_spatial,
           b_spatial, g_q, g_k, g_kidx, b_kidx, w_branch_a, w_branch_b, w_out, g_norm3,
           w2_gate, w2_up, w2_down):
    batch, depth = x.shape[0], w_ada.shape[0]
    outs = []
    for b in range(batch):
        xb = x[b]
        for l in range(depth):
            xb = _layer(xb, c[b], w_ada[l], b_ada[l], g_norm1[l], w1_gate[l], w1_up[l], w1_down[l],
                        g_norm2[l], w_in[l], g_sgu[l], w_spatial[l], b_spatial[l], g_q[l], g_k[l],
                        g_kidx[l], b_kidx[l], w_branch_a[l], w_branch_b[l], w_out[l], g_norm3[l],
                        w2_gate[l], w2_up[l], w2_down[l])
        outs.append(xb[None])
    return outs[0] if batch == 1 else jnp.concatenate(outs)
```

```python
import functools

import numpy as np
import jax
import jax.numpy as jnp
from jax import lax
from jax.experimental import pallas as pl
from jax.experimental.pallas import tpu as pltpu

F32 = jnp.float32
BF16 = jnp.bfloat16
I32 = jnp.int32

CHUNK = 128
SGU_GROUPS = 8
SGU_WIDTH = 1024
ATTN_HEADS = 8
HEAD_DIM = 128
ATTN_WIDTH = ATTN_HEADS * HEAD_DIM
IDX_HEADS = 16
IDX_DIM = 64
TOPK_MAX = 256
N_ADA = 9
EPS = 1e-6
NEG = -1e30

LANES = 128
SUBLANES = 8
VMEM_LIMIT = 56 * 1024 * 1024
FFN_VMEM_LIMIT = 60 * 1024 * 1024

FFN_TM = 512
FFN_TF = 1024
PROJ_TM = 1024
PROJ_TN = 1024
SGU_TM = 512
MERGE_TM = 512
ATT_TQ = 256
ATT_TK = 1024
IDX_TK = 256
IDX_SUB = 4
CNT_BLK = 1024
CNT_ACC = 8
BISECT_CAP = 512
AUG = 128
N_PIECES = 3
LOG2E = 1.4426950408889634


def _cparams(sem):
    return pltpu.CompilerParams(dimension_semantics=sem, vmem_limit_bytes=VMEM_LIMIT)


def _gelu_tanh(x):
    return 0.5 * x * (1.0 + jnp.tanh(0.7978845608028654 * (x + 0.044715 * (x * x * x))))


def _rms_mod(x, g, sh, sc):
    ms = jnp.mean(x * x, axis=-1, keepdims=True)
    y = (x * lax.rsqrt(ms + EPS)) * g
    return y * (1.0 + sc) + sh


def _ada_kernel(c_ref, w_ref, b_ref, o_ref, sb_ref):
    d = w_ref.shape[0]
    tn = w_ref.shape[1]

    @pl.when(pl.program_id(0) == 0)
    def _():
        cc = c_ref[...]
        sb_ref[...] = jnp.broadcast_to(cc * jax.nn.sigmoid(cc), (d, LANES))

    sb = sb_ref[...]
    for cb in range(tn // LANES):
        cols = slice(cb * LANES, (cb + 1) * LANES)
        prod = (w_ref[:, cols] * sb).reshape(d // SUBLANES, SUBLANES, LANES).sum(axis=0)
        o_ref[:, cols] = prod.sum(axis=0, keepdims=True) + b_ref[:, cols]


def _ada(c_col, w, b):
    d, n = w.shape
    tn = 1024
    return pl.pallas_call(
        _ada_kernel,
        grid=(n // tn,),
        in_specs=[pl.BlockSpec((d, 1), lambda j: (0, 0)),
                  pl.BlockSpec((d, tn), lambda j: (0, j)),
                  pl.BlockSpec((1, tn), lambda j: (0, j))],
        out_specs=pl.BlockSpec((1, tn), lambda j: (0, j)),
        out_shape=jax.ShapeDtypeStruct((1, n), F32),
        scratch_shapes=[pltpu.VMEM((d, LANES), F32)],
        compiler_params=_cparams(("arbitrary",)),
        name="ada",
    )(c_col, w, b)


def _gate_up(n, wg_ref, wu_ref):
    g = jnp.dot(n, wg_ref[...], preferred_element_type=F32)
    u = jnp.dot(n, wu_ref[...], preferred_element_type=F32)
    return ((g * jax.nn.sigmoid(g)) * u).astype(BF16)


def _ffn_kernel(x_ref, g_ref, sh_ref, sc_ref, gt_ref, wg_ref, wu_ref, wd_ref, *rest, has_tail, has_next):
    rest = list(rest)
    tail = [rest.pop(0) for _ in range(3)] if has_tail else []
    nxt = [rest.pop(0) for _ in range(3)] if has_next else []
    o_ref = rest.pop(0)
    nxt_ref = rest.pop(0) if has_next else None
    n_ref, h_ref = rest
    j = pl.program_id(1)
    last = pl.num_programs(1) - 1

    @pl.when(j == 0)
    def _():
        n_ref[...] = _rms_mod(x_ref[...], g_ref[...], sh_ref[...], sc_ref[...]).astype(BF16)
        o_ref[...] = jnp.zeros(o_ref.shape, F32)
        h_ref[0] = _gate_up(n_ref[...], wg_ref, wu_ref)

    @pl.when((j > 0) & (j < last))
    def _():
        o_ref[...] += jnp.dot(h_ref[(j - 1) % 2], wd_ref[...], preferred_element_type=F32)
        h_ref[j % 2] = _gate_up(n_ref[...], wg_ref, wu_ref)

    @pl.when(j == last)
    def _():
        y = o_ref[...] + jnp.dot(h_ref[(j - 1) % 2], wd_ref[...], preferred_element_type=F32)
        if tail:
            wg_t, wu_t, wd_t = tail
            y = y + jnp.dot(_gate_up(n_ref[...], wg_t, wu_t), wd_t[...], preferred_element_type=F32)
        out = x_ref[...] + (0.5 * gt_ref[...]) * y
        o_ref[...] = out
        if has_next:
            g2, sh2, sc2 = nxt
            nxt_ref[...] = _rms_mod(out, g2[...], sh2[...], sc2[...]).astype(nxt_ref.dtype)


def _ffn(x, g, sh, sc, gt, wg, wu, wd, next_norm=None):
    s, d = x.shape
    f = wg.shape[1]
    tm = min(FFN_TM, s)
    n_main = f // FFN_TF
    f_main = n_main * FFN_TF
    row = pl.BlockSpec((1, d), lambda i, j: (0, 0))
    assert n_main >= 1
    in_specs = [pl.BlockSpec((tm, d), lambda i, j: (i, 0)), row, row, row, row,
                pl.BlockSpec((d, FFN_TF), lambda i, j: (0, jnp.minimum(j, n_main - 1))),
                pl.BlockSpec((d, FFN_TF), lambda i, j: (0, jnp.minimum(j, n_main - 1))),
                pl.BlockSpec((FFN_TF, d), lambda i, j: (jnp.maximum(j - 1, 0), 0))]
    args = [x, g, sh, sc, gt, wg, wu, wd]
    if f_main < f:
        assert (f - f_main) % LANES == 0
        args += [wg[:, f_main:], wu[:, f_main:], wd[f_main:]]
        once = dict(pipeline_mode=pl.Buffered(1))
        in_specs += [pl.BlockSpec((d, f - f_main), lambda i, j: (0, 0), **once),
                     pl.BlockSpec((d, f - f_main), lambda i, j: (0, 0), **once),
                     pl.BlockSpec((f - f_main, d), lambda i, j: (0, 0), **once)]
    out_specs = pl.BlockSpec((tm, d), lambda i, j: (i, 0))
    out_shape = jax.ShapeDtypeStruct((s, d), F32)
    if next_norm is not None:
        args += list(next_norm)
        in_specs += [row, row, row]
        out_specs = [out_specs, pl.BlockSpec((tm, d), lambda i, j: (i, 0))]
        out_shape = [out_shape, jax.ShapeDtypeStruct((s, d), BF16)]
    return pl.pallas_call(
        functools.partial(_ffn_kernel, has_tail=f_main < f, has_next=next_norm is not None),
        grid=(s // tm, n_main + 1),
        in_specs=in_specs,
        out_specs=out_specs,
        out_shape=out_shape,
        scratch_shapes=[pltpu.VMEM((tm, d), BF16), pltpu.VMEM((2, tm, FFN_TF), BF16)],
        compiler_params=pltpu.CompilerParams(dimension_semantics=("parallel", "arbitrary"),
                                             vmem_limit_bytes=FFN_VMEM_LIMIT),
        name="ffn",
    )(*args)


def _group_rms(z, gain, post_scale):
    outs = []
    for gidx in range(z.shape[1] // LANES):
        cols = slice(gidx * LANES, (gidx + 1) * LANES)
        zg = z[:, cols]
        ms = jnp.mean(zg * zg, axis=-1, keepdims=True)
        y = (zg * lax.rsqrt(ms + EPS)) * gain[:, cols]
        if post_scale != 1.0:
            y = y * post_scale
        outs.append(y)
    return jnp.concatenate(outs, axis=-1)


def _proj_kernel(n_ref, w_ref, *rest, mode):
    o_ref = rest[-1]
    z = jnp.dot(n_ref[...], w_ref[...], preferred_element_type=F32)
    if mode == "gelu":
        out = _gelu_tanh(z)
    elif mode == "gelu_gnorm":
        out = _group_rms(_gelu_tanh(z), rest[0][...], 1.0)
    elif mode == "sigmoid":
        out = jax.nn.sigmoid(z)
    else:
        assert mode == "none"
        out = z
    o_ref[...] = out.astype(o_ref.dtype)


def _proj(n, w, mode, gain=None):
    s, d = n.shape
    nout = w.shape[1]
    in_specs = [pl.BlockSpec((PROJ_TM, d), lambda i, j: (i, 0)),
                pl.BlockSpec((d, PROJ_TN), lambda i, j: (0, j))]
    args = [n, w]
    if gain is not None:
        in_specs.append(pl.BlockSpec((1, PROJ_TN), lambda i, j: (0, j)))
        args.append(gain)
    return pl.pallas_call(
        functools.partial(_proj_kernel, mode=mode),
        grid=(s // PROJ_TM, nout // PROJ_TN),
        in_specs=in_specs,
        out_specs=pl.BlockSpec((PROJ_TM, PROJ_TN), lambda i, j: (i, j)),
        out_shape=jax.ShapeDtypeStruct((s, nout), BF16),
        compiler_params=_cparams(("parallel", "arbitrary")),
        name="proj_" + mode,
    )(*args)


def _split_hi_lo(pos):
    lo = pos & (LANES - 1)
    return pos - lo, lo


def _piece_slot(idx, k):
    return (idx >= k * N_PIECES) & (idx < (k + 1) * N_PIECES)


def _attn_q_kernel(n_ref, w_ref, gain_ref, cpos_ref, o_ref, *, post_scale):
    tm = n_ref.shape[0]
    y = _group_rms(jnp.dot(n_ref[...], w_ref[...], preferred_element_type=F32),
                   gain_ref[...], post_scale)
    col = lax.broadcasted_iota(I32, (tm, AUG), 1)
    t_hi, t_lo = _split_hi_lo(lax.broadcasted_iota(I32, (tm, AUG), 0) & (ATT_TQ - 1))
    tab = jnp.where(_piece_slot(col, 2), t_hi, jnp.where(_piece_slot(col, 3), t_lo, 0)).astype(F32)
    hw = HEAD_DIM + AUG
    for h in range(ATTN_HEADS):
        o_ref[:, h * hw:h * hw + HEAD_DIM] = y[:, h * HEAD_DIM:(h + 1) * HEAD_DIM].astype(o_ref.dtype)
        o_ref[:, h * hw + HEAD_DIM:(h + 1) * hw] = (tab + cpos_ref[h:h + 1, :]).astype(o_ref.dtype)


def _attn_kt_kernel(n_ref, w_ref, gain_ref, cneg_ref, o_ref):
    tm = n_ref.shape[0]
    y = _group_rms(jnp.dot(n_ref[...], w_ref[...], preferred_element_type=F32), gain_ref[...], 1.0)
    yt = jnp.transpose(y)
    row = lax.broadcasted_iota(I32, (AUG, tm), 0)
    pos = pl.program_id(0) * tm + lax.broadcasted_iota(I32, (AUG, tm), 1)
    u_hi, u_lo = _split_hi_lo(pos & (ATT_TK - 1))
    tab = jnp.where(_piece_slot(row, 0), u_hi, jnp.where(_piece_slot(row, 1), u_lo, 0)).astype(F32)
    hw = HEAD_DIM + AUG
    for h in range(ATTN_HEADS):
        o_ref[h * hw:h * hw + HEAD_DIM, :] = yt[h * HEAD_DIM:(h + 1) * HEAD_DIM, :].astype(o_ref.dtype)
        o_ref[h * hw + HEAD_DIM:(h + 1) * hw, :] = (tab + cneg_ref[h]).astype(o_ref.dtype)


def _idx_q_kernel(n_ref, w_ref, o_ref):
    zt = jnp.transpose(jnp.dot(n_ref[...], w_ref[...], preferred_element_type=F32))
    for qt in range(n_ref.shape[0] // ATT_TQ):
        for h in range(IDX_HEADS):
            o_ref[qt, :, h * ATT_TQ:(h + 1) * ATT_TQ] = zt[
                h * IDX_DIM:(h + 1) * IDX_DIM, qt * ATT_TQ:(qt + 1) * ATT_TQ].astype(o_ref.dtype)


def _attn_proj(kernel, n, w, extra, extra_specs, out_shape, out_spec, name):
    s, d = n.shape
    return pl.pallas_call(
        kernel,
        grid=(s // PROJ_TM,),
        in_specs=[pl.BlockSpec((PROJ_TM, d), lambda i: (i, 0)),
                  pl.BlockSpec(w.shape, lambda i: (0, 0))] + extra_specs,
        out_specs=out_spec,
        out_shape=out_shape,
        compiler_params=_cparams(("parallel",)),
        name=name,
    )(n, w, *extra)


def _kidx_kernel(n_ref, w_ref, g_ref, b_ref, ki_ref, wit_ref):
    z = jnp.dot(n_ref[...], w_ref[...], preferred_element_type=F32)
    ki = z[:, :IDX_DIM]
    mu = jnp.mean(ki, axis=-1, keepdims=True)
    var = jnp.mean(jnp.square(ki - mu), axis=-1, keepdims=True)
    y = (ki - mu) * lax.rsqrt(var + EPS)
    ki_ref[...] = (y * g_ref[...] + b_ref[...]).astype(ki_ref.dtype)
    wit_ref[...] = jnp.transpose(z)[IDX_DIM:IDX_DIM + IDX_HEADS, :] * (IDX_HEADS ** -0.5 * IDX_DIM ** -0.5)


def _kidx(n, w, g, b):
    s, d = n.shape
    tm = 512
    return pl.pallas_call(
        _kidx_kernel,
        grid=(s // tm,),
        in_specs=[pl.BlockSpec((tm, d), lambda i: (i, 0)),
                  pl.BlockSpec((d, LANES), lambda i: (0, 0)),
                  pl.BlockSpec((1, IDX_DIM), lambda i: (0, 0)),
                  pl.BlockSpec((1, IDX_DIM), lambda i: (0, 0))],
        out_specs=[pl.BlockSpec((tm, IDX_DIM), lambda i: (i, 0)),
                   pl.BlockSpec((IDX_HEADS, tm), lambda i: (0, i))],
        out_shape=[jax.ShapeDtypeStruct((s, IDX_DIM), BF16),
                   jax.ShapeDtypeStruct((IDX_HEADS, s), F32)],
        compiler_params=_cparams(("parallel",)),
        name="proj_kidx",
    )(n, w, g, b)


def _sgu_kernel(u_ref, v_ref, ws_ref, bt_ref, o_ref):
    tm = u_ref.shape[0]
    r = lax.broadcasted_iota(I32, (CHUNK, CHUNK), 0)
    c = lax.broadcasted_iota(I32, (CHUNK, CHUNK), 1)
    causal = c <= r
    for g in range(SGU_GROUPS):
        cols = slice(g * LANES, (g + 1) * LANES)
        w = jnp.where(causal, ws_ref[g], 0.0).astype(BF16)
        bcol = bt_ref[:, g:g + 1]
        for ch in range(tm // CHUNK):
            rows = slice(ch * CHUNK, (ch + 1) * CHUNK)
            sv = jnp.dot(w, v_ref[rows, cols], preferred_element_type=F32) + bcol
            o_ref[rows, cols] = (u_ref[rows, cols].astype(F32) * sv).astype(o_ref.dtype)


def _sgu(u, v, ws, bt):
    s, wdt = u.shape
    return pl.pallas_call(
        _sgu_kernel,
        grid=(s // SGU_TM,),
        in_specs=[pl.BlockSpec((SGU_TM, wdt), lambda i: (i, 0)),
                  pl.BlockSpec((SGU_TM, wdt), lambda i: (i, 0)),
                  pl.BlockSpec((SGU_GROUPS, CHUNK, CHUNK), lambda i: (0, 0, 0)),
                  pl.BlockSpec((CHUNK, SGU_GROUPS), lambda i: (0, 0))],
        out_specs=pl.BlockSpec((SGU_TM, wdt), lambda i: (i, 0)),
        out_shape=jax.ShapeDtypeStruct((s, wdt), BF16),
        compiler_params=_cparams(("parallel",)),
        name="sgu",
    )(u, v, ws, bt)


def _tile_lanes(x, n):
    return x if n == 1 else jnp.concatenate([x] * n, axis=1)


def _sublane_all(x, op):
    shift = SUBLANES // 2
    while shift:
        x = op(x, pltpu.roll(x, shift, 0))
        shift //= 2
    return x


def _count_keys(score_ref, n_blocks, pred):
    tq = score_ref.shape[1]

    def body(it, acc):
        off = pl.multiple_of(it * CNT_BLK, CNT_BLK)
        blk = score_ref[pl.ds(off, CNT_BLK), :].reshape(CNT_BLK // SUBLANES, SUBLANES, tq)
        ind = jnp.where(pred(blk, off), 1, 0)
        return acc + ind.reshape(-1, CNT_ACC, SUBLANES, tq).sum(axis=0)

    acc = lax.fori_loop(0, n_blocks, body, jnp.zeros((CNT_ACC, SUBLANES, tq), I32))
    tot = jnp.sum(acc.sum(axis=0).astype(F32), axis=0, keepdims=True)
    return jnp.broadcast_to(tot, (SUBLANES, tq)).astype(I32)


def _attn_kernel(itab_ref, jtab_ref, qit_ref, wit_ref, ki_ref, q_ref, kt_ref, v_ref, o_ref,
                 score_ref, r_ref, gmax_ref, thr_ref, m_ref, acc_ref, *, alibi_c, topk, idx_bits):
    i = itab_ref[pl.program_id(0)]
    j = jtab_ref[pl.program_id(0)]
    tq, tk = ATT_TQ, ATT_TK
    hw = HEAD_DIM + AUG
    q0 = i * tq
    n_kv = (q0 + tq + tk - 1) // tk
    n_blk = (q0 + tq + CNT_BLK - 1) // CNT_BLK

    @pl.when(j == 0)
    def _index_and_threshold():
        m_ref[...] = jnp.full(m_ref.shape, NEG, F32)
        acc_ref[...] = jnp.zeros(acc_ref.shape, F32)

        def index_chunk(off, n_sub):
            rows = n_sub * IDX_TK
            r_ref[0:rows, :] = jnp.dot(ki_ref[pl.ds(off, rows), :], qit_ref[0],
                                       preferred_element_type=F32)
            for sub in range(n_sub):
                r0 = sub * IDX_TK
                s_idx = off + r0 + lax.broadcasted_iota(I32, (IDX_TK, LANES), 0)
                for g in range(tq // LANES):
                    lanes = slice(g * LANES, (g + 1) * LANES)
                    t_idx = q0 + g * LANES + lax.broadcasted_iota(I32, (IDX_TK, LANES), 1)
                    acc = jnp.zeros((IDX_TK, LANES), F32)
                    for h in range(IDX_HEADS):
                        rr = r_ref[r0:r0 + IDX_TK, h * tq + g * LANES:h * tq + (g + 1) * LANES]
                        acc = acc + jnp.maximum(rr, 0.0) * wit_ref[h:h + 1, lanes]
                    score = jnp.where(s_idx <= t_idx, acc, -jnp.inf)
                    score_ref[pl.ds(pl.multiple_of(off + r0, IDX_TK), IDX_TK), lanes] = score
                    gmax_ref[:, lanes] = jnp.maximum(gmax_ref[:, lanes], score)

        def chunk_body(c, carry):
            index_chunk(pl.multiple_of(c * (IDX_SUB * IDX_TK), IDX_SUB * IDX_TK), IDX_SUB)
            return carry

        gmax_ref[...] = jnp.full(gmax_ref.shape, -jnp.inf, F32)
        n_idx = (q0 + tq) // IDX_TK
        lax.fori_loop(0, n_idx // IDX_SUB, chunk_body, 0)
        for rest in range(1, IDX_SUB):
            @pl.when(n_idx % IDX_SUB == rest)
            def _(rest=rest):
                index_chunk(pl.multiple_of((n_idx - rest) * IDX_TK, IDX_TK), rest)

        def fill_body(f, carry):
            off = pl.multiple_of(q0 + tq + f * tq, tq)
            score_ref[pl.ds(off, tq), :] = jnp.full((tq, tq), -jnp.inf, F32)
            return carry

        lax.fori_loop(0, (n_blk * CNT_BLK - (q0 + tq)) // tq, fill_body, 0)

        t_row = q0 + lax.broadcasted_iota(I32, (SUBLANES, tq), 1)
        kq = jnp.minimum(topk, t_row + 1)

        g = gmax_ref[...].reshape(IDX_TK // SUBLANES, SUBLANES, tq)
        hi0 = _sublane_all(g.max(axis=0), jnp.maximum)
        lo0 = _sublane_all(jnp.where(g == -jnp.inf, jnp.inf, g).min(axis=0), jnp.minimum)

        def bis_cond(st):
            it, _, _, _, _, done = st
            return jnp.logical_and(it < BISECT_CAP, done == 0)

        def bis_body(st):
            it, lo, hi, cnt, fin, _ = st
            settled = (cnt == kq) | (fin != 0)
            done = (jnp.min(jnp.where(settled, 1.0, 0.0)) > 0.5).astype(I32)
            mid = 0.5 * lo + 0.5 * hi
            stuck = (mid <= lo) | (mid >= hi)
            cand = jnp.where(stuck, hi, mid)
            tot = _count_keys(score_ref, n_blk, lambda blk, off: blk >= cand[None])
            take = (tot >= kq) & (fin == 0)
            lo = jnp.where(take, cand, lo)
            cnt = jnp.where(take, tot, cnt)
            hi = jnp.where(take | (fin != 0), hi, cand)
            fin = jnp.where(stuck, 1, fin)
            return it + 1, lo, hi, cnt, fin, done

        empty = lo0 > hi0
        _, thr, _, cnt, _, _ = lax.while_loop(
            bis_cond, bis_body,
            (jnp.int32(0), lo0, hi0, jnp.where(empty, 0, -1), jnp.where(empty, 1, 0), jnp.int32(0)))
        thr_ref[...] = thr
        cnt = lax.cond(jnp.min(cnt.astype(F32)) < 0.0,
                       lambda: jnp.where(cnt < 0, _count_keys(
                           score_ref, n_blk, lambda blk, off: blk >= thr[None]), cnt),
                       lambda: cnt)

        @pl.when(jnp.max(jnp.where(cnt > kq, 1.0, 0.0)) > 0.5)
        def _break_ties():
            def row_idx(off):
                return off + lax.broadcasted_iota(I32, (CNT_BLK, tq), 0).reshape(
                    CNT_BLK // SUBLANES, SUBLANES, tq)

            above = _count_keys(score_ref, n_blk, lambda blk, off: blk > thr[None])
            need = kq - above

            def cut_body(b, p):
                c = p + lax.shift_left(jnp.int32(1), idx_bits - 1 - b)
                below = _count_keys(
                    score_ref, n_blk, lambda blk, off: (blk == thr[None]) & (row_idx(off) < c[None]))
                return jnp.where(below < need, c, p)

            p = lax.fori_loop(0, idx_bits, cut_body, jnp.zeros((SUBLANES, tq), I32))

            def drop_body(it, carry):
                off = pl.multiple_of(it * CNT_BLK, CNT_BLK)
                blk = score_ref[pl.ds(off, CNT_BLK), :].reshape(CNT_BLK // SUBLANES, SUBLANES, tq)
                new = jnp.where((blk == thr[None]) & (row_idx(off) > p[None]), -jnp.inf, blk)
                score_ref[pl.ds(off, CNT_BLK), :] = new.reshape(CNT_BLK, tq)
                return carry

            lax.fori_loop(0, n_blk, drop_body, 0)

    k0 = pl.multiple_of(j * tk, tk)
    nl = tk // LANES
    sel_t = score_ref[pl.ds(k0, tk), :] >= thr_ref[0:1, :]
    bias = jnp.transpose(jnp.where(sel_t, 0.0, NEG))
    tile_dist = jnp.full((tq, LANES), k0 - q0, I32).astype(F32)
    ones_col = (lax.broadcasted_iota(I32, (tk, LANES), 1) == 0).astype(BF16)

    def logits(h):
        s = bias + jnp.dot(q_ref[:, h * hw:(h + 1) * hw], kt_ref[h * hw:(h + 1) * hw, :],
                           preferred_element_type=F32)
        return s, jnp.max(s, axis=-1, keepdims=True)

    def probs(h, s, s_max):
        shift = tile_dist * alibi_c[h]
        m_prev = m_ref[h]
        m_new = jnp.maximum(m_prev, s_max + shift)
        m_ref[h] = m_new
        p = jnp.exp2(s - _tile_lanes(m_new - shift, nl)).astype(BF16)
        return p, jnp.exp2(m_prev - m_new)

    def accumulate(h, p, alpha):
        v_aug = jnp.concatenate([v_ref[:, h * HEAD_DIM:(h + 1) * HEAD_DIM], ones_col], axis=1)
        acc_ref[h] = _tile_lanes(alpha, 2) * acc_ref[h] + jnp.dot(
            p, v_aug, preferred_element_type=F32)

    st_a, st_b = {}, {}
    for step in range(ATTN_HEADS + 2):
        if 0 <= step - 2 < ATTN_HEADS:
            accumulate(step - 2, *st_b.pop(step - 2))
        if step < ATTN_HEADS:
            st_a[step] = logits(step)
        if 0 <= step - 1 < ATTN_HEADS:
            st_b[step - 1] = probs(step - 1, *st_a.pop(step - 1))

    @pl.when(j == n_kv - 1)
    def _finish():
        for h in range(ATTN_HEADS):
            acc = acc_ref[h]
            o_ref[:, h * HEAD_DIM:(h + 1) * HEAD_DIM] = (
                acc[:, :HEAD_DIM] / acc[:, HEAD_DIM:HEAD_DIM + 1]).astype(o_ref.dtype)


def _alibi_pieces():
    out = []
    for h in range(ATTN_HEADS):
        rest = np.float32(LOG2E * 2.0 ** (-8.0 * (h + 1) / ATTN_HEADS))
        pieces = []
        for _ in range(N_PIECES):
            piece = np.float32(rest.astype(BF16))
            pieces.append(piece)
            rest = np.float32(rest - piece)
        out.append(pieces)
    return np.asarray(out, np.float32)


def _alibi_tables():
    pieces = _alibi_pieces()
    n = N_PIECES
    assert 4 * n <= AUG
    cpos = np.zeros((ATTN_HEADS, AUG), np.float32)
    cneg = np.zeros((ATTN_HEADS, AUG, 1), np.float32)
    cpos[:, 0 * n:1 * n] = pieces
    cpos[:, 1 * n:2 * n] = pieces
    cneg[:, 2 * n:3 * n, 0] = -pieces
    cneg[:, 3 * n:4 * n, 0] = -pieces
    alibi_c = tuple(float(p.sum(dtype=np.float32)) for p in pieces)
    return jnp.asarray(cpos), jnp.asarray(cneg), alibi_c


def _attention(qit, wit, ki, q_aug, kt_aug, v, alibi_c):
    s = q_aug.shape[0]
    tq, tk = ATT_TQ, ATT_TK
    hw = HEAD_DIM + AUG
    assert s % CNT_BLK == 0 and CNT_BLK % tk == 0 and CNT_BLK % tq == 0 and tq % LANES == 0
    assert tq & (tq - 1) == 0 and tk & (tk - 1) == 0 and PROJ_TM % tq == 0
    topk = min(TOPK_MAX, s // 4)
    idx_bits = max(1, (s - 1).bit_length())
    assert IDX_TK >= topk and tq % IDX_TK == 0

    steps = [(i, j) for i in range(s // tq) for j in range(((i + 1) * tq + tk - 1) // tk)]
    itab = jnp.asarray(np.array([p[0] for p in steps], np.int32))
    jtab = jnp.asarray(np.array([p[1] for p in steps], np.int32))

    grid_spec = pltpu.PrefetchScalarGridSpec(
        num_scalar_prefetch=2,
        grid=(len(steps),),
        in_specs=[pl.BlockSpec((1, IDX_DIM, IDX_HEADS * tq), lambda t, it, jt: (it[t], 0, 0)),
                  pl.BlockSpec((IDX_HEADS, tq), lambda t, it, jt: (0, it[t])),
                  pl.BlockSpec((s, IDX_DIM), lambda t, it, jt: (0, 0)),
                  pl.BlockSpec((tq, ATTN_HEADS * hw), lambda t, it, jt: (it[t], 0)),
                  pl.BlockSpec((ATTN_HEADS * hw, tk), lambda t, it, jt: (0, jt[t])),
                  pl.BlockSpec((tk, ATTN_WIDTH), lambda t, it, jt: (jt[t], 0))],
        out_specs=pl.BlockSpec((tq, ATTN_WIDTH), lambda t, it, jt: (it[t], 0)),
        scratch_shapes=[pltpu.VMEM((s, tq), F32),
                        pltpu.VMEM((IDX_SUB * IDX_TK, IDX_HEADS * tq), F32),
                        pltpu.VMEM((IDX_TK, tq), F32),
                        pltpu.VMEM((SUBLANES, tq), F32),
                        pltpu.VMEM((ATTN_HEADS, tq, LANES), F32),
                        pltpu.VMEM((ATTN_HEADS, tq, 2 * HEAD_DIM), F32)])
    return pl.pallas_call(
        functools.partial(_attn_kernel, alibi_c=alibi_c, topk=topk, idx_bits=idx_bits),
        grid_spec=grid_spec,
        out_shape=jax.ShapeDtypeStruct((s, ATTN_WIDTH), BF16),
        compiler_params=_cparams(("arbitrary",)),
        name="dsa_attention",
    )(itab, jtab, qit, wit, ki, q_aug, kt_aug, v)


def _merge_kernel(ya_ref, yb_ref, sg_a_ref, sg_b_ref, x_ref, gt_ref, wa_ref, wb_ref, wo_ref, o_ref):
    a = jnp.dot(ya_ref[...], wa_ref[...], preferred_element_type=F32)
    b = jnp.dot(yb_ref[...], wb_ref[...], preferred_element_type=F32)
    merged = sg_a_ref[...].astype(F32) * a + sg_b_ref[...].astype(F32) * b
    o_ref[...] = x_ref[...] + gt_ref[...] * jnp.dot(
        merged.astype(BF16), wo_ref[...], preferred_element_type=F32)


def _merge(ya, yb, sg, x, gt, wa, wb, wo):
    s, d = x.shape
    tm = MERGE_TM
    wdt = ya.shape[1]
    const = lambda i: (0, 0)
    return pl.pallas_call(
        _merge_kernel,
        grid=(s // tm,),
        in_specs=[pl.BlockSpec((tm, wdt), lambda i: (i, 0)),
                  pl.BlockSpec((tm, wdt), lambda i: (i, 0)),
                  pl.BlockSpec((tm, d), lambda i: (i, 0)),
                  pl.BlockSpec((tm, d), lambda i: (i, 1)),
                  pl.BlockSpec((tm, d), lambda i: (i, 0)),
                  pl.BlockSpec((1, d), const),
                  pl.BlockSpec((wdt, d), const, pipeline_mode=pl.Buffered(1)),
                  pl.BlockSpec((wdt, d), const, pipeline_mode=pl.Buffered(1)),
                  pl.BlockSpec((d, d), const, pipeline_mode=pl.Buffered(1))],
        out_specs=pl.BlockSpec((tm, d), lambda i: (i, 0)),
        out_shape=jax.ShapeDtypeStruct((s, d), F32),
        compiler_params=_cparams(("parallel",)),
        name="merge",
    )(ya, yb, sg, sg, x, gt, wa, wb, wo)


def _pad_cols(w, n):
    return jnp.pad(w, ((0, 0), (0, n - w.shape[1])))


def _layer(x, c, w_ada, b_ada, g_norm1, w1_gate, w1_up, w1_down, g_norm2, w_in, g_sgu, w_spatial,
           b_spatial, g_q, g_k, g_kidx, b_kidx, w_branch_a, w_branch_b, w_out, g_norm3,
           w2_gate, w2_up, w2_down):
    s, d = x.shape

    ada = _ada(c.reshape(d, 1), w_ada, b_ada.reshape(1, -1))
    sh1, sc1, gt1, sh2, sc2, gt2, sh3, sc3, gt3 = [ada[:, k * d:(k + 1) * d] for k in range(N_ADA)]

    def ffn_weights(wg, wu, wd):
        return wg.astype(BF16), wu.astype(BF16), wd.astype(BF16)

    x1, n2 = _ffn(x, g_norm1.reshape(1, d), sh1, sc1, gt1, *ffn_weights(w1_gate, w1_up, w1_down),
                  next_norm=(g_norm2.reshape(1, d), sh2, sc2))

    wi_b = w_in.astype(BF16)
    o_u, o_v, o_q = 0, SGU_WIDTH, 2 * SGU_WIDTH
    o_k, o_vv = o_q + ATTN_WIDTH, o_q + 2 * ATTN_WIDTH
    o_qi = o_vv + ATTN_WIDTH
    o_ki = o_qi + IDX_HEADS * IDX_DIM
    o_g = o_ki + IDX_DIM + IDX_HEADS

    ug = _proj(n2, wi_b[:, o_u:o_v], "gelu")
    vn = _proj(n2, wi_b[:, o_v:o_q], "gelu_gnorm", gain=g_sgu.reshape(1, SGU_WIDTH))
    vv = _proj(n2, wi_b[:, o_vv:o_qi], "none")
    sg = _proj(n2, wi_b[:, o_g:], "sigmoid")
    ki, wit = _kidx(n2, _pad_cols(wi_b[:, o_ki:o_g], LANES),
                    g_kidx.reshape(1, IDX_DIM), b_kidx.reshape(1, IDX_DIM))

    cpos, cneg, alibi_c = _alibi_tables()
    hw = HEAD_DIM + AUG
    row = lambda i: (0, 0)
    q_aug = _attn_proj(
        functools.partial(_attn_q_kernel, post_scale=HEAD_DIM ** -0.5 * LOG2E),
        n2, wi_b[:, o_q:o_k], [jnp.tile(g_q, ATTN_HEADS).reshape(1, -1), cpos],
        [pl.BlockSpec((1, ATTN_WIDTH), row), pl.BlockSpec((ATTN_HEADS, AUG), row)],
        jax.ShapeDtypeStruct((s, ATTN_HEADS * hw), BF16),
        pl.BlockSpec((PROJ_TM, ATTN_HEADS * hw), lambda i: (i, 0)), "proj_attn_q")
    kt_aug = _attn_proj(
        _attn_kt_kernel, n2, wi_b[:, o_k:o_vv], [jnp.tile(g_k, ATTN_HEADS).reshape(1, -1), cneg],
        [pl.BlockSpec((1, ATTN_WIDTH), row), pl.BlockSpec((ATTN_HEADS, AUG, 1), lambda i: (0, 0, 0))],
        jax.ShapeDtypeStruct((ATTN_HEADS * hw, s), BF16),
        pl.BlockSpec((ATTN_HEADS * hw, PROJ_TM), lambda i: (0, i)), "proj_attn_kt")
    qit = _attn_proj(
        _idx_q_kernel, n2, wi_b[:, o_qi:o_ki], [], [],
        jax.ShapeDtypeStruct((s // ATT_TQ, IDX_DIM, IDX_HEADS * ATT_TQ), BF16),
        pl.BlockSpec((PROJ_TM // ATT_TQ, IDX_DIM, IDX_HEADS * ATT_TQ), lambda i: (i, 0, 0)), "proj_idx_q")

    y_a = _sgu(ug, vn, w_spatial, jnp.transpose(b_spatial))
    y_b = _attention(qit, wit, ki, q_aug, kt_aug, vv, alibi_c)

    x2 = _merge(y_a, y_b, sg, x1, gt2, w_branch_a.astype(BF16), w_branch_b.astype(BF16),
                w_out.astype(BF16))

    return _ffn(x2, g_norm3.reshape(1, d), sh3, sc3, gt3, *ffn_weights(w2_gate, w2_up, w2_down))


def kernel(x, c, w_ada, b_ada, g_norm1, w1_gate, w1_up, w1_down, g_norm2, w_in, g_sgu, w_spatial,
           b_spatial, g_q, g_k, g_kidx, b_kidx, w_branch_a, w_branch_b, w_out, g_norm3,
           w2_gate, w2_up, w2_down):
    batch, depth = x.shape[0], w_ada.shape[0]
    outs = []
    for b in range(batch):
        xb = x[b]
        for l in range(depth):
            xb = _layer(xb, c[b], w_ada[l], b_ada[l], g_norm1[l], w1_gate[l], w1_up[l], w1_down[l],
                        g_norm2[l], w_in[l], g_sgu[l], w_spatial[l], b_spatial[l], g_q[l], g_k[l],
                        g_kidx[l], b_kidx[l], w_branch_a[l], w_branch_b[l], w_out[l], g_norm3[l],
                        w2_gate[l], w2_up[l], w2_down[l])
        outs.append(xb[None])
    return outs[0] if batch == 1 else jnp.concatenate(outs)
```
